```python
import math, functools
import jax, jax.numpy as jnp
from jax import lax
import numpy as np

D_MODEL = 1024
BATCH = 8
SEQ = 2048
DEPTH = 2
DEC_BATCH = 128
DEC_SEQ = 1
PAST_LEN = 16384
PAGE_SIZE = 128

N_HEADS = 8
Q_LORA_RANK = D_MODEL // 4
KV_LORA_RANK = D_MODEL // 8
QK_NOPE_DIM = 64
QK_ROPE_DIM = 32
QK_DIM = QK_NOPE_DIM + QK_ROPE_DIM
V_HEAD_DIM = 128
ROPE_THETA = 10000.0
Q_BLOCK = 128
SM_SCALE = QK_DIM ** -0.5
CHUNK = 128
SG_GROUP_DIM = 128
SG_HALF = 3 * D_MODEL // 2
SG_GROUPS = SG_HALF // SG_GROUP_DIM
PEER_HEADS = 8
N_KEYS = 128
N_EXPERTS = N_KEYS * N_KEYS
PEER_TOPK = 16
PEER_QDIM = 256
PEER_HALF = PEER_QDIM // 2
PEER_BLOCK = 128
RMS_EPS = 1e-6

kernel_name = "hybrid_mla_chunk_sgu_peer_adaln_step"


def rms_norm(x, g):
    xf = x.astype(jnp.float32)
    y = xf * lax.rsqrt(jnp.mean(xf * xf, axis=-1, keepdims=True) + RMS_EPS)
    return y.astype(x.dtype) * g


def modulation(c, w_mod, b_mod):
    m = (jax.nn.silu(c) @ w_mod + b_mod).reshape(c.shape[0], 6, 1, D_MODEL)
    return [m[:, j] for j in range(6)]


def modulate(h, shift, scale):
    return h * (1.0 + scale) + shift


def rope_tail(x, pos):
    half = QK_ROPE_DIM // 2
    inv_freq = ROPE_THETA ** (-jnp.arange(half, dtype=jnp.float32) * 2.0 / QK_ROPE_DIM)
    ang = pos.astype(jnp.float32)[:, None] * inv_freq[None, :]
    cos = jnp.cos(ang)[:, None, :].astype(x.dtype)
    sin = jnp.sin(ang)[:, None, :].astype(x.dtype)
    x_nope = x[..., :QK_NOPE_DIM]
    x1 = x[..., QK_NOPE_DIM:QK_NOPE_DIM + half]
    x2 = x[..., QK_NOPE_DIM + half:]
    return jnp.concatenate([x_nope, x1 * cos - x2 * sin, x2 * cos + x1 * sin], axis=-1)


def mla_in(h, pos, w_in, g_q_a, g_kv_a, w_uq, g_q):
    z = h @ w_in
    cq = rms_norm(z[..., :Q_LORA_RANK], g_q_a)
    ckv = rms_norm(z[..., Q_LORA_RANK:Q_LORA_RANK + KV_LORA_RANK], g_kv_a)
    kr = z[..., Q_LORA_RANK + KV_LORA_RANK:]
    q = (cq @ w_uq).reshape(h.shape[:2] + (N_HEADS, QK_DIM))
    q = rope_tail(rms_norm(q, g_q), pos)
    return q, ckv, kr


def mla_keys(ckv, kr, k_pos, w_uk, g_k):
    kn = jnp.einsum('btc,chd->bthd', ckv, w_uk)
    kr_h = jnp.broadcast_to(kr[:, :, None, :], kn.shape[:3] + (QK_ROPE_DIM,))
    k = rms_norm(jnp.concatenate([kn, kr_h], axis=-1), g_k)
    return rope_tail(k, k_pos)


def mla_attend(q, k, ckv, q_pos, k_pos, w_uv):
    s = jnp.einsum('bqhd,bthd->bhqt', q, k).astype(jnp.float32) * SM_SCALE
    s = jnp.where(k_pos[None, :] <= q_pos[:, None], s, -jnp.inf)
    p = jax.nn.softmax(s, axis=-1).astype(ckv.dtype)
    o_lat = jnp.einsum('bhqt,btc->bqhc', p, ckv)
    return jnp.einsum('bqhc,chv->bqhv', o_lat, w_uv)


def mla_prompt(h, w_in, g_q_a, g_kv_a, w_uq, g_q, w_uk, w_uv, g_k, w_o):
    B, S, _ = h.shape
    pos = jnp.arange(S)
    q, ckv, kr = mla_in(h, pos, w_in, g_q_a, g_kv_a, w_uq, g_q)
    k = mla_keys(ckv, kr, pos, w_uk, g_k)

    def block(i):
        start = i * Q_BLOCK
        q_i = lax.dynamic_slice_in_dim(q, start, Q_BLOCK, axis=1)
        q_pos = start + jnp.arange(Q_BLOCK)
        return mla_attend(q_i, k, ckv, q_pos, pos, w_uv)

    o = lax.map(block, jnp.arange(S // Q_BLOCK))
    o = jnp.moveaxis(o, 0, 1).reshape(B, S, N_HEADS * V_HEAD_DIM)
    return o @ w_o, ckv, kr


def mla_sample(h, cache_kv, cache_kr, page_table, w_in, g_q_a, g_kv_a, w_uq, g_q, w_uk, w_uv, g_k, w_o):
    Bd, Sd, _ = h.shape
    past = page_table.shape[1] * cache_kv.shape[1]
    q_pos = past + jnp.arange(Sd)
    k_pos = jnp.arange(past + Sd)
    q, ckv, kr = mla_in(h, q_pos, w_in, g_q_a, g_kv_a, w_uq, g_q)

    def one(args):
        q_b, ckv_b, kr_b, pages = args
        ckv_all = jnp.concatenate([cache_kv[pages].reshape(past, KV_LORA_RANK), ckv_b], axis=0)[None]
        kr_all = jnp.concatenate([cache_kr[pages].reshape(past, QK_ROPE_DIM), kr_b], axis=0)[None]
        k = mla_keys(ckv_all, kr_all, k_pos, w_uk, g_k)
        return mla_attend(q_b[None], k, ckv_all, q_pos, k_pos, w_uv)[0]

    o = lax.map(one, (q, ckv, kr, page_table))
    return o.reshape(Bd, Sd, N_HEADS * V_HEAD_DIM) @ w_o, ckv, kr


def sgu_mixer(h, w_in, b_in, g_v, w_s, b_s, w_out):
    B, S, _ = h.shape
    z = jax.nn.gelu(h @ w_in + b_in)
    u = z[..., :SG_HALF]
    v = rms_norm(z[..., SG_HALF:], g_v)
    pad = (-S) % CHUNK
    n_c = (S + pad) // CHUNK
    vc = jnp.pad(v, ((0, 0), (0, pad), (0, 0))).reshape(B, n_c, CHUNK, SG_GROUPS, SG_GROUP_DIM)
    w_causal = w_s * jnp.tril(jnp.ones((CHUNK, CHUNK), w_s.dtype))
    s = jnp.einsum('gts,bcsgd->bctgd', w_causal, vc) + b_s.T[:, :, None]
    s = s.reshape(B, n_c * CHUNK, SG_HALF)[:, :S]
    return (u * s) @ w_out, v


def peer(h, w_q, sub_keys, u_tab, v_tab):
    lead = h.shape[:-1]
    xf = h.reshape(-1, D_MODEL)
    n_tok = xf.shape[0]
    pad = (-n_tok) % PEER_BLOCK
    xb = jnp.pad(xf, ((0, pad), (0, 0))).reshape(-1, PEER_BLOCK, D_MODEL)

    def block(xt):
        q = (xt @ w_q).reshape(PEER_BLOCK, PEER_HEADS, 2, PEER_HALF)
        s1 = jnp.einsum('thk,nk->thn', q[:, :, 0], sub_keys[0]).astype(jnp.float32)
        s2 = jnp.einsum('thk,nk->thn', q[:, :, 1], sub_keys[1]).astype(jnp.float32)
        t1, i1 = lax.top_k(s1, PEER_TOPK)
        t2, i2 = lax.top_k(s2, PEER_TOPK)
        cand = (t1[..., :, None] + t2[..., None, :]).reshape(PEER_BLOCK, PEER_HEADS, PEER_TOPK * PEER_TOPK)
        top, ci = lax.top_k(cand, PEER_TOPK)
        e1 = jnp.take_along_axis(i1, ci // PEER_TOPK, axis=-1)
        e2 = jnp.take_along_axis(i2, ci % PEER_TOPK, axis=-1)
        idx = e1 * N_KEYS + e2
        g = jax.nn.softmax(top, axis=-1).astype(xt.dtype)
        act = jax.nn.gelu(jnp.einsum('thkd,td->thk', u_tab[idx], xt))
        return jnp.einsum('thk,thkd->td', g * act, v_tab[idx])

    out = lax.map(block, xb).reshape(-1, D_MODEL)[:n_tok]
    return out.reshape(lead + (D_MODEL,))


def setup_inputs(seed: int = 0) -> dict:
    key = jax.random.key(seed)
    ks = iter(jax.random.split(key, 40))

    def nrm(shape, scale):
        return jax.random.normal(next(ks), shape, jnp.float32) * scale

    n_mla = (DEPTH + 1) // 2
    n_sg = DEPTH // 2
    n_pages = PAST_LEN // PAGE_SIZE
    n_used = DEC_BATCH * n_pages
    n_pool = n_used + max(1, n_used // 4)
    page_table = jax.random.permutation(next(ks), n_pool)[:n_used].reshape(DEC_BATCH, n_pages).astype(jnp.int32)
    return {
        "x_prompt": nrm((BATCH, SEQ, D_MODEL), 1.0),
        "x_sample": nrm((DEC_BATCH, DEC_SEQ, D_MODEL), 1.0),
        "c_prompt": nrm((BATCH, D_MODEL), 1.0),
        "c_sample": nrm((DEC_BATCH, D_MODEL), 1.0),
        "cache_kv_latent": nrm((n_mla, n_pool, PAGE_SIZE, KV_LORA_RANK), 1.0),
        "cache_k_rope": nrm((n_mla, n_pool, PAGE_SIZE, QK_ROPE_DIM), 1.0),
        "page_table": page_table,
        "norm1_g": 1.0 + nrm((DEPTH, D_MODEL), 0.1),
        "norm2_g": 1.0 + nrm((DEPTH, D_MODEL), 0.1),
        "w_mod": nrm((DEPTH, D_MODEL, 6 * D_MODEL), D_MODEL ** -0.5),
        "b_mod": nrm((DEPTH, 6 * D_MODEL), 0.02),
        "mla_w_in": nrm((n_mla, D_MODEL, Q_LORA_RANK + KV_LORA_RANK + QK_ROPE_DIM), D_MODEL ** -0.5),
        "mla_g_q_a": 1.0 + nrm((n_mla, Q_LORA_RANK), 0.1),
        "mla_g_kv_a": 1.0 + nrm((n_mla, KV_LORA_RANK), 0.1),
        "mla_w_uq": nrm((n_mla, Q_LORA_RANK, N_HEADS * QK_DIM), Q_LORA_RANK ** -0.5),
        "mla_g_q": 1.0 + nrm((n_mla, QK_DIM), 0.1),
        "mla_w_uk": nrm((n_mla, KV_LORA_RANK, N_HEADS, QK_NOPE_DIM), KV_LORA_RANK ** -0.5),
        "mla_w_uv": nrm((n_mla, KV_LORA_RANK, N_HEADS, V_HEAD_DIM), KV_LORA_RANK ** -0.5),
        "mla_g_k": 1.0 + nrm((n_mla, QK_DIM), 0.1),
        "mla_w_o": nrm((n_mla, N_HEADS * V_HEAD_DIM, D_MODEL), (N_HEADS * V_HEAD_DIM) ** -0.5),
        "sg_w_in": nrm((n_sg, D_MODEL, 2 * SG_HALF), D_MODEL ** -0.5),
        "sg_b_in": nrm((n_sg, 2 * SG_HALF), 0.02),
        "sg_g_v": 1.0 + nrm((n_sg, SG_HALF), 0.1),
        "sg_w_s": nrm((n_sg, SG_GROUPS, CHUNK, CHUNK), CHUNK ** -0.5),
        "sg_b_s": 1.0 + nrm((n_sg, SG_GROUPS, CHUNK), 0.1),
        "sg_w_out": nrm((n_sg, SG_HALF, D_MODEL), SG_HALF ** -0.5),
        "peer_w_q": nrm((DEPTH, D_MODEL, PEER_HEADS * PEER_QDIM), D_MODEL ** -0.5),
        "peer_sub_keys": nrm((DEPTH, 2, N_KEYS, PEER_HALF), PEER_HALF ** -0.5),
        "peer_u": nrm((DEPTH, N_EXPERTS, D_MODEL), D_MODEL ** -0.5),
        "peer_v": nrm((DEPTH, N_EXPERTS, D_MODEL), PEER_HEADS ** -0.5),
    }


def reference(x_prompt, x_sample, c_prompt, c_sample, cache_kv_latent, cache_k_rope, page_table,
              norm1_g, norm2_g, w_mod, b_mod,
              mla_w_in, mla_g_q_a, mla_g_kv_a, mla_w_uq, mla_g_q, mla_w_uk, mla_w_uv, mla_g_k, mla_w_o,
              sg_w_in, sg_b_in, sg_g_v, sg_w_s, sg_b_s, sg_w_out,
              peer_w_q, peer_sub_keys, peer_u, peer_v):
    yp, ys = x_prompt, x_sample
    kv_p, kr_p, kv_s, kr_s, v_p, v_s = [], [], [], [], [], []
    for i in range(DEPTH):
        mp = modulation(c_prompt, w_mod[i], b_mod[i])
        ms = modulation(c_sample, w_mod[i], b_mod[i])
        hp = modulate(rms_norm(yp, norm1_g[i]), mp[0], mp[1])
        hs = modulate(rms_norm(ys, norm1_g[i]), ms[0], ms[1])
        if i % 2 == 0:
            a = i // 2
            w = (mla_w_in[a], mla_g_q_a[a], mla_g_kv_a[a], mla_w_uq[a], mla_g_q[a],
                 mla_w_uk[a], mla_w_uv[a], mla_g_k[a], mla_w_o[a])
            o_p, ckv_p, krp = mla_prompt(hp, *w)
            o_s, ckv_s, krs = mla_sample(hs, cache_kv_latent[a], cache_k_rope[a], page_table, *w)
            kv_p.append(ckv_p)
            kr_p.append(krp)
            kv_s.append(ckv_s)
            kr_s.append(krs)
        else:
            b = i // 2
            w = (sg_w_in[b], sg_b_in[b], sg_g_v[b], sg_w_s[b], sg_b_s[b], sg_w_out[b])
            o_p, vp = sgu_mixer(hp, *w)
            o_s, vs = sgu_mixer(hs, *w)
            start = ((yp.shape[1] - 1) // CHUNK) * CHUNK
            v_p.append(vp[:, start:])
            v_s.append(vs)
        yp = yp + mp[2] * o_p
        ys = ys + ms[2] * o_s
        pw = (peer_w_q[i], peer_sub_keys[i], peer_u[i], peer_v[i])
        yp = yp + mp[5] * peer(modulate(rms_norm(yp, norm2_g[i]), mp[3], mp[4]), *pw)
        ys = ys + ms[5] * peer(modulate(rms_norm(ys, norm2_g[i]), ms[3], ms[4]), *pw)
    new_kv_latent_prompt = jnp.stack(kv_p)
    new_k_rope_prompt = jnp.stack(kr_p)
    new_kv_latent_sample = jnp.stack(kv_s)
    new_k_rope_sample = jnp.stack(kr_s)
    new_sg_v_prompt = jnp.stack(v_p)
    new_sg_v_sample = jnp.stack(v_s)
    return (yp, ys, new_kv_latent_prompt, new_k_rope_prompt, new_kv_latent_sample, new_k_rope_sample, new_sg_v_prompt, new_sg_v_sample)
```

```python
import functools

import jax
import jax.numpy as jnp
from jax import lax
from jax.experimental import pallas as pl
from jax.experimental.pallas import tpu as pltpu

F32 = jnp.float32
BF16 = jnp.bfloat16

D_MODEL = 1024
BATCH = 8
SEQ = 2048
DEPTH = 2
DEC_BATCH = 128
PAST_LEN = 16384
PAGE_SIZE = 128
N_PAGES = PAST_LEN // PAGE_SIZE

N_HEADS = 8
Q_LORA_RANK = 256
KV_LORA_RANK = 128
QK_NOPE_DIM = 64
QK_ROPE_DIM = 32
QK_DIM = QK_NOPE_DIM + QK_ROPE_DIM
V_HEAD_DIM = 128
ROPE_THETA = 10000.0
SM_SCALE = QK_DIM ** -0.5
HEAD_PAD = 128
ROPE_HALF = QK_ROPE_DIM // 2

CHUNK = 128
SG_HALF = 1536
SG_GROUPS = 12

PEER_HEADS = 8
N_KEYS = 128
N_EXPERTS = N_KEYS * N_KEYS
PEER_TOPK = 16
RMS_EPS = 1e-6

NEG_BIG = -1e30
VMEM_LIMIT = 56 * 1024 * 1024


def _cparams(*sem):
    return pltpu.CompilerParams(dimension_semantics=sem, vmem_limit_bytes=VMEM_LIMIT)


def _rms(x):
    return x * lax.rsqrt(jnp.mean(x * x, axis=-1, keepdims=True) + RMS_EPS)


def _gelu(x):
    cdf = 0.5 * (1.0 + jnp.tanh(0.7978845608028654 * (x + 0.044715 * (x * x * x))))
    return x * cdf


def _dot(a, b):
    return jnp.dot(a, b, preferred_element_type=F32)


def _dot_nt(a, b):
    return lax.dot_general(a, b, (((1,), (1,)), ((), ())), preferred_element_type=F32)


def _full(shape):
    n = len(shape)
    return pl.BlockSpec(shape, lambda *_: (0,) * n)


def _mod_spec(arr, tiles_per_seq):
    return pl.BlockSpec((None,) + arr.shape[1:], lambda t, *_: (t // tiles_per_seq, 0, 0))


def _mod_kernel(c_ref, w_ref, b_ref, o_ref):
    c = c_ref[...]
    a = (c * jax.nn.sigmoid(c)).astype(BF16)
    o_ref[...] = _dot(a, w_ref[...].astype(BF16)) + b_ref[...]


def _modulation(c_all, w_mod, b_mod):
    r = c_all.shape[0]
    tn = 1536
    return pl.pallas_call(
        _mod_kernel,
        grid=(DEPTH, 6 * D_MODEL // tn),
        in_specs=[
            pl.BlockSpec((r, D_MODEL), lambda l, n: (0, 0)),
            pl.BlockSpec((None, D_MODEL, tn), lambda l, n: (l, 0, n)),
            pl.BlockSpec((None, 1, tn), lambda l, n: (l, 0, n)),
        ],
        out_specs=pl.BlockSpec((None, r, tn), lambda l, n: (l, 0, n)),
        out_shape=jax.ShapeDtypeStruct((DEPTH, r, 6 * D_MODEL), F32),
        compiler_params=_cparams("parallel", "parallel"),
        name="modulation",
    )(c_all, w_mod, b_mod.reshape(DEPTH, 1, 6 * D_MODEL))


def _rope(x, cos, sa, sb):
    return x * cos + pltpu.roll(x, HEAD_PAD - ROPE_HALF, 1) * sa + pltpu.roll(x, ROPE_HALF, 1) * sb


def _mla_pre_kernel(y_ref, sh_ref, sc_ref, g1_ref, win_ref, gqa_ref, gkva_ref, wuq_ref, gq_ref,
                    wuk_ref, gk_ref, cos_ref, sa_ref, sb_ref, q_ref, k_ref, ckv_ref, kr_ref):
    h = _rms(y_ref[...]) * g1_ref[...]
    h = h * (1.0 + sc_ref[...]) + sh_ref[...]
    z = _dot(h.astype(BF16), win_ref[...])
    cq = _rms(z[:, :Q_LORA_RANK]) * gqa_ref[...]
    ckv = _rms(z[:, Q_LORA_RANK:Q_LORA_RANK + KV_LORA_RANK]) * gkva_ref[...]
    krp = z[:, Q_LORA_RANK + KV_LORA_RANK:]
    ckv_ref[...] = ckv
    kr_ref[...] = krp
    q = _dot(cq.astype(BF16), wuq_ref[...])
    kn = _dot(ckv.astype(BF16), wuk_ref[...])
    cos, sa, sb = cos_ref[...], sa_ref[...], sb_ref[...]
    gq, gk = gq_ref[...], gk_ref[...]
    for hd in range(N_HEADS):
        sl = slice(hd * HEAD_PAD, (hd + 1) * HEAD_PAD)
        qh = q[:, sl]
        qh = qh * lax.rsqrt(jnp.sum(qh * qh, axis=-1, keepdims=True) / QK_DIM + RMS_EPS) * gq
        q_ref[:, sl] = _rope(qh, cos, sa, sb).astype(q_ref.dtype)
        kh = kn[:, sl] + krp
        kh = kh * lax.rsqrt(jnp.sum(kh * kh, axis=-1, keepdims=True) / QK_DIM + RMS_EPS) * gk
        k_ref[:, sl] = _rope(kh, cos, sa, sb).astype(k_ref.dtype)


def _mla_pre(y, sh, sc, g1, w, rope_tabs, tm, tiles_per_seq, rope_tiled, qk_dtype):
    t = y.shape[0]
    cos, sa, sb = rope_tabs
    if rope_tiled:
        rspec = pl.BlockSpec((tm, HEAD_PAD), lambda i: (i % tiles_per_seq, 0))
    else:
        rspec = _full((1, HEAD_PAD))
    wide = N_HEADS * HEAD_PAD
    tok = lambda n: pl.BlockSpec((tm, n), lambda i: (i, 0))
    return pl.pallas_call(
        _mla_pre_kernel,
        grid=(t // tm,),
        in_specs=[
            tok(D_MODEL), _mod_spec(sh, tiles_per_seq), _mod_spec(sc, tiles_per_seq),
            _full((1, D_MODEL)), _full(w["w_in"].shape), _full((1, Q_LORA_RANK)),
            _full((1, KV_LORA_RANK)), _full(w["w_uq"].shape), _full((1, HEAD_PAD)),
            _full(w["w_uk"].shape), _full((1, HEAD_PAD)), rspec, rspec, rspec,
        ],
        out_specs=[tok(wide), tok(wide), tok(KV_LORA_RANK), tok(HEAD_PAD)],
        out_shape=[
            jax.ShapeDtypeStruct((t, wide), qk_dtype),
            jax.ShapeDtypeStruct((t, wide), qk_dtype),
            jax.ShapeDtypeStruct((t, KV_LORA_RANK), F32),
            jax.ShapeDtypeStruct((t, HEAD_PAD), F32),
        ],
        compiler_params=_cparams("parallel"),
        name="mla_pre",
    )(y, sh, sc, g1, w["w_in"], w["g_q_a"], w["g_kv_a"], w["w_uq"], w["g_q"], w["w_uk"], w["g_k"],
      cos, sa, sb)


ATT_TQ = 256
ATT_TK = 256


def _attn_prompt_kernel(q_ref, k_ref, ckv_ref, o_ref):
    i = pl.program_id(1)
    row = i * ATT_TQ + lax.broadcasted_iota(jnp.int32, (ATT_TQ, ATT_TK), 0)
    col0 = lax.broadcasted_iota(jnp.int32, (ATT_TQ, ATT_TK), 1)
    for hd in range(N_HEADS):
        sl = slice(hd * HEAD_PAD, (hd + 1) * HEAD_PAD)
        qh = q_ref[:, sl]

        def body(j, carry, sl=sl, qh=qh):
            m, l, acc = carry
            off = pl.multiple_of(j * ATT_TK, ATT_TK)
            kb = k_ref[pl.ds(off, ATT_TK), sl]
            s = _dot_nt(qh, kb) * SM_SCALE
            s = jnp.where(col0 + off <= row, s, NEG_BIG)
            m_new = jnp.maximum(m, jnp.max(s, axis=-1, keepdims=True))
            p = jnp.exp(s - m_new)
            alpha = jnp.exp(m - m_new)
            l = alpha * l + jnp.sum(p, axis=-1, keepdims=True)
            cb = ckv_ref[pl.ds(off, ATT_TK), :].astype(BF16)
            acc = alpha * acc + _dot(p.astype(BF16), cb)
            return m_new, l, acc

        init = (jnp.full((ATT_TQ, 1), NEG_BIG, F32), jnp.zeros((ATT_TQ, 1), F32),
                jnp.zeros((ATT_TQ, KV_LORA_RANK), F32))
        m, l, acc = lax.fori_loop(0, i + 1, body, init)
        o_ref[:, sl] = (acc / l).astype(o_ref.dtype)


def _attn_prompt(q, k, ckv):
    nq = SEQ // ATT_TQ
    wide = N_HEADS * HEAD_PAD
    return pl.pallas_call(
        _attn_prompt_kernel,
        grid=(BATCH, nq),
        in_specs=[
            pl.BlockSpec((ATT_TQ, wide), lambda b, i: (b * nq + i, 0)),
            pl.BlockSpec((SEQ, wide), lambda b, i: (b, 0)),
            pl.BlockSpec((SEQ, KV_LORA_RANK), lambda b, i: (b, 0)),
        ],
        out_specs=pl.BlockSpec((ATT_TQ, wide), lambda b, i: (b * nq + i, 0)),
        out_shape=jax.ShapeDtypeStruct((BATCH * SEQ, wide), BF16),
        compiler_params=_cparams("parallel", "parallel"),
        name="attn_prompt",
    )(q, k, ckv)


SA_PAGES = 8
SA_POS = SA_PAGES * PAGE_SIZE
SA_STEPS = N_PAGES // SA_PAGES


def _absorb_kernel(q_ref, gk_ref, wukt_ref, qa_ref, qr_ref):
    gk = gk_ref[...]
    for hd in range(N_HEADS):
        qh = q_ref[:, hd * HEAD_PAD:(hd + 1) * HEAD_PAD]
        qa_ref[hd] = _dot((qh * gk).astype(BF16), wukt_ref[hd])
        qr_ref[hd] = pltpu.roll(qh, HEAD_PAD - QK_NOPE_DIM, 1)


def _absorb(q_s, gk_nope, wukt):
    b = q_s.shape[0]
    shp = jax.ShapeDtypeStruct((N_HEADS, b, HEAD_PAD), F32)
    return pl.pallas_call(
        _absorb_kernel,
        in_specs=[_full(q_s.shape), _full(gk_nope.shape), _full(wukt.shape)],
        out_specs=[_full(shp.shape), _full(shp.shape)],
        out_shape=[shp, shp],
        grid=(1,),
        compiler_params=_cparams("arbitrary"),
        name="absorb_q",
    )(q_s, gk_nope, wukt)


def _attn_sample_kernel(pt_ref, *refs):
    kv_refs = refs[:SA_PAGES]
    kr_refs = refs[SA_PAGES:2 * SA_PAGES]
    (wukt_ref, qa_ref, qr_ref, qs_ref, knew_ref, cnew_ref, gkr_ref, eye_ref, cos_ref, sin_ref,
     o_ref, m_ref, l_ref, acc_ref, ssq_ref) = refs[2 * SA_PAGES:]
    c = pl.program_id(1)

    @pl.when(c == 0)
    def _():
        m_ref[...] = jnp.full_like(m_ref, NEG_BIG)
        l_ref[...] = jnp.zeros_like(l_ref)
        acc_ref[...] = jnp.zeros_like(acc_ref)

    cc = jnp.concatenate([r[...] for r in kv_refs], axis=0).astype(BF16)
    qa16 = jnp.concatenate([qa_ref[...], jnp.zeros((8, KV_LORA_RANK), F32)], axis=0).astype(BF16)
    lhs = jnp.concatenate([wukt_ref[...], qa16], axis=0)
    res = _dot_nt(lhs, cc)
    for hd in range(N_HEADS):
        x = res[hd * QK_NOPE_DIM:(hd + 1) * QK_NOPE_DIM]
        ssq_ref[hd:hd + 1, :] = jnp.sum(x * x, axis=0, keepdims=True)
    sn = res[N_HEADS * QK_NOPE_DIM:N_HEADS * QK_NOPE_DIM + N_HEADS]

    krs = jnp.concatenate([r[...] for r in kr_refs], axis=0)
    krt = _dot_nt(eye_ref[...], krs.astype(BF16))
    kr2 = jnp.sum(krt * krt, axis=0, keepdims=True)
    krg = krt * gkr_ref[...]
    x1, x2 = krg[:ROPE_HALF], krg[ROPE_HALF:]
    cos, sin = cos_ref[...], sin_ref[...]
    rk = jnp.concatenate([x1 * cos - x2 * sin, x2 * cos + x1 * sin], axis=0).astype(BF16)
    sr = _dot(qr_ref[...][:, :QK_ROPE_DIM].astype(BF16), rk)

    rinv = lax.rsqrt((ssq_ref[...] + kr2) / QK_DIM + RMS_EPS)
    s = (sn + sr) * rinv * SM_SCALE
    m_old = m_ref[...]
    m_new = jnp.maximum(m_old, jnp.max(s, axis=-1, keepdims=True))
    p = jnp.exp(s - m_new)
    alpha = jnp.exp(m_old - m_new)
    l_ref[...] = alpha * l_ref[...] + jnp.sum(p, axis=-1, keepdims=True)
    acc_ref[...] = alpha * acc_ref[...] + _dot(p.astype(BF16), cc)
    m_ref[...] = m_new

    @pl.when(c == SA_STEPS - 1)
    def _():
        s_new = jnp.sum(qs_ref[...] * knew_ref[...], axis=-1, keepdims=True) * SM_SCALE
        m_o = m_ref[...]
        m_n = jnp.maximum(m_o, s_new)
        p_new = jnp.exp(s_new - m_n)
        a = jnp.exp(m_o - m_n)
        l = a * l_ref[...] + p_new
        acc = a * acc_ref[...] + p_new * cnew_ref[...]
        o_ref[...] = acc / l


def _attn_sample(page_table, cache_kv, cache_kr, wukt_flat, qa, qr, qs, knew, cnew, gkr, cos_t, sin_t):
    b = qa.shape[0]

    def page_spec(g, width):
        return pl.BlockSpec((None, PAGE_SIZE, width),
                            lambda s, c, pt, g=g: (pt[s, c * SA_PAGES + g], 0, 0))

    per_seq = pl.BlockSpec((None, N_HEADS, HEAD_PAD), lambda s, c, pt: (s, 0, 0))
    eye = jnp.eye(QK_ROPE_DIM, dtype=BF16)
    in_specs = (
        [page_spec(g, KV_LORA_RANK) for g in range(SA_PAGES)]
        + [page_spec(g, QK_ROPE_DIM) for g in range(SA_PAGES)]
        + [
            pl.BlockSpec(wukt_flat.shape, lambda s, c, pt: (0, 0)),
            per_seq, per_seq, per_seq, per_seq,
            pl.BlockSpec((None, 1, KV_LORA_RANK), lambda s, c, pt: (s, 0, 0)),
            pl.BlockSpec(gkr.shape, lambda s, c, pt: (0, 0)),
            pl.BlockSpec(eye.shape, lambda s, c, pt: (0, 0)),
            pl.BlockSpec((ROPE_HALF, SA_POS), lambda s, c, pt: (0, c)),
            pl.BlockSpec((ROPE_HALF, SA_POS), lambda s, c, pt: (0, c)),
        ]
    )
    grid_spec = pltpu.PrefetchScalarGridSpec(
        num_scalar_prefetch=1,
        grid=(b, SA_STEPS),
        in_specs=in_specs,
        out_specs=pl.BlockSpec((None, N_HEADS, KV_LORA_RANK), lambda s, c, pt: (s, 0, 0)),
        scratch_shapes=[
            pltpu.VMEM((N_HEADS, 1), F32), pltpu.VMEM((N_HEADS, 1), F32),
            pltpu.VMEM((N_HEADS, KV_LORA_RANK), F32), pltpu.VMEM((N_HEADS, SA_POS), F32),
        ],
    )
    return pl.pallas_call(
        _attn_sample_kernel,
        grid_spec=grid_spec,
        out_shape=jax.ShapeDtypeStruct((b, N_HEADS, KV_LORA_RANK), F32),
        compiler_params=_cparams("parallel", "arbitrary"),
        name="attn_sample",
    )(page_table, *([cache_kv] * SA_PAGES), *([cache_kr] * SA_PAGES), wukt_flat, qa, qr, qs, knew,
      cnew, gkr, eye, cos_t, sin_t)


def _mla_post_kernel(ol_ref, wuv_ref, wo_ref, y_ref, gate_ref, o_ref):
    parts = [_dot(ol_ref[:, hd * KV_LORA_RANK:(hd + 1) * KV_LORA_RANK], wuv_ref[hd])
             for hd in range(N_HEADS)]
    o = jnp.concatenate(parts, axis=-1).astype(BF16)
    o_ref[...] = y_ref[...] + gate_ref[...] * _dot(o, wo_ref[...])


def _mla_post(o_lat, wuv, wo, y, gate, tm, tiles_per_seq):
    t = y.shape[0]
    tok = lambda n: pl.BlockSpec((tm, n), lambda i: (i, 0))
    return pl.pallas_call(
        _mla_post_kernel,
        grid=(t // tm,),
        in_specs=[tok(N_HEADS * KV_LORA_RANK), _full(wuv.shape), _full(wo.shape), tok(D_MODEL),
                  _mod_spec(gate, tiles_per_seq)],
        out_specs=tok(D_MODEL),
        out_shape=jax.ShapeDtypeStruct((t, D_MODEL), F32),
        compiler_params=_cparams("parallel"),
        name="mla_post",
    )(o_lat, wuv, wo, y, gate)


def _sgu_kernel(y_ref, sh_ref, sc_ref, gate_ref, g1_ref, win_ref, bin_ref, gv_ref, sp_a_ref, sp_b_ref,
                wout_ref, o_ref, v_ref, *, tm, spatial):
    y = y_ref[...]
    h = _rms(y) * g1_ref[...]
    h = h * (1.0 + sc_ref[...]) + sh_ref[...]
    z = _gelu(_dot(h.astype(BF16), win_ref[...]) + bin_ref[...])
    u = z[:, :SG_HALF]
    v = _rms(z[:, SG_HALF:]) * gv_ref[...]
    if spatial:
        r = lax.broadcasted_iota(jnp.int32, (CHUNK, CHUNK), 0)
        cidx = lax.broadcasted_iota(jnp.int32, (CHUNK, CHUNK), 1)
        vb = v.astype(BF16)
        cols = []
        for g in range(SG_GROUPS):
            wc = jnp.where(cidx <= r, sp_a_ref[g], 0.0).astype(BF16)
            rows = [_dot(wc, vb[c * CHUNK:(c + 1) * CHUNK, g * CHUNK:(g + 1) * CHUNK]) + sp_b_ref[g]
                    for c in range(tm // CHUNK)]
            cols.append(jnp.concatenate(rows, axis=0))
        s = jnp.concatenate(cols, axis=-1)
        v_ref[...] = v[tm - CHUNK:, :]
    else:
        s = v * sp_a_ref[...] + sp_b_ref[...]
        v_ref[...] = v
    o = _dot((u * s).astype(BF16), wout_ref[...])
    o_ref[...] = y + gate_ref[...] * o


def _sgu(y, sh, sc, gate, g1, w, sp_a, sp_b, tm, tiles_per_seq, spatial):
    t = y.shape[0]
    tok = lambda n: pl.BlockSpec((tm, n), lambda i: (i, 0))
    if spatial:
        n_seq = t // (tm * tiles_per_seq)
        v_spec = pl.BlockSpec((None, CHUNK, SG_HALF), lambda i: (i // tiles_per_seq, 0, 0))
        v_shape = jax.ShapeDtypeStruct((n_seq, CHUNK, SG_HALF), F32)
    else:
        v_spec = tok(SG_HALF)
        v_shape = jax.ShapeDtypeStruct((t, SG_HALF), F32)
    return pl.pallas_call(
        functools.partial(_sgu_kernel, tm=tm, spatial=spatial),
        grid=(t // tm,),
        in_specs=[tok(D_MODEL), _mod_spec(sh, tiles_per_seq), _mod_spec(sc, tiles_per_seq),
                  _mod_spec(gate, tiles_per_seq), _full((1, D_MODEL)), _full(w["w_in"].shape),
                  _full((1, 2 * SG_HALF)), _full((1, SG_HALF)), _full(sp_a.shape), _full(sp_b.shape),
                  _full(w["w_out"].shape)],
        out_specs=[tok(D_MODEL), v_spec],
        out_shape=[jax.ShapeDtypeStruct((t, D_MODEL), F32), v_shape],
        compiler_params=_cparams("arbitrary"),
        name="sgu",
    )(y, sh, sc, gate, g1, w["w_in"], w["b_in"], w["g_v"], sp_a, sp_b, w["w_out"])


N_TOP = PEER_TOPK + 1
TOP_ROWS = 24
N_CAND = TOP_ROWS + 7 * 8 + (TOP_ROWS - 8)


def _top_values(x, n, emit):
    prev = None
    for r in range(n):
        cand = x if prev is None else jnp.where(x < prev, x, NEG_BIG)
        prev = jnp.max(cand, axis=0, keepdims=True)
        emit(r, prev)


def _peer_pre_kernel(y_ref, sh_ref, sc_ref, g2_ref, wq_ref, sk_ref, ht_ref, s2_ref, eb_ref, th_ref,
                     cc_ref, s_scr, t_scr, cand_scr, tv_scr):
    h = _rms(y_ref[...]) * g2_ref[...]
    h = h * (1.0 + sc_ref[...]) + sh_ref[...]
    ht_ref[...] = h.T.astype(BF16)
    q = _dot(h.astype(BF16), wq_ref[...])
    for hd in range(PEER_HEADS):
        for side in range(2):
            o = (hd * 2 + side) * N_KEYS
            s_scr[side, hd] = _dot_nt(sk_ref[side], q[:, o:o + N_KEYS].astype(BF16))
    t_scr[...] = jnp.full_like(t_scr, NEG_BIG)

    def tops(idx, _):
        side, hd = idx // PEER_HEADS, idx % PEER_HEADS

        def emit(r, row):
            t_scr[side, hd, r:r + 1, :] = row

        _top_values(s_scr[side, hd], N_TOP, emit)
        return 0

    lax.fori_loop(0, 2 * PEER_HEADS, tops, 0)

    def finish(hd, _):
        t1, t2 = t_scr[0, hd], t_scr[1, hd]
        cand_scr[0:TOP_ROWS] = t1[0:1] + t2
        for a in range(1, 8):
            cand_scr[TOP_ROWS + (a - 1) * 8:TOP_ROWS + a * 8] = t1[a:a + 1] + t2[0:8]
        cand_scr[TOP_ROWS + 56:] = t1[8:] + t2[0:1]
        cand = cand_scr[...]

        def emit(r, row):
            tv_scr[r:r + 1, :] = row

        _top_values(cand, N_TOP, emit)
        tau = 0.5 * (tv_scr[PEER_TOPK - 1:PEER_TOPK, :] + tv_scr[PEER_TOPK:PEER_TOPK + 1, :])
        top = t1[:1] + t2[:1]
        z = jnp.sum(jnp.where(cand >= tau, jnp.exp(cand - top), 0.0), axis=0, keepdims=True)
        s1, s2 = s_scr[0, hd], s_scr[1, hd]
        s2_ref[hd] = s2
        eb_ref[hd] = jnp.exp(s2 - t2[:1])
        th_ref[hd] = tau - s1
        cc_ref[hd] = jnp.exp(s1 - t1[:1]) / z
        return 0

    lax.fori_loop(0, PEER_HEADS, finish, 0)


def _peer_pre(y, sh, sc, g2, wq, sk, tm, tiles_per_seq):
    t = y.shape[0]
    hk = pl.BlockSpec((PEER_HEADS, N_KEYS, tm), lambda i: (0, 0, i))
    hk_shape = jax.ShapeDtypeStruct((PEER_HEADS, N_KEYS, t), F32)
    return pl.pallas_call(
        _peer_pre_kernel,
        grid=(t // tm,),
        in_specs=[pl.BlockSpec((tm, D_MODEL), lambda i: (i, 0)), _mod_spec(sh, tiles_per_seq),
                  _mod_spec(sc, tiles_per_seq), _full((1, D_MODEL)), _full(wq.shape), _full(sk.shape)],
        out_specs=[pl.BlockSpec((D_MODEL, tm), lambda i: (0, i)), hk, hk, hk, hk],
        out_shape=[jax.ShapeDtypeStruct((D_MODEL, t), BF16), hk_shape, hk_shape, hk_shape, hk_shape],
        scratch_shapes=[pltpu.VMEM((2, PEER_HEADS, N_KEYS, tm), F32),
                        pltpu.VMEM((2, PEER_HEADS, TOP_ROWS, tm), F32),
                        pltpu.VMEM((N_CAND, tm), F32),
                        pltpu.VMEM((TOP_ROWS, tm), F32)],
        compiler_params=_cparams("parallel"),
        name="peer_pre",
    )(y, sh, sc, g2, wq, sk)


PEER_EB = 1024
PEER_ROWS = PEER_EB // N_KEYS
PEER_STEPS = N_EXPERTS // PEER_EB


def _peer_dense_kernel(ht_ref, u_ref, vt_ref, s2_ref, eb_ref, th_ref, cc_ref, y_ref, gate_ref, o_ref,
                       acc_ref, p_ref, *, tm):
    e = pl.program_id(1)

    @pl.when(e == 0)
    def _():
        acc_ref[...] = jnp.zeros_like(acc_ref)

    p_ref[...] = _gelu(_dot(u_ref[...], ht_ref[...]))

    def column(col, _):
        cs = pl.ds(pl.multiple_of(col * 128, 128), 128)
        for ii in range(PEER_ROWS):
            rs = slice(ii * N_KEYS, (ii + 1) * N_KEYS)
            w = jnp.zeros((N_KEYS, 128), F32)
            for hd in range(PEER_HEADS):
                th = th_ref[hd, ii:ii + 1, cs]
                cc = cc_ref[hd, ii:ii + 1, cs]
                w = w + jnp.where(s2_ref[hd, :, cs] >= th, eb_ref[hd, :, cs], 0.0) * cc
            p_ref[rs, cs] = p_ref[rs, cs] * w
        return 0

    lax.fori_loop(0, tm // 128, column, 0)
    acc_ref[...] += _dot(vt_ref[...], p_ref[...].astype(BF16))

    @pl.when(e == PEER_STEPS - 1)
    def _():
        o_ref[...] = y_ref[...] + gate_ref[...] * acc_ref[...].T


def _peer_dense(ht, u_b, vt_b, s2, eb, th, cc, y, gate, tm, tiles_per_seq):
    t = y.shape[0]
    hk = pl.BlockSpec((PEER_HEADS, N_KEYS, tm), lambda i, e: (0, 0, i))
    hr = pl.BlockSpec((PEER_HEADS, PEER_ROWS, tm), lambda i, e: (0, e, i))
    return pl.pallas_call(
        functools.partial(_peer_dense_kernel, tm=tm),
        grid=(t // tm, PEER_STEPS),
        in_specs=[pl.BlockSpec((D_MODEL, tm), lambda i, e: (0, i)),
                  pl.BlockSpec((PEER_EB, D_MODEL), lambda i, e: (e, 0)),
                  pl.BlockSpec((D_MODEL, PEER_EB), lambda i, e: (0, e)),
                  hk, hk, hr, hr,
                  pl.BlockSpec((tm, D_MODEL), lambda i, e: (i, 0)),
                  pl.BlockSpec((None,) + gate.shape[1:], lambda i, e: (i // tiles_per_seq, 0, 0))],
        out_specs=pl.BlockSpec((tm, D_MODEL), lambda i, e: (i, 0)),
        out_shape=jax.ShapeDtypeStruct((t, D_MODEL), F32),
        scratch_shapes=[pltpu.VMEM((D_MODEL, tm), F32), pltpu.VMEM((PEER_EB, tm), F32)],
        compiler_params=_cparams("parallel", "arbitrary"),
        name="peer_dense",
    )(ht, u_b, vt_b, s2, eb, th, cc, y, gate)


def _peer(y, sh, sc, gate, g2, wq, sk, u_b, vt_b, tm, tiles_per_seq):
    ht, s2, eb, th, cc = _peer_pre(y, sh, sc, g2, wq, sk, tm, tiles_per_seq)
    return _peer_dense(ht, u_b, vt_b, s2, eb, th, cc, y, gate, tm, tiles_per_seq)


def _pad_heads(w, used):
    w = jnp.pad(w, [(0, 0)] * (w.ndim - 1) + [(0, HEAD_PAD - used)])
    return w.reshape(w.shape[:-2] + (N_HEADS * HEAD_PAD,))


def _pad_gain(g):
    return jnp.pad(g, (0, HEAD_PAD - QK_DIM)).reshape(1, HEAD_PAD)


def _rope_tables(pos):
    inv_freq = ROPE_THETA ** (-jnp.arange(ROPE_HALF, dtype=F32) * 2.0 / QK_ROPE_DIM)
    ang = pos.astype(F32)[:, None] * inv_freq[None, :]
    cos, sin = jnp.cos(ang), jnp.sin(ang)
    t = pos.shape[0]
    one = jnp.ones((t, QK_NOPE_DIM), F32)
    z64 = jnp.zeros((t, QK_NOPE_DIM), F32)
    z16 = jnp.zeros((t, ROPE_HALF), F32)
    z32 = jnp.zeros((t, HEAD_PAD - QK_DIM), F32)
    cos_t = jnp.concatenate([one, cos, cos, z32], axis=1)
    sa = jnp.concatenate([z64, -sin, z16, z32], axis=1)
    sb = jnp.concatenate([z64, z16, sin, z32], axis=1)
    return cos_t, sa, sb, cos, sin


def _mla_weights(w_in, g_q_a, g_kv_a, w_uq, g_q, w_uk):
    lo = Q_LORA_RANK + KV_LORA_RANK
    w_in_p = jnp.concatenate(
        [w_in[:, :lo], jnp.zeros((D_MODEL, QK_NOPE_DIM), F32), w_in[:, lo:],
         jnp.zeros((D_MODEL, HEAD_PAD - QK_DIM), F32)], axis=1).astype(BF16)
    return {
        "w_in": w_in_p,
        "g_q_a": g_q_a.reshape(1, -1),
        "g_kv_a": g_kv_a.reshape(1, -1),
        "w_uq": _pad_heads(w_uq.reshape(Q_LORA_RANK, N_HEADS, QK_DIM), QK_DIM).astype(BF16),
        "g_q": _pad_gain(g_q),
        "w_uk": _pad_heads(w_uk, QK_NOPE_DIM).astype(BF16),
    }


def _split_mod(m, n_prompt):
    m = m.reshape(m.shape[0], 6, D_MODEL)
    mp = [m[:n_prompt, j].reshape(n_prompt, 1, D_MODEL) for j in range(6)]
    ms = [m[n_prompt:, j].reshape(1, -1, D_MODEL) for j in range(6)]
    return mp, ms


def kernel(x_prompt, x_sample, c_prompt, c_sample, cache_kv_latent, cache_k_rope, page_table, norm1_g, norm2_g, w_mod, b_mod, mla_w_in, mla_g_q_a, mla_g_kv_a, mla_w_uq, mla_g_q, mla_w_uk, mla_w_uv, mla_g_k, mla_w_o, sg_w_in, sg_b_in, sg_g_v, sg_w_s, sg_b_s, sg_w_out, peer_w_q, peer_sub_keys, peer_u, peer_v):
    yp = x_prompt.reshape(BATCH * SEQ, D_MODEL)
    ys = x_sample.reshape(DEC_BATCH, D_MODEL)
    mods = _modulation(jnp.concatenate([c_prompt, c_sample], axis=0), w_mod, b_mod)

    tm_p, tps_p = 512, SEQ // 512
    tm_s, tps_s = DEC_BATCH, 1

    outs = {}
    for layer in range(DEPTH):
        mp, ms = _split_mod(mods[layer], BATCH)
        g1 = norm1_g[layer].reshape(1, D_MODEL)
        g2 = norm2_g[layer].reshape(1, D_MODEL)
        if layer % 2 == 0:
            a = layer // 2
            w = _mla_weights(mla_w_in[a], mla_g_q_a[a], mla_g_kv_a[a], mla_w_uq[a], mla_g_q[a], mla_w_uk[a])
            gk = _pad_gain(mla_g_k[a])
            w["g_k"] = gk
            wuv = jnp.transpose(mla_w_uv[a], (1, 0, 2)).astype(BF16)
            wo = mla_w_o[a].astype(BF16)
            cos_p, sa_p, sb_p, _, _ = _rope_tables(jnp.arange(SEQ))
            cos_s, sa_s, sb_s, _, _ = _rope_tables(jnp.full((1,), PAST_LEN))
            _, _, _, cos_c, sin_c = _rope_tables(jnp.arange(PAST_LEN))

            q, k, ckv_p, krp_p = _mla_pre(yp, mp[0], mp[1], g1, w, (cos_p, sa_p, sb_p), tm_p, tps_p, True, BF16)
            o_lat = _attn_prompt(q, k, ckv_p)
            yp = _mla_post(o_lat, wuv, wo, yp, mp[2], tm_p, tps_p)

            q_s, k_s, ckv_s, krp_s = _mla_pre(ys, ms[0], ms[1], g1, w, (cos_s, sa_s, sb_s), tm_s, tps_s, False, F32)
            wukt = jnp.pad(jnp.transpose(mla_w_uk[a], (1, 2, 0)),
                           ((0, 0), (0, HEAD_PAD - QK_NOPE_DIM), (0, 0))).astype(BF16)
            gk_nope = gk * (jnp.arange(HEAD_PAD) < QK_NOPE_DIM)[None, :]
            qa, qr = (jnp.transpose(x, (1, 0, 2)) for x in _absorb(q_s, gk_nope, wukt))
            wukt_flat = jnp.transpose(mla_w_uk[a], (1, 2, 0)).reshape(N_HEADS * QK_NOPE_DIM, KV_LORA_RANK).astype(BF16)
            gkr = mla_g_k[a][QK_NOPE_DIM:].reshape(QK_ROPE_DIM, 1)
            o_lat_s = _attn_sample(
                page_table, cache_kv_latent[a], cache_k_rope[a], wukt_flat, qa, qr,
                q_s.reshape(DEC_BATCH, N_HEADS, HEAD_PAD), k_s.reshape(DEC_BATCH, N_HEADS, HEAD_PAD),
                ckv_s.reshape(DEC_BATCH, 1, KV_LORA_RANK), gkr, cos_c.T, sin_c.T)
            ys = _mla_post(o_lat_s.reshape(DEC_BATCH, N_HEADS * KV_LORA_RANK).astype(BF16), wuv, wo, ys,
                           ms[2], tm_s, tps_s)

            outs.setdefault("kv_p", []).append(ckv_p.reshape(BATCH, SEQ, KV_LORA_RANK))
            outs.setdefault("kr_p", []).append(krp_p[:, QK_NOPE_DIM:QK_DIM].reshape(BATCH, SEQ, QK_ROPE_DIM))
            outs.setdefault("kv_s", []).append(ckv_s.reshape(DEC_BATCH, 1, KV_LORA_RANK))
            outs.setdefault("kr_s", []).append(krp_s[:, QK_NOPE_DIM:QK_DIM].reshape(DEC_BATCH, 1, QK_ROPE_DIM))
        else:
            bidx = layer // 2
            w = {"w_in": sg_w_in[bidx].astype(BF16), "b_in": sg_b_in[bidx].reshape(1, -1),
                 "g_v": sg_g_v[bidx].reshape(1, -1), "w_out": sg_w_out[bidx].astype(BF16)}
            sp_a = sg_w_s[bidx]
            sp_b = sg_b_s[bidx].reshape(SG_GROUPS, CHUNK, 1)
            yp, vp = _sgu(yp, mp[0], mp[1], mp[2], g1, w, sp_a, sp_b, 256, SEQ // 256, True)
            coef = jnp.repeat(sg_w_s[bidx][:, 0, 0], CHUNK).reshape(1, SG_HALF)
            bias = jnp.repeat(sg_b_s[bidx][:, 0], CHUNK).reshape(1, SG_HALF)
            ys, vs = _sgu(ys, ms[0], ms[1], ms[2], g1, w, coef, bias, tm_s, tps_s, False)
            outs.setdefault("v_p", []).append(vp)
            outs.setdefault("v_s", []).append(vs.reshape(DEC_BATCH, 1, SG_HALF))

        wq = peer_w_q[layer].astype(BF16)
        sk = peer_sub_keys[layer].astype(BF16)
        u_b = peer_u[layer].astype(BF16)
        vt_b = peer_v[layer].T.astype(BF16)
        yp = _peer(yp, mp[3], mp[4], mp[5], g2, wq, sk, u_b, vt_b, tm_p, tps_p)
        ys = _peer(ys, ms[3], ms[4], ms[5], g2, wq, sk, u_b, vt_b, tm_s, tps_s)

    return (yp.reshape(BATCH, SEQ, D_MODEL), ys.reshape(DEC_BATCH, 1, D_MODEL),
            jnp.stack(outs["kv_p"]), jnp.stack(outs["kr_p"]), jnp.stack(outs["kv_s"]), jnp.stack(outs["kr_s"]),
            jnp.stack(outs["v_p"]), jnp.stack(outs["v_s"]))
```

```python
import functools

import jax
import jax.numpy as jnp
from jax import lax
from jax.experimental import pallas as pl
from jax.experimental.pallas import tpu as pltpu

F32 = jnp.float32
BF16 = jnp.bfloat16

D_MODEL = 1024
BATCH = 8
SEQ = 2048
DEPTH = 2
DEC_BATCH = 128
PAST_LEN = 16384
PAGE_SIZE = 128
N_PAGES = PAST_LEN // PAGE_SIZE

N_HEADS = 8
Q_LORA_RANK = 256
KV_LORA_RANK = 128
QK_NOPE_DIM = 64
QK_ROPE_DIM = 32
QK_DIM = QK_NOPE_DIM + QK_ROPE_DIM
V_HEAD_DIM = 128
ROPE_THETA = 10000.0
SM_SCALE = QK_DIM ** -0.5
HEAD_PAD = 128
ROPE_HALF = QK_ROPE_DIM // 2

CHUNK = 128
SG_HALF = 1536
SG_GROUPS = 12

PEER_HEADS = 8
N_KEYS = 128
N_EXPERTS = N_KEYS * N_KEYS
PEER_TOPK = 16
RMS_EPS = 1e-6

NEG_BIG = -1e30
VMEM_LIMIT = 56 * 1024 * 1024


def _cparams(*sem):
    return pltpu.CompilerParams(dimension_semantics=sem, vmem_limit_bytes=VMEM_LIMIT)


def _rms(x):
    return x * lax.rsqrt(jnp.mean(x * x, axis=-1, keepdims=True) + RMS_EPS)


def _gelu(x):
    cdf = 0.5 * (1.0 + jnp.tanh(0.7978845608028654 * (x + 0.044715 * (x * x * x))))
    return x * cdf


def _dot(a, b):
    return jnp.dot(a, b, preferred_element_type=F32)


def _dot_nt(a, b):
    return lax.dot_general(a, b, (((1,), (1,)), ((), ())), preferred_element_type=F32)


def _full(shape):
    n = len(shape)
    return pl.BlockSpec(shape, lambda *_: (0,) * n)


def _mod_spec(arr, tiles_per_seq):
    return pl.BlockSpec((None,) + arr.shape[1:], lambda t, *_: (t // tiles_per_seq, 0, 0))


def _mod_kernel(c_ref, w_ref, b_ref, o_ref):
    c = c_ref[...]
    a = (c * jax.nn.sigmoid(c)).astype(BF16)
    o_ref[...] = _dot(a, w_ref[...].astype(BF16)) + b_ref[...]


def _modulation(c_all, w_mod, b_mod):
    r = c_all.shape[0]
    tn = 1536
    return pl.pallas_call(
        _mod_kernel,
        grid=(DEPTH, 6 * D_MODEL // tn),
        in_specs=[
            pl.BlockSpec((r, D_MODEL), lambda l, n: (0, 0)),
            pl.BlockSpec((None, D_MODEL, tn), lambda l, n: (l, 0, n)),
            pl.BlockSpec((None, 1, tn), lambda l, n: (l, 0, n)),
        ],
        out_specs=pl.BlockSpec((None, r, tn), lambda l, n: (l, 0, n)),
        out_shape=jax.ShapeDtypeStruct((DEPTH, r, 6 * D_MODEL), F32),
        compiler_params=_cparams("parallel", "parallel"),
        name="modulation",
    )(c_all, w_mod, b_mod.reshape(DEPTH, 1, 6 * D_MODEL))


def _rope(x, cos, sa, sb):
    return x * cos + pltpu.roll(x, HEAD_PAD - ROPE_HALF, 1) * sa + pltpu.roll(x, ROPE_HALF, 1) * sb


def _mla_pre_kernel(y_ref, sh_ref, sc_ref, g1_ref, win_ref, gqa_ref, gkva_ref, wuq_ref, gq_ref,
                    wuk_ref, gk_ref, cos_ref, sa_ref, sb_ref, q_ref, k_ref, ckv_ref, kr_ref):
    h = _rms(y_ref[...]) * g1_ref[...]
    h = h * (1.0 + sc_ref[...]) + sh_ref[...]
    z = _dot(h.astype(BF16), win_ref[...])
    cq = _rms(z[:, :Q_LORA_RANK]) * gqa_ref[...]
    ckv = _rms(z[:, Q_LORA_RANK:Q_LORA_RANK + KV_LORA_RANK]) * gkva_ref[...]
    krp = z[:, Q_LORA_RANK + KV_LORA_RANK:]
    ckv_ref[...] = ckv
    kr_ref[...] = krp
    q = _dot(cq.astype(BF16), wuq_ref[...])
    kn = _dot(ckv.astype(BF16), wuk_ref[...])
    cos, sa, sb = cos_ref[...], sa_ref[...], sb_ref[...]
    gq, gk = gq_ref[...], gk_ref[...]
    for hd in range(N_HEADS):
        sl = slice(hd * HEAD_PAD, (hd + 1) * HEAD_PAD)
        qh = q[:, sl]
        qh = qh * lax.rsqrt(jnp.sum(qh * qh, axis=-1, keepdims=True) / QK_DIM + RMS_EPS) * gq
        q_ref[:, sl] = _rope(qh, cos, sa, sb).astype(q_ref.dtype)
        kh = kn[:, sl] + krp
        kh = kh * lax.rsqrt(jnp.sum(kh * kh, axis=-1, keepdims=True) / QK_DIM + RMS_EPS) * gk
        k_ref[:, sl] = _rope(kh, cos, sa, sb).astype(k_ref.dtype)


def _mla_pre(y, sh, sc, g1, w, rope_tabs, tm, tiles_per_seq, rope_tiled, qk_dtype):
    t = y.shape[0]
    cos, sa, sb = rope_tabs
    if rope_tiled:
        rspec = pl.BlockSpec((tm, HEAD_PAD), lambda i: (i % tiles_per_seq, 0))
    else:
        rspec = _full((1, HEAD_PAD))
    wide = N_HEADS * HEAD_PAD
    tok = lambda n: pl.BlockSpec((tm, n), lambda i: (i, 0))
    return pl.pallas_call(
        _mla_pre_kernel,
        grid=(t // tm,),
        in_specs=[
            tok(D_MODEL), _mod_spec(sh, tiles_per_seq), _mod_spec(sc, tiles_per_seq),
            _full((1, D_MODEL)), _full(w["w_in"].shape), _full((1, Q_LORA_RANK)),
            _full((1, KV_LORA_RANK)), _full(w["w_uq"].shape), _full((1, HEAD_PAD)),
            _full(w["w_uk"].shape), _full((1, HEAD_PAD)), rspec, rspec, rspec,
        ],
        out_specs=[tok(wide), tok(wide), tok(KV_LORA_RANK), tok(HEAD_PAD)],
        out_shape=[
            jax.ShapeDtypeStruct((t, wide), qk_dtype),
            jax.ShapeDtypeStruct((t, wide), qk_dtype),
            jax.ShapeDtypeStruct((t, KV_LORA_RANK), F32),
            jax.ShapeDtypeStruct((t, HEAD_PAD), F32),
        ],
        compiler_params=_cparams("parallel"),
        name="mla_pre",
    )(y, sh, sc, g1, w["w_in"], w["g_q_a"], w["g_kv_a"], w["w_uq"], w["g_q"], w["w_uk"], w["g_k"],
      cos, sa, sb)


ATT_TQ = 256
ATT_TK = 256


def _attn_prompt_kernel(q_ref, k_ref, ckv_ref, o_ref, m_ref, acc_ref):
    i = pl.program_id(1)
    m_ref[...] = jnp.full_like(m_ref, NEG_BIG)
    acc_ref[...] = jnp.zeros_like(acc_ref)
    ones = jnp.ones((ATT_TK, KV_LORA_RANK), BF16)

    def block(j, diagonal):
        off = pl.multiple_of(j * ATT_TK, ATT_TK)
        cb = jnp.concatenate([ckv_ref[pl.ds(off, ATT_TK), :].astype(BF16), ones], axis=1)
        for hd in range(N_HEADS):
            sl = slice(hd * HEAD_PAD, (hd + 1) * HEAD_PAD)
            s = _dot_nt(q_ref[:, sl], k_ref[pl.ds(off, ATT_TK), sl]) * SM_SCALE
            if diagonal:
                row = lax.broadcasted_iota(jnp.int32, (ATT_TQ, ATT_TK), 0)
                col = lax.broadcasted_iota(jnp.int32, (ATT_TQ, ATT_TK), 1)
                s = jnp.where(col <= row, s, NEG_BIG)
            m_old = m_ref[hd]
            s_max = jnp.max(jnp.maximum(s[:, :128], s[:, 128:]), axis=-1, keepdims=True)
            m_new = jnp.maximum(m_old, jnp.broadcast_to(s_max, m_old.shape))
            p = jnp.concatenate([jnp.exp(s[:, :128] - m_new), jnp.exp(s[:, 128:] - m_new)], axis=1)
            alpha = jnp.exp(m_old - m_new)
            pv = _dot(p.astype(BF16), cb)
            acc_ref[hd] = jnp.concatenate([alpha, alpha], axis=1) * acc_ref[hd] + pv
            m_ref[hd] = m_new

    def body(j, carry):
        block(j, False)
        return carry

    lax.fori_loop(0, i, body, 0)
    block(i, True)
    for hd in range(N_HEADS):
        acc = acc_ref[hd]
        o_ref[:, hd * HEAD_PAD:(hd + 1) * HEAD_PAD] = (
            acc[:, :KV_LORA_RANK] / acc[:, KV_LORA_RANK:]).astype(o_ref.dtype)


def _attn_prompt(q, k, ckv):
    nq = SEQ // ATT_TQ
    wide = N_HEADS * HEAD_PAD
    return pl.pallas_call(
        _attn_prompt_kernel,
        grid=(BATCH, nq),
        in_specs=[
            pl.BlockSpec((ATT_TQ, wide), lambda b, i: (b * nq + i, 0)),
            pl.BlockSpec((SEQ, wide), lambda b, i: (b, 0)),
            pl.BlockSpec((SEQ, KV_LORA_RANK), lambda b, i: (b, 0)),
        ],
        out_specs=pl.BlockSpec((ATT_TQ, wide), lambda b, i: (b * nq + i, 0)),
        out_shape=jax.ShapeDtypeStruct((BATCH * SEQ, wide), BF16),
        scratch_shapes=[pltpu.VMEM((N_HEADS, ATT_TQ, 128), F32),
                        pltpu.VMEM((N_HEADS, ATT_TQ, 2 * KV_LORA_RANK), F32)],
        compiler_params=_cparams("parallel", "parallel"),
        name="attn_prompt",
    )(q, k, ckv)


SA_PAGES = 32
SA_POS = SA_PAGES * PAGE_SIZE
SA_STEPS = N_PAGES // SA_PAGES


def _absorb_kernel(q_ref, gk_ref, wukt_ref, qa_ref, qr_ref):
    gk = gk_ref[...]
    for hd in range(N_HEADS):
        qh = q_ref[:, hd * HEAD_PAD:(hd + 1) * HEAD_PAD]
        qa_ref[hd] = _dot((qh * gk).astype(BF16), wukt_ref[hd])
        qr_ref[hd] = pltpu.roll(qh, HEAD_PAD - QK_NOPE_DIM, 1)


def _absorb(q_s, gk_nope, wukt):
    b = q_s.shape[0]
    shp = jax.ShapeDtypeStruct((N_HEADS, b, HEAD_PAD), F32)
    return pl.pallas_call(
        _absorb_kernel,
        in_specs=[_full(q_s.shape), _full(gk_nope.shape), _full(wukt.shape)],
        out_specs=[_full(shp.shape), _full(shp.shape)],
        out_shape=[shp, shp],
        grid=(1,),
        compiler_params=_cparams("arbitrary"),
        name="absorb_q",
    )(q_s, gk_nope, wukt)


def _attn_sample_kernel(pt_ref, *refs):
    kv_refs = refs[:SA_PAGES]
    kr_refs = refs[SA_PAGES:2 * SA_PAGES]
    (wukt_ref, qa_ref, qr_ref, qs_ref, knew_ref, cnew_ref, gkr_ref, cos_ref, sin_ref,
     o_ref, m_ref, l_ref, acc_ref, ssq_ref) = refs[2 * SA_PAGES:]
    c = pl.program_id(1)

    @pl.when(c == 0)
    def _():
        m_ref[...] = jnp.full_like(m_ref, NEG_BIG)
        l_ref[...] = jnp.zeros_like(l_ref)
        acc_ref[...] = jnp.zeros_like(acc_ref)

    cc = jnp.concatenate([r[...] for r in kv_refs], axis=0).astype(BF16)
    qa16 = jnp.concatenate([qa_ref[...], jnp.zeros((8, KV_LORA_RANK), F32)], axis=0).astype(BF16)
    lhs = jnp.concatenate([wukt_ref[...], qa16], axis=0)
    res = _dot_nt(lhs, cc)
    for hd in range(N_HEADS):
        x = res[hd * QK_NOPE_DIM:(hd + 1) * QK_NOPE_DIM]
        ssq_ref[hd:hd + 1, :] = jnp.sum(x * x, axis=0, keepdims=True)
    sn = res[N_HEADS * QK_NOPE_DIM:N_HEADS * QK_NOPE_DIM + N_HEADS]

    krt = jnp.concatenate([r[...] for r in kr_refs], axis=1)
    kr2 = jnp.sum(krt * krt, axis=0, keepdims=True)
    krg = krt * gkr_ref[...]
    x1, x2 = krg[:ROPE_HALF], krg[ROPE_HALF:]
    cos, sin = cos_ref[...], sin_ref[...]
    rk = jnp.concatenate([x1 * cos - x2 * sin, x2 * cos + x1 * sin], axis=0).astype(BF16)
    sr = _dot(qr_ref[...][:, :QK_ROPE_DIM].astype(BF16), rk)

    rinv = lax.rsqrt((ssq_ref[...] + kr2) / QK_DIM + RMS_EPS)
    s = (sn + sr) * rinv * SM_SCALE
    m_old = m_ref[...]
    m_new = jnp.maximum(m_old, jnp.max(s, axis=-1, keepdims=True))
    p = jnp.exp(s - m_new)
    alpha = jnp.exp(m_old - m_new)
    l_ref[...] = alpha * l_ref[...] + jnp.sum(p, axis=-1, keepdims=True)
    acc_ref[...] = alpha * acc_ref[...] + _dot(p.astype(BF16), cc)
    m_ref[...] = m_new

    @pl.when(c == SA_STEPS - 1)
    def _():
        s_new = jnp.sum(qs_ref[...] * knew_ref[...], axis=-1, keepdims=True) * SM_SCALE
        m_o = m_ref[...]
        m_n = jnp.maximum(m_o, s_new)
        p_new = jnp.exp(s_new - m_n)
        a = jnp.exp(m_o - m_n)
        l = a * l_ref[...] + p_new
        acc = a * acc_ref[...] + p_new * cnew_ref[...]
        o_ref[...] = acc / l


def _attn_sample(page_table, cache_kv, cache_kr, wukt_flat, qa, qr, qs, knew, cnew, gkr, cos_t, sin_t):
    b = qa.shape[0]

    def page_spec(g, rows, width):
        return pl.BlockSpec((None, rows, width),
                            lambda s, c, pt, g=g: (pt[s, c * SA_PAGES + g], 0, 0))

    per_seq = pl.BlockSpec((None, N_HEADS, HEAD_PAD), lambda s, c, pt: (s, 0, 0))
    in_specs = (
        [page_spec(g, PAGE_SIZE, KV_LORA_RANK) for g in range(SA_PAGES)]
        + [page_spec(g, QK_ROPE_DIM, PAGE_SIZE) for g in range(SA_PAGES)]
        + [
            pl.BlockSpec(wukt_flat.shape, lambda s, c, pt: (0, 0)),
            per_seq, per_seq, per_seq, per_seq,
            pl.BlockSpec((None, 1, KV_LORA_RANK), lambda s, c, pt: (s, 0, 0)),
            pl.BlockSpec(gkr.shape, lambda s, c, pt: (0, 0)),
            pl.BlockSpec((ROPE_HALF, SA_POS), lambda s, c, pt: (0, c)),
            pl.BlockSpec((ROPE_HALF, SA_POS), lambda s, c, pt: (0, c)),
        ]
    )
    grid_spec = pltpu.PrefetchScalarGridSpec(
        num_scalar_prefetch=1,
        grid=(b, SA_STEPS),
        in_specs=in_specs,
        out_specs=pl.BlockSpec((None, N_HEADS, KV_LORA_RANK), lambda s, c, pt: (s, 0, 0)),
        scratch_shapes=[
            pltpu.VMEM((N_HEADS, 1), F32), pltpu.VMEM((N_HEADS, 1), F32),
            pltpu.VMEM((N_HEADS, KV_LORA_RANK), F32), pltpu.VMEM((N_HEADS, SA_POS), F32),
        ],
    )
    return pl.pallas_call(
        _attn_sample_kernel,
        grid_spec=grid_spec,
        out_shape=jax.ShapeDtypeStruct((b, N_HEADS, KV_LORA_RANK), F32),
        compiler_params=_cparams("parallel", "arbitrary"),
        name="attn_sample",
    )(page_table, *([cache_kv] * SA_PAGES), *([cache_kr] * SA_PAGES), wukt_flat, qa, qr, qs, knew,
      cnew, gkr, cos_t, sin_t)


def _mla_post_kernel(ol_ref, wuv_ref, wo_ref, y_ref, gate_ref, o_ref):
    parts = [_dot(ol_ref[:, hd * KV_LORA_RANK:(hd + 1) * KV_LORA_RANK], wuv_ref[hd])
             for hd in range(N_HEADS)]
    o = jnp.concatenate(parts, axis=-1).astype(BF16)
    o_ref[...] = y_ref[...] + gate_ref[...] * _dot(o, wo_ref[...])


def _mla_post(o_lat, wuv, wo, y, gate, tm, tiles_per_seq):
    t = y.shape[0]
    tok = lambda n: pl.BlockSpec((tm, n), lambda i: (i, 0))
    return pl.pallas_call(
        _mla_post_kernel,
        grid=(t // tm,),
        in_specs=[tok(N_HEADS * KV_LORA_RANK), _full(wuv.shape), _full(wo.shape), tok(D_MODEL),
                  _mod_spec(gate, tiles_per_seq)],
        out_specs=tok(D_MODEL),
        out_shape=jax.ShapeDtypeStruct((t, D_MODEL), F32),
        compiler_params=_cparams("parallel"),
        name="mla_post",
    )(o_lat, wuv, wo, y, gate)


def _sgu_kernel(y_ref, sh_ref, sc_ref, gate_ref, g1_ref, win_ref, bin_ref, gv_ref, sp_a_ref, sp_b_ref,
                wout_ref, o_ref, v_ref, *, tm, spatial):
    y = y_ref[...]
    h = _rms(y) * g1_ref[...]
    h = h * (1.0 + sc_ref[...]) + sh_ref[...]
    z = _gelu(_dot(h.astype(BF16), win_ref[...]) + bin_ref[...])
    u = z[:, :SG_HALF]
    v = _rms(z[:, SG_HALF:]) * gv_ref[...]
    if spatial:
        r = lax.broadcasted_iota(jnp.int32, (CHUNK, CHUNK), 0)
        cidx = lax.broadcasted_iota(jnp.int32, (CHUNK, CHUNK), 1)
        vb = v.astype(BF16)
        cols = []
        for g in range(SG_GROUPS):
            wc = jnp.where(cidx <= r, sp_a_ref[g], 0.0).astype(BF16)
            rows = [_dot(wc, vb[c * CHUNK:(c + 1) * CHUNK, g * CHUNK:(g + 1) * CHUNK]) + sp_b_ref[g]
                    for c in range(tm // CHUNK)]
            cols.append(jnp.concatenate(rows, axis=0))
        s = jnp.concatenate(cols, axis=-1)
        v_ref[...] = v[tm - CHUNK:, :]
    else:
        s = v * sp_a_ref[...] + sp_b_ref[...]
        v_ref[...] = v
    o = _dot((u * s).astype(BF16), wout_ref[...])
    o_ref[...] = y + gate_ref[...] * o


def _sgu(y, sh, sc, gate, g1, w, sp_a, sp_b, tm, tiles_per_seq, spatial):
    t = y.shape[0]
    tok = lambda n: pl.BlockSpec((tm, n), lambda i: (i, 0))
    if spatial:
        n_seq = t // (tm * tiles_per_seq)
        v_spec = pl.BlockSpec((None, CHUNK, SG_HALF), lambda i: (i // tiles_per_seq, 0, 0))
        v_shape = jax.ShapeDtypeStruct((n_seq, CHUNK, SG_HALF), F32)
    else:
        v_spec = tok(SG_HALF)
        v_shape = jax.ShapeDtypeStruct((t, SG_HALF), F32)
    return pl.pallas_call(
        functools.partial(_sgu_kernel, tm=tm, spatial=spatial),
        grid=(t // tm,),
        in_specs=[tok(D_MODEL), _mod_spec(sh, tiles_per_seq), _mod_spec(sc, tiles_per_seq),
                  _mod_spec(gate, tiles_per_seq), _full((1, D_MODEL)), _full(w["w_in"].shape),
                  _full((1, 2 * SG_HALF)), _full((1, SG_HALF)), _full(sp_a.shape), _full(sp_b.shape),
                  _full(w["w_out"].shape)],
        out_specs=[tok(D_MODEL), v_spec],
        out_shape=[jax.ShapeDtypeStruct((t, D_MODEL), F32), v_shape],
        compiler_params=_cparams("arbitrary"),
        name="sgu",
    )(y, sh, sc, gate, g1, w["w_in"], w["b_in"], w["g_v"], sp_a, sp_b, w["w_out"])


N_TOP = PEER_TOPK + 1
TOP_ROWS = 24
N_CAND = TOP_ROWS + 7 * 8 + (TOP_ROWS - 8)


def _top_values(x, n, emit):
    prev = None
    for r in range(n):
        cand = x if prev is None else jnp.where(x < prev, x, NEG_BIG)
        prev = jnp.max(cand, axis=0, keepdims=True)
        emit(r, prev)


def _peer_pre_kernel(y_ref, sh_ref, sc_ref, g2_ref, wq_ref, sk_ref, ht_ref, s2_ref, eb_ref, th_ref,
                     cc_ref, s_scr, t_scr, cand_scr, tv_scr):
    h = _rms(y_ref[...]) * g2_ref[...]
    h = h * (1.0 + sc_ref[...]) + sh_ref[...]
    ht_ref[...] = h.T.astype(BF16)
    q = _dot(h.astype(BF16), wq_ref[...])
    for hd in range(PEER_HEADS):
        for side in range(2):
            o = (hd * 2 + side) * N_KEYS
            s_scr[side, hd] = _dot_nt(sk_ref[side], q[:, o:o + N_KEYS].astype(BF16))
    t_scr[...] = jnp.full_like(t_scr, NEG_BIG)

    def tops(idx, _):
        side, hd = idx // PEER_HEADS, idx % PEER_HEADS

        def emit(r, row):
            t_scr[side, hd, r:r + 1, :] = row

        _top_values(s_scr[side, hd], N_TOP, emit)
        return 0

    lax.fori_loop(0, 2 * PEER_HEADS, tops, 0)

    def finish(hd, _):
        t1, t2 = t_scr[0, hd], t_scr[1, hd]
        cand_scr[0:TOP_ROWS] = t1[0:1] + t2
        for a in range(1, 8):
            cand_scr[TOP_ROWS + (a - 1) * 8:TOP_ROWS + a * 8] = t1[a:a + 1] + t2[0:8]
        cand_scr[TOP_ROWS + 56:] = t1[8:] + t2[0:1]
        cand = cand_scr[...]

        def emit(r, row):
            tv_scr[r:r + 1, :] = row

        _top_values(cand, N_TOP, emit)
        tau = 0.5 * (tv_scr[PEER_TOPK - 1:PEER_TOPK, :] + tv_scr[PEER_TOPK:PEER_TOPK + 1, :])
        top = t1[:1] + t2[:1]
        z = jnp.sum(jnp.where(cand >= tau, jnp.exp(cand - top), 0.0), axis=0, keepdims=True)
        s1, s2 = s_scr[0, hd], s_scr[1, hd]
        eb = jnp.exp(s2 - t2[:1])
        th = tau - s1
        cc = jnp.exp(s1 - t1[:1]) / z
        for col in range(s1.shape[1] // 128):
            cs = slice(col * 128, (col + 1) * 128)
            s2_ref[hd, col] = s2[:, cs]
            eb_ref[hd, col] = eb[:, cs]
            th_ref[hd, col] = th[:, cs]
            cc_ref[hd, col] = cc[:, cs]
        return 0

    lax.fori_loop(0, PEER_HEADS, finish, 0)


def _peer_pre(y, sh, sc, g2, wq, sk, tm, tiles_per_seq):
    t = y.shape[0]
    hk = pl.BlockSpec((PEER_HEADS, tm // 128, N_KEYS, 128), lambda i: (0, i, 0, 0))
    hk_shape = jax.ShapeDtypeStruct((PEER_HEADS, t // 128, N_KEYS, 128), F32)
    return pl.pallas_call(
        _peer_pre_kernel,
        grid=(t // tm,),
        in_specs=[pl.BlockSpec((tm, D_MODEL), lambda i: (i, 0)), _mod_spec(sh, tiles_per_seq),
                  _mod_spec(sc, tiles_per_seq), _full((1, D_MODEL)), _full(wq.shape), _full(sk.shape)],
        out_specs=[pl.BlockSpec((D_MODEL, tm), lambda i: (0, i)), hk, hk, hk, hk],
        out_shape=[jax.ShapeDtypeStruct((D_MODEL, t), BF16), hk_shape, hk_shape, hk_shape, hk_shape],
        scratch_shapes=[pltpu.VMEM((2, PEER_HEADS, N_KEYS, tm), F32),
                        pltpu.VMEM((2, PEER_HEADS, TOP_ROWS, tm), F32),
                        pltpu.VMEM((N_CAND, tm), F32),
                        pltpu.VMEM((TOP_ROWS, tm), F32)],
        compiler_params=_cparams("parallel"),
        name="peer_pre",
    )(y, sh, sc, g2, wq, sk)


PEER_EB = 1024
PEER_ROWS = PEER_EB // N_KEYS
PEER_BLOCKS = N_EXPERTS // PEER_EB


def _selection_weights(s2_ref, eb_ref, th_ref, cc_ref, col, row0):
    rows = []
    for ii in range(PEER_ROWS):
        w = None
        for hd in range(PEER_HEADS):
            th = th_ref[hd, col, row0 + ii:row0 + ii + 1, :]
            cc = cc_ref[hd, col, row0 + ii:row0 + ii + 1, :]
            t = jnp.where(s2_ref[hd, col] >= th, eb_ref[hd, col], 0.0) * cc
            w = t if w is None else w + t
        rows.append(w)
    return jnp.concatenate(rows, axis=0)


def _peer_dense_kernel(ht_ref, u_ref, vt_ref, s2_ref, eb_ref, th_ref, cc_ref, y_ref, gate_ref, o_ref,
                       acc_ref, *, tm):
    e = pl.program_id(1)

    @pl.when(e == 0)
    def _():
        acc_ref[...] = jnp.zeros_like(acc_ref)

    act = _gelu(_dot(u_ref[...], ht_ref[...]))
    w = jnp.concatenate([_selection_weights(s2_ref, eb_ref, th_ref, cc_ref, col, 0)
                         for col in range(tm // 128)], axis=1)
    acc_ref[...] += _dot(vt_ref[...], (act * w).astype(BF16))

    @pl.when(e == PEER_BLOCKS - 1)
    def _():
        o_ref[...] = y_ref[...] + gate_ref[...] * acc_ref[...].T


def _peer_dense(ht, u_b, vt_b, s2, eb, th, cc, y, gate, tm, tiles_per_seq):
    t = y.shape[0]
    cols = tm // 128
    hk = pl.BlockSpec((PEER_HEADS, cols, N_KEYS, 128), lambda i, e: (0, i, 0, 0))
    hr = pl.BlockSpec((PEER_HEADS, cols, PEER_ROWS, 128), lambda i, e: (0, i, e, 0))
    return pl.pallas_call(
        functools.partial(_peer_dense_kernel, tm=tm),
        grid=(t // tm, PEER_BLOCKS),
        in_specs=[pl.BlockSpec((D_MODEL, tm), lambda i, e: (0, i)),
                  pl.BlockSpec((PEER_EB, D_MODEL), lambda i, e: (e, 0)),
                  pl.BlockSpec((D_MODEL, PEER_EB), lambda i, e: (0, e)),
                  hk, hk, hr, hr,
                  pl.BlockSpec((tm, D_MODEL), lambda i, e: (i, 0)),
                  pl.BlockSpec((None,) + gate.shape[1:], lambda i, e: (i // tiles_per_seq, 0, 0))],
        out_specs=pl.BlockSpec((tm, D_MODEL), lambda i, e: (i, 0)),
        out_shape=jax.ShapeDtypeStruct((t, D_MODEL), F32),
        scratch_shapes=[pltpu.VMEM((D_MODEL, tm), F32)],
        compiler_params=_cparams("parallel", "arbitrary"),
        name="peer_dense",
    )(ht, u_b, vt_b, s2, eb, th, cc, y, gate)


def _peer(y, sh, sc, gate, g2, wq, sk, u_b, vt_b, tm, tiles_per_seq):
    ht, s2, eb, th, cc = _peer_pre(y, sh, sc, g2, wq, sk, tm, tiles_per_seq)
    return _peer_dense(ht, u_b, vt_b, s2, eb, th, cc, y, gate, tm, tiles_per_seq)


def _pad_heads(w, used):
    w = jnp.pad(w, [(0, 0)] * (w.ndim - 1) + [(0, HEAD_PAD - used)])
    return w.reshape(w.shape[:-2] + (N_HEADS * HEAD_PAD,))


def _pad_gain(g):
    return jnp.pad(g, (0, HEAD_PAD - QK_DIM)).reshape(1, HEAD_PAD)


def _rope_tables(pos):
    inv_freq = ROPE_THETA ** (-jnp.arange(ROPE_HALF, dtype=F32) * 2.0 / QK_ROPE_DIM)
    ang = pos.astype(F32)[:, None] * inv_freq[None, :]
    cos, sin = jnp.cos(ang), jnp.sin(ang)
    t = pos.shape[0]
    one = jnp.ones((t, QK_NOPE_DIM), F32)
    z64 = jnp.zeros((t, QK_NOPE_DIM), F32)
    z16 = jnp.zeros((t, ROPE_HALF), F32)
    z32 = jnp.zeros((t, HEAD_PAD - QK_DIM), F32)
    cos_t = jnp.concatenate([one, cos, cos, z32], axis=1)
    sa = jnp.concatenate([z64, -sin, z16, z32], axis=1)
    sb = jnp.concatenate([z64, z16, sin, z32], axis=1)
    return cos_t, sa, sb, cos, sin


def _mla_weights(w_in, g_q_a, g_kv_a, w_uq, g_q, w_uk):
    lo = Q_LORA_RANK + KV_LORA_RANK
    w_in_p = jnp.concatenate(
        [w_in[:, :lo], jnp.zeros((D_MODEL, QK_NOPE_DIM), F32), w_in[:, lo:],
         jnp.zeros((D_MODEL, HEAD_PAD - QK_DIM), F32)], axis=1).astype(BF16)
    return {
        "w_in": w_in_p,
        "g_q_a": g_q_a.reshape(1, -1),
        "g_kv_a": g_kv_a.reshape(1, -1),
        "w_uq": _pad_heads(w_uq.reshape(Q_LORA_RANK, N_HEADS, QK_DIM), QK_DIM).astype(BF16),
        "g_q": _pad_gain(g_q),
        "w_uk": _pad_heads(w_uk, QK_NOPE_DIM).astype(BF16),
    }


def _split_mod(m, n_prompt):
    m = m.reshape(m.shape[0], 6, D_MODEL)
    mp = [m[:n_prompt, j].reshape(n_prompt, 1, D_MODEL) for j in range(6)]
    ms = [m[n_prompt:, j].reshape(1, -1, D_MODEL) for j in range(6)]
    return mp, ms


def kernel(x_prompt, x_sample, c_prompt, c_sample, cache_kv_latent, cache_k_rope, page_table, norm1_g, norm2_g, w_mod, b_mod, mla_w_in, mla_g_q_a, mla_g_kv_a, mla_w_uq, mla_g_q, mla_w_uk, mla_w_uv, mla_g_k, mla_w_o, sg_w_in, sg_b_in, sg_g_v, sg_w_s, sg_b_s, sg_w_out, peer_w_q, peer_sub_keys, peer_u, peer_v):
    yp = x_prompt.reshape(BATCH * SEQ, D_MODEL)
    ys = x_sample.reshape(DEC_BATCH, D_MODEL)
    mods = _modulation(jnp.concatenate([c_prompt, c_sample], axis=0), w_mod, b_mod)

    tm_p, tps_p = 512, SEQ // 512
    tm_s, tps_s = DEC_BATCH, 1

    outs = {}
    for layer in range(DEPTH):
        mp, ms = _split_mod(mods[layer], BATCH)
        g1 = norm1_g[layer].reshape(1, D_MODEL)
        g2 = norm2_g[layer].reshape(1, D_MODEL)
        if layer % 2 == 0:
            a = layer // 2
            w = _mla_weights(mla_w_in[a], mla_g_q_a[a], mla_g_kv_a[a], mla_w_uq[a], mla_g_q[a], mla_w_uk[a])
            gk = _pad_gain(mla_g_k[a])
            w["g_k"] = gk
            wuv = jnp.transpose(mla_w_uv[a], (1, 0, 2)).astype(BF16)
            wo = mla_w_o[a].astype(BF16)
            cos_p, sa_p, sb_p, _, _ = _rope_tables(jnp.arange(SEQ))
            cos_s, sa_s, sb_s, _, _ = _rope_tables(jnp.full((1,), PAST_LEN))
            _, _, _, cos_c, sin_c = _rope_tables(jnp.arange(PAST_LEN))

            q, k, ckv_p, krp_p = _mla_pre(yp, mp[0], mp[1], g1, w, (cos_p, sa_p, sb_p), tm_p, tps_p, True, BF16)
            o_lat = _attn_prompt(q, k, ckv_p)
            yp = _mla_post(o_lat, wuv, wo, yp, mp[2], tm_p, tps_p)

            q_s, k_s, ckv_s, krp_s = _mla_pre(ys, ms[0], ms[1], g1, w, (cos_s, sa_s, sb_s), tm_s, tps_s, False, F32)
            wukt = jnp.pad(jnp.transpose(mla_w_uk[a], (1, 2, 0)),
                           ((0, 0), (0, HEAD_PAD - QK_NOPE_DIM), (0, 0))).astype(BF16)
            gk_nope = gk * (jnp.arange(HEAD_PAD) < QK_NOPE_DIM)[None, :]
            qa, qr = (jnp.transpose(x, (1, 0, 2)) for x in _absorb(q_s, gk_nope, wukt))
            wukt_flat = jnp.transpose(mla_w_uk[a], (1, 2, 0)).reshape(N_HEADS * QK_NOPE_DIM, KV_LORA_RANK).astype(BF16)
            gkr = mla_g_k[a][QK_NOPE_DIM:].reshape(QK_ROPE_DIM, 1)
            o_lat_s = _attn_sample(
                page_table, cache_kv_latent[a], jnp.swapaxes(cache_k_rope[a], 1, 2), wukt_flat, qa, qr,
                q_s.reshape(DEC_BATCH, N_HEADS, HEAD_PAD), k_s.reshape(DEC_BATCH, N_HEADS, HEAD_PAD),
                ckv_s.reshape(DEC_BATCH, 1, KV_LORA_RANK), gkr, cos_c.T, sin_c.T)
            ys = _mla_post(o_lat_s.reshape(DEC_BATCH, N_HEADS * KV_LORA_RANK).astype(BF16), wuv, wo, ys,
                           ms[2], tm_s, tps_s)

            outs.setdefault("kv_p", []).append(ckv_p.reshape(BATCH, SEQ, KV_LORA_RANK))
            outs.setdefault("kr_p", []).append(krp_p[:, QK_NOPE_DIM:QK_DIM].reshape(BATCH, SEQ, QK_ROPE_DIM))
            outs.setdefault("kv_s", []).append(ckv_s.reshape(DEC_BATCH, 1, KV_LORA_RANK))
            outs.setdefault("kr_s", []).append(krp_s[:, QK_NOPE_DIM:QK_DIM].reshape(DEC_BATCH, 1, QK_ROPE_DIM))
        else:
            bidx = layer // 2
            w = {"w_in": sg_w_in[bidx].astype(BF16), "b_in": sg_b_in[bidx].reshape(1, -1),
                 "g_v": sg_g_v[bidx].reshape(1, -1), "w_out": sg_w_out[bidx].astype(BF16)}
            sp_a = sg_w_s[bidx]
            sp_b = sg_b_s[bidx].reshape(SG_GROUPS, CHUNK, 1)
            yp, vp = _sgu(yp, mp[0], mp[1], mp[2], g1, w, sp_a, sp_b, 256, SEQ // 256, True)
            coef = jnp.repeat(sg_w_s[bidx][:, 0, 0], CHUNK).reshape(1, SG_HALF)
            bias = jnp.repeat(sg_b_s[bidx][:, 0], CHUNK).reshape(1, SG_HALF)
            ys, vs = _sgu(ys, ms[0], ms[1], ms[2], g1, w, coef, bias, tm_s, tps_s, False)
            outs.setdefault("v_p", []).append(vp)
            outs.setdefault("v_s", []).append(vs.reshape(DEC_BATCH, 1, SG_HALF))

        wq = peer_w_q[layer].astype(BF16)
        sk = peer_sub_keys[layer].astype(BF16)
        u_b = peer_u[layer].astype(BF16)
        vt_b = peer_v[layer].T.astype(BF16)
        yp = _peer(yp, mp[3], mp[4], mp[5], g2, wq, sk, u_b, vt_b, tm_p, tps_p)
        ys = _peer(ys, ms[3], ms[4], ms[5], g2, wq, sk, u_b, vt_b, tm_s, tps_s)

    return (yp.reshape(BATCH, SEQ, D_MODEL), ys.reshape(DEC_BATCH, 1, D_MODEL),
            jnp.stack(outs["kv_p"]), jnp.stack(outs["kr_p"]), jnp.stack(outs["kv_s"]), jnp.stack(outs["kr_s"]),
            jnp.stack(outs["v_p"]), jnp.stack(outs["v_s"]))
```

```python
import functools

import jax
import jax.numpy as jnp
from jax import lax
from jax.experimental import pallas as pl
from jax.experimental.pallas import tpu as pltpu

F32 = jnp.float32
BF16 = jnp.bfloat16

D_MODEL = 1024
BATCH = 8
SEQ = 2048
DEPTH = 2
DEC_BATCH = 128
PAST_LEN = 16384
PAGE_SIZE = 128
N_PAGES = PAST_LEN // PAGE_SIZE

N_HEADS = 8
Q_LORA_RANK = 256
KV_LORA_RANK = 128
QK_NOPE_DIM = 64
QK_ROPE_DIM = 32
QK_DIM = QK_NOPE_DIM + QK_ROPE_DIM
V_HEAD_DIM = 128
ROPE_THETA = 10000.0
SM_SCALE = QK_DIM ** -0.5
HEAD_PAD = 128
ROPE_HALF = QK_ROPE_DIM // 2

CHUNK = 128
SG_HALF = 1536
SG_GROUPS = 12

PEER_HEADS = 8
N_KEYS = 128
N_EXPERTS = N_KEYS * N_KEYS
PEER_TOPK = 16
RMS_EPS = 1e-6

NEG_BIG = -1e30
VMEM_LIMIT = 56 * 1024 * 1024


def _cparams(*sem):
    return pltpu.CompilerParams(dimension_semantics=sem, vmem_limit_bytes=VMEM_LIMIT)


def _rms(x):
    return x * lax.rsqrt(jnp.mean(x * x, axis=-1, keepdims=True) + RMS_EPS)


def _gelu(x):
    cdf = 0.5 * (1.0 + jnp.tanh(0.7978845608028654 * (x + 0.044715 * (x * x * x))))
    return x * cdf


def _gelu_twice(x):
    c = 0.7978845608028654
    return x * (1.0 + jnp.tanh(x * (c + (0.044715 * c) * (x * x))))


def _dot(a, b):
    return jnp.dot(a, b, preferred_element_type=F32)


def _dot_nt(a, b):
    return lax.dot_general(a, b, (((1,), (1,)), ((), ())), preferred_element_type=F32)


def _full(shape):
    n = len(shape)
    return pl.BlockSpec(shape, lambda *_: (0,) * n)


def _mod_spec(arr, tiles_per_seq):
    return pl.BlockSpec((None,) + arr.shape[1:], lambda t, *_: (t // tiles_per_seq, 0, 0))


def _mod_kernel(c_ref, w_ref, b_ref, o_ref):
    c = c_ref[...]
    a = (c * jax.nn.sigmoid(c)).astype(BF16)
    o_ref[...] = _dot(a, w_ref[...].astype(BF16)) + b_ref[...]


def _modulation(c_all, w_mod, b_mod):
    r = c_all.shape[0]
    tn = 1536
    return pl.pallas_call(
        _mod_kernel,
        grid=(DEPTH, 6 * D_MODEL // tn),
        in_specs=[
            pl.BlockSpec((r, D_MODEL), lambda l, n: (0, 0)),
            pl.BlockSpec((None, D_MODEL, tn), lambda l, n: (l, 0, n)),
            pl.BlockSpec((None, 1, tn), lambda l, n: (l, 0, n)),
        ],
        out_specs=pl.BlockSpec((None, r, tn), lambda l, n: (l, 0, n)),
        out_shape=jax.ShapeDtypeStruct((DEPTH, r, 6 * D_MODEL), F32),
        compiler_params=_cparams("parallel", "parallel"),
        name="modulation",
    )(c_all, w_mod, b_mod.reshape(DEPTH, 1, 6 * D_MODEL))


def _rope(x, cos, sa, sb):
    return x * cos + pltpu.roll(x, HEAD_PAD - ROPE_HALF, 1) * sa + pltpu.roll(x, ROPE_HALF, 1) * sb


def _mla_pre_kernel(y_ref, sh_ref, sc_ref, g1_ref, win_ref, gqa_ref, gkva_ref, wuq_ref, gq_ref,
                    wuk_ref, gk_ref, cos_ref, sa_ref, sb_ref, q_ref, k_ref, ckv_ref, kr_ref):
    h = _rms(y_ref[...]) * g1_ref[...]
    h = h * (1.0 + sc_ref[...]) + sh_ref[...]
    z = _dot(h.astype(BF16), win_ref[...])
    cq = _rms(z[:, :Q_LORA_RANK]) * gqa_ref[...]
    ckv = _rms(z[:, Q_LORA_RANK:Q_LORA_RANK + KV_LORA_RANK]) * gkva_ref[...]
    krp = z[:, Q_LORA_RANK + KV_LORA_RANK:]
    ckv_ref[...] = ckv
    kr_ref[...] = krp
    q = _dot(cq.astype(BF16), wuq_ref[...])
    kn = _dot(ckv.astype(BF16), wuk_ref[...])
    cos, sa, sb = cos_ref[...], sa_ref[...], sb_ref[...]
    gq, gk = gq_ref[...], gk_ref[...]
    for hd in range(N_HEADS):
        sl = slice(hd * HEAD_PAD, (hd + 1) * HEAD_PAD)
        qh = q[:, sl]
        qh = qh * lax.rsqrt(jnp.sum(qh * qh, axis=-1, keepdims=True) / QK_DIM + RMS_EPS) * gq
        q_ref[:, sl] = _rope(qh, cos, sa, sb).astype(q_ref.dtype)
        kh = kn[:, sl] + krp
        kh = kh * lax.rsqrt(jnp.sum(kh * kh, axis=-1, keepdims=True) / QK_DIM + RMS_EPS) * gk
        k_ref[:, sl] = _rope(kh, cos, sa, sb).astype(k_ref.dtype)


def _mla_pre(y, sh, sc, g1, w, rope_tabs, tm, tiles_per_seq, rope_tiled, qk_dtype):
    t = y.shape[0]
    cos, sa, sb = rope_tabs
    if rope_tiled:
        rspec = pl.BlockSpec((tm, HEAD_PAD), lambda i: (i % tiles_per_seq, 0))
    else:
        rspec = _full((1, HEAD_PAD))
    wide = N_HEADS * HEAD_PAD
    tok = lambda n: pl.BlockSpec((tm, n), lambda i: (i, 0))
    return pl.pallas_call(
        _mla_pre_kernel,
        grid=(t // tm,),
        in_specs=[
            tok(D_MODEL), _mod_spec(sh, tiles_per_seq), _mod_spec(sc, tiles_per_seq),
            _full((1, D_MODEL)), _full(w["w_in"].shape), _full((1, Q_LORA_RANK)),
            _full((1, KV_LORA_RANK)), _full(w["w_uq"].shape), _full((1, HEAD_PAD)),
            _full(w["w_uk"].shape), _full((1, HEAD_PAD)), rspec, rspec, rspec,
        ],
        out_specs=[tok(wide), tok(wide), tok(KV_LORA_RANK), tok(HEAD_PAD)],
        out_shape=[
            jax.ShapeDtypeStruct((t, wide), qk_dtype),
            jax.ShapeDtypeStruct((t, wide), qk_dtype),
            jax.ShapeDtypeStruct((t, KV_LORA_RANK), F32),
            jax.ShapeDtypeStruct((t, HEAD_PAD), F32),
        ],
        compiler_params=_cparams("parallel"),
        name="mla_pre",
    )(y, sh, sc, g1, w["w_in"], w["g_q_a"], w["g_kv_a"], w["w_uq"], w["g_q"], w["w_uk"], w["g_k"],
      cos, sa, sb)


ATT_TQ = 256
ATT_TK = 256


def _attn_prompt_kernel(q_ref, k_ref, ckv_ref, o_ref, m_ref, acc_ref):
    i = pl.program_id(1)
    m_ref[...] = jnp.full_like(m_ref, NEG_BIG)
    acc_ref[...] = jnp.zeros_like(acc_ref)
    ones = jnp.ones((ATT_TK, KV_LORA_RANK), BF16)

    def block(j, diagonal):
        off = pl.multiple_of(j * ATT_TK, ATT_TK)
        cb = jnp.concatenate([ckv_ref[pl.ds(off, ATT_TK), :].astype(BF16), ones], axis=1)
        for hd in range(N_HEADS):
            sl = slice(hd * HEAD_PAD, (hd + 1) * HEAD_PAD)
            s = _dot_nt(q_ref[:, sl], k_ref[pl.ds(off, ATT_TK), sl]) * SM_SCALE
            if diagonal:
                row = lax.broadcasted_iota(jnp.int32, (ATT_TQ, ATT_TK), 0)
                col = lax.broadcasted_iota(jnp.int32, (ATT_TQ, ATT_TK), 1)
                s = jnp.where(col <= row, s, NEG_BIG)
            m_old = m_ref[hd]
            s_max = jnp.max(jnp.maximum(s[:, :128], s[:, 128:]), axis=-1, keepdims=True)
            m_new = jnp.maximum(m_old, jnp.broadcast_to(s_max, m_old.shape))
            p = jnp.concatenate([jnp.exp(s[:, :128] - m_new), jnp.exp(s[:, 128:] - m_new)], axis=1)
            alpha = jnp.exp(m_old - m_new)
            pv = _dot(p.astype(BF16), cb)
            acc_ref[hd] = jnp.concatenate([alpha, alpha], axis=1) * acc_ref[hd] + pv
            m_ref[hd] = m_new

    def body(j, carry):
        block(j, False)
        return carry

    lax.fori_loop(0, i, body, 0)
    block(i, True)
    for hd in range(N_HEADS):
        acc = acc_ref[hd]
        o_ref[:, hd * HEAD_PAD:(hd + 1) * HEAD_PAD] = (
            acc[:, :KV_LORA_RANK] / acc[:, KV_LORA_RANK:]).astype(o_ref.dtype)


def _attn_prompt(q, k, ckv):
    nq = SEQ // ATT_TQ
    wide = N_HEADS * HEAD_PAD
    return pl.pallas_call(
        _attn_prompt_kernel,
        grid=(BATCH, nq),
        in_specs=[
            pl.BlockSpec((ATT_TQ, wide), lambda b, i: (b * nq + i, 0)),
            pl.BlockSpec((SEQ, wide), lambda b, i: (b, 0)),
            pl.BlockSpec((SEQ, KV_LORA_RANK), lambda b, i: (b, 0)),
        ],
        out_specs=pl.BlockSpec((ATT_TQ, wide), lambda b, i: (b * nq + i, 0)),
        out_shape=jax.ShapeDtypeStruct((BATCH * SEQ, wide), BF16),
        scratch_shapes=[pltpu.VMEM((N_HEADS, ATT_TQ, 128), F32),
                        pltpu.VMEM((N_HEADS, ATT_TQ, 2 * KV_LORA_RANK), F32)],
        compiler_params=_cparams("parallel", "parallel"),
        name="attn_prompt",
    )(q, k, ckv)


SA_PAGES = 32
SA_POS = SA_PAGES * PAGE_SIZE
SA_STEPS = N_PAGES // SA_PAGES


def _absorb_kernel(q_ref, gk_ref, wukt_ref, qa_ref, qr_ref):
    gk = gk_ref[...]
    for hd in range(N_HEADS):
        qh = q_ref[:, hd * HEAD_PAD:(hd + 1) * HEAD_PAD]
        qa_ref[hd] = _dot((qh * gk).astype(BF16), wukt_ref[hd])
        qr_ref[hd] = pltpu.roll(qh, HEAD_PAD - QK_NOPE_DIM, 1)


def _absorb(q_s, gk_nope, wukt):
    b = q_s.shape[0]
    shp = jax.ShapeDtypeStruct((N_HEADS, b, HEAD_PAD), F32)
    return pl.pallas_call(
        _absorb_kernel,
        in_specs=[_full(q_s.shape), _full(gk_nope.shape), _full(wukt.shape)],
        out_specs=[_full(shp.shape), _full(shp.shape)],
        out_shape=[shp, shp],
        grid=(1,),
        compiler_params=_cparams("arbitrary"),
        name="absorb_q",
    )(q_s, gk_nope, wukt)


def _attn_sample_kernel(pt_ref, *refs):
    kv_refs = refs[:SA_PAGES]
    kr_refs = refs[SA_PAGES:2 * SA_PAGES]
    (wukt_ref, qa_ref, qr_ref, qs_ref, knew_ref, cnew_ref, gkr_ref, cos_ref, sin_ref,
     o_ref, m_ref, l_ref, acc_ref, ssq_ref) = refs[2 * SA_PAGES:]
    c = pl.program_id(1)

    @pl.when(c == 0)
    def _():
        m_ref[...] = jnp.full_like(m_ref, NEG_BIG)
        l_ref[...] = jnp.zeros_like(l_ref)
        acc_ref[...] = jnp.zeros_like(acc_ref)

    cc = jnp.concatenate([r[...] for r in kv_refs], axis=0).astype(BF16)
    qa16 = jnp.concatenate([qa_ref[...], jnp.zeros((8, KV_LORA_RANK), F32)], axis=0).astype(BF16)
    lhs = jnp.concatenate([wukt_ref[...], qa16], axis=0)
    res = _dot_nt(lhs, cc)
    for hd in range(N_HEADS):
        x = res[hd * QK_NOPE_DIM:(hd + 1) * QK_NOPE_DIM]
        ssq_ref[hd:hd + 1, :] = jnp.sum(x * x, axis=0, keepdims=True)
    sn = res[N_HEADS * QK_NOPE_DIM:N_HEADS * QK_NOPE_DIM + N_HEADS]

    krt = jnp.concatenate([r[...] for r in kr_refs], axis=1)
    kr2 = jnp.sum(krt * krt, axis=0, keepdims=True)
    krg = krt * gkr_ref[...]
    x1, x2 = krg[:ROPE_HALF], krg[ROPE_HALF:]
    cos, sin = cos_ref[...], sin_ref[...]
    rk = jnp.concatenate([x1 * cos - x2 * sin, x2 * cos + x1 * sin], axis=0).astype(BF16)
    sr = _dot(qr_ref[...][:, :QK_ROPE_DIM].astype(BF16), rk)

    rinv = lax.rsqrt((ssq_ref[...] + kr2) / QK_DIM + RMS_EPS)
    s = (sn + sr) * rinv * SM_SCALE
    m_old = m_ref[...]
    m_new = jnp.maximum(m_old, jnp.max(s, axis=-1, keepdims=True))
    p = jnp.exp(s - m_new)
    alpha = jnp.exp(m_old - m_new)
    l_ref[...] = alpha * l_ref[...] + jnp.sum(p, axis=-1, keepdims=True)
    acc_ref[...] = alpha * acc_ref[...] + _dot(p.astype(BF16), cc)
    m_ref[...] = m_new

    @pl.when(c == SA_STEPS - 1)
    def _():
        s_new = jnp.sum(qs_ref[...] * knew_ref[...], axis=-1, keepdims=True) * SM_SCALE
        m_o = m_ref[...]
        m_n = jnp.maximum(m_o, s_new)
        p_new = jnp.exp(s_new - m_n)
        a = jnp.exp(m_o - m_n)
        l = a * l_ref[...] + p_new
        acc = a * acc_ref[...] + p_new * cnew_ref[...]
        o_ref[...] = acc / l


def _attn_sample(page_table, cache_kv, cache_kr, wukt_flat, qa, qr, qs, knew, cnew, gkr, cos_t, sin_t):
    b = qa.shape[0]

    def page_spec(g, rows, width):
        return pl.BlockSpec((None, rows, width),
                            lambda s, c, pt, g=g: (pt[s, c * SA_PAGES + g], 0, 0))

    per_seq = pl.BlockSpec((None, N_HEADS, HEAD_PAD), lambda s, c, pt: (s, 0, 0))
    in_specs = (
        [page_spec(g, PAGE_SIZE, KV_LORA_RANK) for g in range(SA_PAGES)]
        + [page_spec(g, QK_ROPE_DIM, PAGE_SIZE) for g in range(SA_PAGES)]
        + [
            pl.BlockSpec(wukt_flat.shape, lambda s, c, pt: (0, 0)),
            per_seq, per_seq, per_seq, per_seq,
            pl.BlockSpec((None, 1, KV_LORA_RANK), lambda s, c, pt: (s, 0, 0)),
            pl.BlockSpec(gkr.shape, lambda s, c, pt: (0, 0)),
            pl.BlockSpec((ROPE_HALF, SA_POS), lambda s, c, pt: (0, c)),
            pl.BlockSpec((ROPE_HALF, SA_POS), lambda s, c, pt: (0, c)),
        ]
    )
    grid_spec = pltpu.PrefetchScalarGridSpec(
        num_scalar_prefetch=1,
        grid=(b, SA_STEPS),
        in_specs=in_specs,
        out_specs=pl.BlockSpec((None, N_HEADS, KV_LORA_RANK), lambda s, c, pt: (s, 0, 0)),
        scratch_shapes=[
            pltpu.VMEM((N_HEADS, 1), F32), pltpu.VMEM((N_HEADS, 1), F32),
            pltpu.VMEM((N_HEADS, KV_LORA_RANK), F32), pltpu.VMEM((N_HEADS, SA_POS), F32),
        ],
    )
    return pl.pallas_call(
        _attn_sample_kernel,
        grid_spec=grid_spec,
        out_shape=jax.ShapeDtypeStruct((b, N_HEADS, KV_LORA_RANK), F32),
        compiler_params=_cparams("parallel", "arbitrary"),
        name="attn_sample",
    )(page_table, *([cache_kv] * SA_PAGES), *([cache_kr] * SA_PAGES), wukt_flat, qa, qr, qs, knew,
      cnew, gkr, cos_t, sin_t)


def _mla_post_kernel(ol_ref, wuv_ref, wo_ref, y_ref, gate_ref, o_ref):
    parts = [_dot(ol_ref[:, hd * KV_LORA_RANK:(hd + 1) * KV_LORA_RANK], wuv_ref[hd])
             for hd in range(N_HEADS)]
    o = jnp.concatenate(parts, axis=-1).astype(BF16)
    o_ref[...] = y_ref[...] + gate_ref[...] * _dot(o, wo_ref[...])


def _mla_post(o_lat, wuv, wo, y, gate, tm, tiles_per_seq):
    t = y.shape[0]
    tok = lambda n: pl.BlockSpec((tm, n), lambda i: (i, 0))
    return pl.pallas_call(
        _mla_post_kernel,
        grid=(t // tm,),
        in_specs=[tok(N_HEADS * KV_LORA_RANK), _full(wuv.shape), _full(wo.shape), tok(D_MODEL),
                  _mod_spec(gate, tiles_per_seq)],
        out_specs=tok(D_MODEL),
        out_shape=jax.ShapeDtypeStruct((t, D_MODEL), F32),
        compiler_params=_cparams("parallel"),
        name="mla_post",
    )(o_lat, wuv, wo, y, gate)


def _sgu_kernel(y_ref, sh_ref, sc_ref, gate_ref, g1_ref, win_ref, bin_ref, gv_ref, sp_a_ref, sp_b_ref,
                wout_ref, o_ref, v_ref, *, tm, spatial):
    y = y_ref[...]
    h = _rms(y) * g1_ref[...]
    h = h * (1.0 + sc_ref[...]) + sh_ref[...]
    z = _gelu(_dot(h.astype(BF16), win_ref[...]) + bin_ref[...])
    u = z[:, :SG_HALF]
    v = _rms(z[:, SG_HALF:]) * gv_ref[...]
    if spatial:
        r = lax.broadcasted_iota(jnp.int32, (CHUNK, CHUNK), 0)
        cidx = lax.broadcasted_iota(jnp.int32, (CHUNK, CHUNK), 1)
        vb = v.astype(BF16)
        cols = []
        for g in range(SG_GROUPS):
            wc = jnp.where(cidx <= r, sp_a_ref[g], 0.0).astype(BF16)
            rows = [_dot(wc, vb[c * CHUNK:(c + 1) * CHUNK, g * CHUNK:(g + 1) * CHUNK]) + sp_b_ref[g]
                    for c in range(tm // CHUNK)]
            cols.append(jnp.concatenate(rows, axis=0))
        s = jnp.concatenate(cols, axis=-1)
        v_ref[...] = v[tm - CHUNK:, :]
    else:
        s = v * sp_a_ref[...] + sp_b_ref[...]
        v_ref[...] = v
    o = _dot((u * s).astype(BF16), wout_ref[...])
    o_ref[...] = y + gate_ref[...] * o


def _sgu(y, sh, sc, gate, g1, w, sp_a, sp_b, tm, tiles_per_seq, spatial):
    t = y.shape[0]
    tok = lambda n: pl.BlockSpec((tm, n), lambda i: (i, 0))
    if spatial:
        n_seq = t // (tm * tiles_per_seq)
        v_spec = pl.BlockSpec((None, CHUNK, SG_HALF), lambda i: (i // tiles_per_seq, 0, 0))
        v_shape = jax.ShapeDtypeStruct((n_seq, CHUNK, SG_HALF), F32)
    else:
        v_spec = tok(SG_HALF)
        v_shape = jax.ShapeDtypeStruct((t, SG_HALF), F32)
    return pl.pallas_call(
        functools.partial(_sgu_kernel, tm=tm, spatial=spatial),
        grid=(t // tm,),
        in_specs=[tok(D_MODEL), _mod_spec(sh, tiles_per_seq), _mod_spec(sc, tiles_per_seq),
                  _mod_spec(gate, tiles_per_seq), _full((1, D_MODEL)), _full(w["w_in"].shape),
                  _full((1, 2 * SG_HALF)), _full((1, SG_HALF)), _full(sp_a.shape), _full(sp_b.shape),
                  _full(w["w_out"].shape)],
        out_specs=[tok(D_MODEL), v_spec],
        out_shape=[jax.ShapeDtypeStruct((t, D_MODEL), F32), v_shape],
        compiler_params=_cparams("arbitrary"),
        name="sgu",
    )(y, sh, sc, gate, g1, w["w_in"], w["b_in"], w["g_v"], sp_a, sp_b, w["w_out"])


N_TOP = PEER_TOPK + 1
TOP_ROWS = 24
N_CAND = TOP_ROWS + 7 * 8 + (TOP_ROWS - 8)


def _top_values(x, n, emit):
    prev = None
    for r in range(n):
        cand = x if prev is None else jnp.where(x < prev, x, NEG_BIG)
        prev = jnp.max(cand, axis=0, keepdims=True)
        emit(r, prev)


def _pair_words(x):
    bits = pltpu.bitcast(x.astype(BF16).astype(F32), jnp.uint32)
    return bits | (bits >> 16)


def _peer_pre_kernel(y_ref, sh_ref, sc_ref, g2_ref, wq_ref, sk_ref, ht_ref, rank_ref, eb_ref, cnt_ref,
                     cc_ref, s_scr, t_scr, cand_scr, tv_scr):
    h = _rms(y_ref[...]) * g2_ref[...]
    h = h * (1.0 + sc_ref[...]) + sh_ref[...]
    ht_ref[...] = h.T.astype(BF16)
    q = _dot(h.astype(BF16), wq_ref[...])
    for hd in range(PEER_HEADS):
        for side in range(2):
            o = (hd * 2 + side) * N_KEYS
            s_scr[side, hd] = _dot_nt(sk_ref[side], q[:, o:o + N_KEYS].astype(BF16))
    t_scr[...] = jnp.full_like(t_scr, NEG_BIG)

    def tops(idx, _):
        side, hd = idx // PEER_HEADS, idx % PEER_HEADS

        def emit(r, row):
            t_scr[side, hd, r:r + 1, :] = row

        _top_values(s_scr[side, hd], N_TOP, emit)
        return 0

    lax.fori_loop(0, 2 * PEER_HEADS, tops, 0)

    def finish(hd, _):
        t1, t2 = t_scr[0, hd], t_scr[1, hd]
        cand_scr[0:TOP_ROWS] = t1[0:1] + t2
        for a in range(1, 8):
            cand_scr[TOP_ROWS + (a - 1) * 8:TOP_ROWS + a * 8] = t1[a:a + 1] + t2[0:8]
        cand_scr[TOP_ROWS + 56:] = t1[8:] + t2[0:1]
        cand = cand_scr[...]

        def emit(r, row):
            tv_scr[r:r + 1, :] = row

        _top_values(cand, N_TOP, emit)
        tau = 0.5 * (tv_scr[PEER_TOPK - 1:PEER_TOPK, :] + tv_scr[PEER_TOPK:PEER_TOPK + 1, :])
        top = t1[:1] + t2[:1]
        z = jnp.sum(jnp.where(cand >= tau, jnp.exp(cand - top), 0.0), axis=0, keepdims=True)
        s1, s2 = s_scr[0, hd], s_scr[1, hd]
        eb = jnp.exp(s2 - t2[:1])
        th = tau - s1
        cc = jnp.exp(s1 - t1[:1]) * (0.5 / z)
        rank = jnp.zeros_like(s2)
        cnt = jnp.zeros_like(s1)
        for b in range(PEER_TOPK):
            tb = t2[b:b + 1]
            rank = rank + jnp.where(tb > s2, 1.0, 0.0)
            cnt = cnt + jnp.where(tb >= th, 1.0, 0.0)
        cntw = _pair_words(cnt)
        ccw = _pair_words(cc)
        for col in range(s1.shape[1] // 128):
            cs = slice(col * 128, (col + 1) * 128)
            rank_ref[hd, col] = pltpu.bitcast(rank[:, cs].astype(BF16), jnp.uint32)
            eb_ref[hd, col] = pltpu.bitcast(eb[:, cs].astype(BF16), jnp.uint32)
            cnt_ref[hd, col] = cntw[:, cs]
            cc_ref[hd, col] = ccw[:, cs]
        return 0

    lax.fori_loop(0, PEER_HEADS, finish, 0)


def _peer_pre(y, sh, sc, g2, wq, sk, tm, tiles_per_seq):
    t = y.shape[0]
    hk = pl.BlockSpec((PEER_HEADS, tm // 128, N_KEYS, 128), lambda i: (0, i, 0, 0))
    hp = pl.BlockSpec((PEER_HEADS, tm // 128, N_KEYS // 2, 128), lambda i: (0, i, 0, 0))
    packed = jax.ShapeDtypeStruct((PEER_HEADS, t // 128, N_KEYS // 2, 128), jnp.uint32)
    words = jax.ShapeDtypeStruct((PEER_HEADS, t // 128, N_KEYS, 128), jnp.uint32)
    return pl.pallas_call(
        _peer_pre_kernel,
        grid=(t // tm,),
        in_specs=[pl.BlockSpec((tm, D_MODEL), lambda i: (i, 0)), _mod_spec(sh, tiles_per_seq),
                  _mod_spec(sc, tiles_per_seq), _full((1, D_MODEL)), _full(wq.shape), _full(sk.shape)],
        out_specs=[pl.BlockSpec((D_MODEL, tm), lambda i: (0, i)), hp, hp, hk, hk],
        out_shape=[jax.ShapeDtypeStruct((D_MODEL, t), BF16), packed, packed, words, words],
        scratch_shapes=[pltpu.VMEM((2, PEER_HEADS, N_KEYS, tm), F32),
                        pltpu.VMEM((2, PEER_HEADS, TOP_ROWS, tm), F32),
                        pltpu.VMEM((N_CAND, tm), F32),
                        pltpu.VMEM((TOP_ROWS, tm), F32)],
        compiler_params=_cparams("parallel"),
        name="peer_pre",
    )(y, sh, sc, g2, wq, sk)


PEER_EB = 1024
PEER_ROWS = PEER_EB // N_KEYS
PEER_BLOCKS = N_EXPERTS // PEER_EB


BF16_ROWS = 16


def _selection_weights(rank_ref, eb_ref, cnt_ref, cc_ref, col):
    def row_of(ref, hd, ii):
        word = jnp.broadcast_to(ref[hd, col, ii:ii + 1, :], (8, 128))
        tile = pltpu.bitcast(word, BF16)
        return jnp.concatenate([tile] * (N_KEYS // BF16_ROWS), axis=0)

    rows = []
    for ii in range(PEER_ROWS):
        w = None
        for hd in range(PEER_HEADS):
            eb = pltpu.bitcast(eb_ref[hd, col], BF16)
            rank = pltpu.bitcast(rank_ref[hd, col], BF16)
            t = jnp.where(rank < row_of(cnt_ref, hd, ii), eb, jnp.zeros_like(eb))
            t = t * row_of(cc_ref, hd, ii)
            w = t if w is None else w + t
        rows.append(w)
    return jnp.concatenate(rows, axis=0)


def _peer_dense_kernel(ht_ref, u_ref, vt_ref, rank_ref, eb_ref, cnt_ref, cc_ref, y_ref, gate_ref, o_ref,
                       acc_ref, *, tm):
    e = pl.program_id(1)

    @pl.when(e == 0)
    def _():
        acc_ref[...] = jnp.zeros_like(acc_ref)

    act = _gelu_twice(_dot(u_ref[...], ht_ref[...]))
    w = jnp.concatenate([_selection_weights(rank_ref, eb_ref, cnt_ref, cc_ref, col)
                         for col in range(tm // 128)], axis=1)
    acc_ref[...] += _dot(vt_ref[...], act.astype(BF16) * w)

    @pl.when(e == PEER_BLOCKS - 1)
    def _():
        o_ref[...] = y_ref[...] + gate_ref[...] * acc_ref[...].T


def _peer_dense(ht, u_b, vt_b, rank, eb, cnt, cc, y, gate, tm, tiles_per_seq):
    t = y.shape[0]
    cols = tm // 128
    hk = pl.BlockSpec((PEER_HEADS, cols, N_KEYS // 2, 128), lambda i, e: (0, i, 0, 0))
    hr = pl.BlockSpec((PEER_HEADS, cols, PEER_ROWS, 128), lambda i, e: (0, i, e, 0))
    return pl.pallas_call(
        functools.partial(_peer_dense_kernel, tm=tm),
        grid=(t // tm, PEER_BLOCKS),
        in_specs=[pl.BlockSpec((D_MODEL, tm), lambda i, e: (0, i)),
                  pl.BlockSpec((PEER_EB, D_MODEL), lambda i, e: (e, 0)),
                  pl.BlockSpec((D_MODEL, PEER_EB), lambda i, e: (0, e)),
                  hk, hk, hr, hr,
                  pl.BlockSpec((tm, D_MODEL), lambda i, e: (i, 0)),
                  pl.BlockSpec((None,) + gate.shape[1:], lambda i, e: (i // tiles_per_seq, 0, 0))],
        out_specs=pl.BlockSpec((tm, D_MODEL), lambda i, e: (i, 0)),
        out_shape=jax.ShapeDtypeStruct((t, D_MODEL), F32),
        scratch_shapes=[pltpu.VMEM((D_MODEL, tm), F32)],
        compiler_params=_cparams("parallel", "arbitrary"),
        name="peer_dense",
    )(ht, u_b, vt_b, rank, eb, cnt, cc, y, gate)


def _peer(y, sh, sc, gate, g2, wq, sk, u_b, vt_b, tm, tiles_per_seq):
    ht, rank, eb, cnt, cc = _peer_pre(y, sh, sc, g2, wq, sk, tm, tiles_per_seq)
    return _peer_dense(ht, u_b, vt_b, rank, eb, cnt, cc, y, gate, tm, tiles_per_seq)


def _pad_heads(w, used):
    w = jnp.pad(w, [(0, 0)] * (w.ndim - 1) + [(0, HEAD_PAD - used)])
    return w.reshape(w.shape[:-2] + (N_HEADS * HEAD_PAD,))


def _pad_gain(g):
    return jnp.pad(g, (0, HEAD_PAD - QK_DIM)).reshape(1, HEAD_PAD)


def _rope_tables(pos):
    inv_freq = ROPE_THETA ** (-jnp.arange(ROPE_HALF, dtype=F32) * 2.0 / QK_ROPE_DIM)
    ang = pos.astype(F32)[:, None] * inv_freq[None, :]
    cos, sin = jnp.cos(ang), jnp.sin(ang)
    t = pos.shape[0]
    one = jnp.ones((t, QK_NOPE_DIM), F32)
    z64 = jnp.zeros((t, QK_NOPE_DIM), F32)
    z16 = jnp.zeros((t, ROPE_HALF), F32)
    z32 = jnp.zeros((t, HEAD_PAD - QK_DIM), F32)
    cos_t = jnp.concatenate([one, cos, cos, z32], axis=1)
    sa = jnp.concatenate([z64, -sin, z16, z32], axis=1)
    sb = jnp.concatenate([z64, z16, sin, z32], axis=1)
    return cos_t, sa, sb, cos, sin


def _mla_weights(w_in, g_q_a, g_kv_a, w_uq, g_q, w_uk):
    lo = Q_LORA_RANK + KV_LORA_RANK
    w_in_p = jnp.concatenate(
        [w_in[:, :lo], jnp.zeros((D_MODEL, QK_NOPE_DIM), F32), w_in[:, lo:],
         jnp.zeros((D_MODEL, HEAD_PAD - QK_DIM), F32)], axis=1).astype(BF16)
    return {
        "w_in": w_in_p,
        "g_q_a": g_q_a.reshape(1, -1),
        "g_kv_a": g_kv_a.reshape(1, -1),
        "w_uq": _pad_heads(w_uq.reshape(Q_LORA_RANK, N_HEADS, QK_DIM), QK_DIM).astype(BF16),
        "g_q": _pad_gain(g_q),
        "w_uk": _pad_heads(w_uk, QK_NOPE_DIM).astype(BF16),
    }


def _split_mod(m, n_prompt):
    m = m.reshape(m.shape[0], 6, D_MODEL)
    mp = [m[:n_prompt, j].reshape(n_prompt, 1, D_MODEL) for j in range(6)]
    ms = [m[n_prompt:, j].reshape(1, -1, D_MODEL) for j in range(6)]
    return mp, ms


def kernel(x_prompt, x_sample, c_prompt, c_sample, cache_kv_latent, cache_k_rope, page_table, norm1_g, norm2_g, w_mod, b_mod, mla_w_in, mla_g_q_a, mla_g_kv_a, mla_w_uq, mla_g_q, mla_w_uk, mla_w_uv, mla_g_k, mla_w_o, sg_w_in, sg_b_in, sg_g_v, sg_w_s, sg_b_s, sg_w_out, peer_w_q, peer_sub_keys, peer_u, peer_v):
    yp = x_prompt.reshape(BATCH * SEQ, D_MODEL)
    ys = x_sample.reshape(DEC_BATCH, D_MODEL)
    mods = _modulation(jnp.concatenate([c_prompt, c_sample], axis=0), w_mod, b_mod)

    tm_p, tps_p = 512, SEQ // 512
    tm_s, tps_s = DEC_BATCH, 1

    outs = {}
    for layer in range(DEPTH):
        mp, ms = _split_mod(mods[layer], BATCH)
        g1 = norm1_g[layer].reshape(1, D_MODEL)
        g2 = norm2_g[layer].reshape(1, D_MODEL)
        if layer % 2 == 0:
            a = layer // 2
            w = _mla_weights(mla_w_in[a], mla_g_q_a[a], mla_g_kv_a[a], mla_w_uq[a], mla_g_q[a], mla_w_uk[a])
            gk = _pad_gain(mla_g_k[a])
            w["g_k"] = gk
            wuv = jnp.transpose(mla_w_uv[a], (1, 0, 2)).astype(BF16)
            wo = mla_w_o[a].astype(BF16)
            cos_p, sa_p, sb_p, _, _ = _rope_tables(jnp.arange(SEQ))
            cos_s, sa_s, sb_s, _, _ = _rope_tables(jnp.full((1,), PAST_LEN))
            _, _, _, cos_c, sin_c = _rope_tables(jnp.arange(PAST_LEN))

            q, k, ckv_p, krp_p = _mla_pre(yp, mp[0], mp[1], g1, w, (cos_p, sa_p, sb_p), tm_p, tps_p, True, BF16)
            o_lat = _attn_prompt(q, k, ckv_p)
            yp = _mla_post(o_lat, wuv, wo, yp, mp[2], tm_p, tps_p)

            q_s, k_s, ckv_s, krp_s = _mla_pre(ys, ms[0], ms[1], g1, w, (cos_s, sa_s, sb_s), tm_s, tps_s, False, F32)
            wukt = jnp.pad(jnp.transpose(mla_w_uk[a], (1, 2, 0)),
                           ((0, 0), (0, HEAD_PAD - QK_NOPE_DIM), (0, 0))).astype(BF16)
            gk_nope = gk * (jnp.arange(HEAD_PAD) < QK_NOPE_DIM)[None, :]
            qa, qr = (jnp.transpose(x, (1, 0, 2)) for x in _absorb(q_s, gk_nope, wukt))
            wukt_flat = jnp.transpose(mla_w_uk[a], (1, 2, 0)).reshape(N_HEADS * QK_NOPE_DIM, KV_LORA_RANK).astype(BF16)
            gkr = mla_g_k[a][QK_NOPE_DIM:].reshape(QK_ROPE_DIM, 1)
            o_lat_s = _attn_sample(
                page_table, cache_kv_latent[a], jnp.swapaxes(cache_k_rope[a], 1, 2), wukt_flat, qa, qr,
                q_s.reshape(DEC_BATCH, N_HEADS, HEAD_PAD), k_s.reshape(DEC_BATCH, N_HEADS, HEAD_PAD),
                ckv_s.reshape(DEC_BATCH, 1, KV_LORA_RANK), gkr, cos_c.T, sin_c.T)
            ys = _mla_post(o_lat_s.reshape(DEC_BATCH, N_HEADS * KV_LORA_RANK).astype(BF16), wuv, wo, ys,
                           ms[2], tm_s, tps_s)

            outs.setdefault("kv_p", []).append(ckv_p.reshape(BATCH, SEQ, KV_LORA_RANK))
            outs.setdefault("kr_p", []).append(krp_p[:, QK_NOPE_DIM:QK_DIM].reshape(BATCH, SEQ, QK_ROPE_DIM))
            outs.setdefault("kv_s", []).append(ckv_s.reshape(DEC_BATCH, 1, KV_LORA_RANK))
            outs.setdefault("kr_s", []).append(krp_s[:, QK_NOPE_DIM:QK_DIM].reshape(DEC_BATCH, 1, QK_ROPE_DIM))
        else:
            bidx = layer // 2
            w = {"w_in": sg_w_in[bidx].astype(BF16), "b_in": sg_b_in[bidx].reshape(1, -1),
                 "g_v": sg_g_v[bidx].reshape(1, -1), "w_out": sg_w_out[bidx].astype(BF16)}
            sp_a = sg_w_s[bidx]
            sp_b = sg_b_s[bidx].reshape(SG_GROUPS, CHUNK, 1)
            yp, vp = _sgu(yp, mp[0], mp[1], mp[2], g1, w, sp_a, sp_b, 256, SEQ // 256, True)
            coef = jnp.repeat(sg_w_s[bidx][:, 0, 0], CHUNK).reshape(1, SG_HALF)
            bias = jnp.repeat(sg_b_s[bidx][:, 0], CHUNK).reshape(1, SG_HALF)
            ys, vs = _sgu(ys, ms[0], ms[1], ms[2], g1, w, coef, bias, tm_s, tps_s, False)
            outs.setdefault("v_p", []).append(vp)
            outs.setdefault("v_s", []).append(vs.reshape(DEC_BATCH, 1, SG_HALF))

        wq = peer_w_q[layer].astype(BF16)
        sk = peer_sub_keys[layer].astype(BF16)
        u_b = peer_u[layer].astype(BF16)
        vt_b = peer_v[layer].T.astype(BF16)
        yp = _peer(yp, mp[3], mp[4], mp[5], g2, wq, sk, u_b, vt_b, tm_p, tps_p)
        ys = _peer(ys, ms[3], ms[4], ms[5], g2, wq, sk, u_b, vt_b, tm_s, tps_s)

    return (yp.reshape(BATCH, SEQ, D_MODEL), ys.reshape(DEC_BATCH, 1, D_MODEL),
            jnp.stack(outs["kv_p"]), jnp.stack(outs["kr_p"]), jnp.stack(outs["kv_s"]), jnp.stack(outs["kr_s"]),
            jnp.stack(outs["v_p"]), jnp.stack(outs["v_s"]))
```

```python
import functools

import jax
import jax.numpy as jnp
from jax import lax
from jax.experimental import pallas as pl
from jax.experimental.pallas import tpu as pltpu

F32 = jnp.float32
BF16 = jnp.bfloat16

D_MODEL = 1024
BATCH = 8
SEQ = 2048
DEPTH = 2
DEC_BATCH = 128
PAST_LEN = 16384
PAGE_SIZE = 128
N_PAGES = PAST_LEN // PAGE_SIZE

N_HEADS = 8
Q_LORA_RANK = 256
KV_LORA_RANK = 128
QK_NOPE_DIM = 64
QK_ROPE_DIM = 32
QK_DIM = QK_NOPE_DIM + QK_ROPE_DIM
V_HEAD_DIM = 128
ROPE_THETA = 10000.0
SM_SCALE = QK_DIM ** -0.5
HEAD_PAD = 128
ROPE_HALF = QK_ROPE_DIM // 2

CHUNK = 128
SG_HALF = 1536
SG_GROUPS = 12

PEER_HEADS = 8
N_KEYS = 128
N_EXPERTS = N_KEYS * N_KEYS
PEER_TOPK = 16
RMS_EPS = 1e-6

NEG_BIG = -1e30
VMEM_LIMIT = 56 * 1024 * 1024


def _cparams(*sem):
    return pltpu.CompilerParams(dimension_semantics=sem, vmem_limit_bytes=VMEM_LIMIT)


def _rms(x):
    return x * lax.rsqrt(jnp.mean(x * x, axis=-1, keepdims=True) + RMS_EPS)


def _gelu(x):
    cdf = 0.5 * (1.0 + jnp.tanh(0.7978845608028654 * (x + 0.044715 * (x * x * x))))
    return x * cdf


def _gelu_twice(x):
    c = 0.7978845608028654
    return x * (1.0 + jnp.tanh(x * (c + (0.044715 * c) * (x * x))))


def _dot(a, b):
    return jnp.dot(a, b, preferred_element_type=F32)


def _dot_nt(a, b):
    return lax.dot_general(a, b, (((1,), (1,)), ((), ())), preferred_element_type=F32)


def _full(shape):
    n = len(shape)
    return pl.BlockSpec(shape, lambda *_: (0,) * n)


def _mod_spec(arr, tiles_per_seq):
    return pl.BlockSpec((None,) + arr.shape[1:], lambda t, *_: (t // tiles_per_seq, 0, 0))


def _mod_kernel(c_ref, w_ref, b_ref, o_ref):
    c = c_ref[...]
    a = (c * jax.nn.sigmoid(c)).astype(BF16)
    o_ref[...] = _dot(a, w_ref[...].astype(BF16)) + b_ref[...]


def _modulation(c_all, w_mod, b_mod):
    r = c_all.shape[0]
    tn = 1536
    return pl.pallas_call(
        _mod_kernel,
        grid=(DEPTH, 6 * D_MODEL // tn),
        in_specs=[
            pl.BlockSpec((r, D_MODEL), lambda l, n: (0, 0)),
            pl.BlockSpec((None, D_MODEL, tn), lambda l, n: (l, 0, n)),
            pl.BlockSpec((None, 1, tn), lambda l, n: (l, 0, n)),
        ],
        out_specs=pl.BlockSpec((None, r, tn), lambda l, n: (l, 0, n)),
        out_shape=jax.ShapeDtypeStruct((DEPTH, r, 6 * D_MODEL), F32),
        compiler_params=_cparams("parallel", "parallel"),
        name="modulation",
    )(c_all, w_mod, b_mod.reshape(DEPTH, 1, 6 * D_MODEL))


def _rope(x, cos, sa, sb):
    return x * cos + pltpu.roll(x, HEAD_PAD - ROPE_HALF, 1) * sa + pltpu.roll(x, ROPE_HALF, 1) * sb


def _mla_pre_kernel(y_ref, sh_ref, sc_ref, g1_ref, win_ref, gqa_ref, gkva_ref, wuq_ref, gq_ref,
                    wuk_ref, gk_ref, cos_ref, sa_ref, sb_ref, q_ref, k_ref, ckv_ref, kr_ref):
    h = _rms(y_ref[...]) * g1_ref[...]
    h = h * (1.0 + sc_ref[...]) + sh_ref[...]
    z = _dot(h.astype(BF16), win_ref[...])
    cq = _rms(z[:, :Q_LORA_RANK]) * gqa_ref[...]
    ckv = _rms(z[:, Q_LORA_RANK:Q_LORA_RANK + KV_LORA_RANK]) * gkva_ref[...]
    krp = z[:, Q_LORA_RANK + KV_LORA_RANK:]
    ckv_ref[...] = ckv
    kr_ref[...] = krp
    q = _dot(cq.astype(BF16), wuq_ref[...])
    kn = _dot(ckv.astype(BF16), wuk_ref[...])
    cos, sa, sb = cos_ref[...], sa_ref[...], sb_ref[...]
    gq, gk = gq_ref[...], gk_ref[...]
    for hd in range(N_HEADS):
        sl = slice(hd * HEAD_PAD, (hd + 1) * HEAD_PAD)
        qh = q[:, sl]
        qh = qh * lax.rsqrt(jnp.sum(qh * qh, axis=-1, keepdims=True) / QK_DIM + RMS_EPS) * gq
        q_ref[:, sl] = _rope(qh, cos, sa, sb).astype(q_ref.dtype)
        kh = kn[:, sl] + krp
        kh = kh * lax.rsqrt(jnp.sum(kh * kh, axis=-1, keepdims=True) / QK_DIM + RMS_EPS) * gk
        k_ref[:, sl] = _rope(kh, cos, sa, sb).astype(k_ref.dtype)


def _mla_pre(y, sh, sc, g1, w, rope_tabs, tm, tiles_per_seq, rope_tiled, qk_dtype):
    t = y.shape[0]
    cos, sa, sb = rope_tabs
    if rope_tiled:
        rspec = pl.BlockSpec((tm, HEAD_PAD), lambda i: (i % tiles_per_seq, 0))
    else:
        rspec = _full((1, HEAD_PAD))
    wide = N_HEADS * HEAD_PAD
    tok = lambda n: pl.BlockSpec((tm, n), lambda i: (i, 0))
    return pl.pallas_call(
        _mla_pre_kernel,
        grid=(t // tm,),
        in_specs=[
            tok(D_MODEL), _mod_spec(sh, tiles_per_seq), _mod_spec(sc, tiles_per_seq),
            _full((1, D_MODEL)), _full(w["w_in"].shape), _full((1, Q_LORA_RANK)),
            _full((1, KV_LORA_RANK)), _full(w["w_uq"].shape), _full((1, HEAD_PAD)),
            _full(w["w_uk"].shape), _full((1, HEAD_PAD)), rspec, rspec, rspec,
        ],
        out_specs=[tok(wide), tok(wide), tok(KV_LORA_RANK), tok(HEAD_PAD)],
        out_shape=[
            jax.ShapeDtypeStruct((t, wide), qk_dtype),
            jax.ShapeDtypeStruct((t, wide), qk_dtype),
            jax.ShapeDtypeStruct((t, KV_LORA_RANK), F32),
            jax.ShapeDtypeStruct((t, HEAD_PAD), F32),
        ],
        compiler_params=_cparams("parallel"),
        name="mla_pre",
    )(y, sh, sc, g1, w["w_in"], w["g_q_a"], w["g_kv_a"], w["w_uq"], w["g_q"], w["w_uk"], w["g_k"],
      cos, sa, sb)


ATT_TQ = 256
ATT_TK = 256


def _attn_prompt_kernel(q_ref, k_ref, ckv_ref, o_ref, m_ref, acc_ref):
    i = pl.program_id(1)
    m_ref[...] = jnp.full_like(m_ref, NEG_BIG)
    acc_ref[...] = jnp.zeros_like(acc_ref)
    ones = jnp.ones((ATT_TK, KV_LORA_RANK), BF16)

    def block(j, diagonal):
        off = pl.multiple_of(j * ATT_TK, ATT_TK)
        cb = jnp.concatenate([ckv_ref[pl.ds(off, ATT_TK), :].astype(BF16), ones], axis=1)
        for hd in range(N_HEADS):
            sl = slice(hd * HEAD_PAD, (hd + 1) * HEAD_PAD)
            s = _dot_nt(q_ref[:, sl], k_ref[pl.ds(off, ATT_TK), sl]) * SM_SCALE
            if diagonal:
                row = lax.broadcasted_iota(jnp.int32, (ATT_TQ, ATT_TK), 0)
                col = lax.broadcasted_iota(jnp.int32, (ATT_TQ, ATT_TK), 1)
                s = jnp.where(col <= row, s, NEG_BIG)
            m_old = m_ref[hd]
            s_max = jnp.max(jnp.maximum(s[:, :128], s[:, 128:]), axis=-1, keepdims=True)
            m_new = jnp.maximum(m_old, jnp.broadcast_to(s_max, m_old.shape))
            p = jnp.concatenate([jnp.exp(s[:, :128] - m_new), jnp.exp(s[:, 128:] - m_new)], axis=1)
            alpha = jnp.exp(m_old - m_new)
            pv = _dot(p.astype(BF16), cb)
            acc_ref[hd] = jnp.concatenate([alpha, alpha], axis=1) * acc_ref[hd] + pv
            m_ref[hd] = m_new

    def body(j, carry):
        block(j, False)
        return carry

    lax.fori_loop(0, i, body, 0)
    block(i, True)
    for hd in range(N_HEADS):
        acc = acc_ref[hd]
        o_ref[:, hd * HEAD_PAD:(hd + 1) * HEAD_PAD] = (
            acc[:, :KV_LORA_RANK] / acc[:, KV_LORA_RANK:]).astype(o_ref.dtype)


def _attn_prompt(q, k, ckv):
    nq = SEQ // ATT_TQ
    wide = N_HEADS * HEAD_PAD
    return pl.pallas_call(
        _attn_prompt_kernel,
        grid=(BATCH, nq),
        in_specs=[
            pl.BlockSpec((ATT_TQ, wide), lambda b, i: (b * nq + i, 0)),
            pl.BlockSpec((SEQ, wide), lambda b, i: (b, 0)),
            pl.BlockSpec((SEQ, KV_LORA_RANK), lambda b, i: (b, 0)),
        ],
        out_specs=pl.BlockSpec((ATT_TQ, wide), lambda b, i: (b * nq + i, 0)),
        out_shape=jax.ShapeDtypeStruct((BATCH * SEQ, wide), BF16),
        scratch_shapes=[pltpu.VMEM((N_HEADS, ATT_TQ, 128), F32),
                        pltpu.VMEM((N_HEADS, ATT_TQ, 2 * KV_LORA_RANK), F32)],
        compiler_params=_cparams("parallel", "parallel"),
        name="attn_prompt",
    )(q, k, ckv)


SA_PAGES = 32
SA_POS = SA_PAGES * PAGE_SIZE
SA_STEPS = N_PAGES // SA_PAGES


def _absorb_kernel(q_ref, gk_ref, wukt_ref, qa_ref, qr_ref):
    gk = gk_ref[...]
    for hd in range(N_HEADS):
        qh = q_ref[:, hd * HEAD_PAD:(hd + 1) * HEAD_PAD]
        qa_ref[hd] = _dot((qh * gk).astype(BF16), wukt_ref[hd])
        qr_ref[hd] = pltpu.roll(qh, HEAD_PAD - QK_NOPE_DIM, 1)


def _absorb(q_s, gk_nope, wukt):
    b = q_s.shape[0]
    shp = jax.ShapeDtypeStruct((N_HEADS, b, HEAD_PAD), F32)
    return pl.pallas_call(
        _absorb_kernel,
        in_specs=[_full(q_s.shape), _full(gk_nope.shape), _full(wukt.shape)],
        out_specs=[_full(shp.shape), _full(shp.shape)],
        out_shape=[shp, shp],
        grid=(1,),
        compiler_params=_cparams("arbitrary"),
        name="absorb_q",
    )(q_s, gk_nope, wukt)


def _attn_sample_kernel(pt_ref, kv_hbm, kr_hbm, wukt_ref, qa_ref, qr_ref, qs_ref, knew_ref, cnew_ref,
                        gkr_ref, cos_ref, sin_ref, o_ref, kv_buf, kr_buf, sems, m_ref, l_ref, acc_ref, ssq_ref):
    s, c = pl.program_id(0), pl.program_id(1)
    g = s * SA_STEPS + c
    last = pl.num_programs(0) * SA_STEPS - 1

    def page_copies(step, slot):
        seq, chunk = step // SA_STEPS, step % SA_STEPS
        for p in range(SA_PAGES):
            page = pt_ref[seq, chunk * SA_PAGES + p]
            yield pltpu.make_async_copy(kv_hbm.at[page], kv_buf.at[slot, p], sems.at[slot, 0])
            yield pltpu.make_async_copy(kr_hbm.at[page], kr_buf.at[slot, :, pl.ds(p * PAGE_SIZE, PAGE_SIZE)],
                                        sems.at[slot, 1])

    @pl.when(g == 0)
    def _():
        for cp in page_copies(g, 0):
            cp.start()

    @pl.when(g < last)
    def _():
        for cp in page_copies(g + 1, (g + 1) % 2):
            cp.start()

    slot = g % 2
    for cp in page_copies(g, slot):
        cp.wait()

    @pl.when(c == 0)
    def _():
        m_ref[...] = jnp.full_like(m_ref, NEG_BIG)
        l_ref[...] = jnp.zeros_like(l_ref)
        acc_ref[...] = jnp.zeros_like(acc_ref)

    cc = kv_buf[slot].reshape(SA_POS, KV_LORA_RANK).astype(BF16)
    qa16 = jnp.concatenate([qa_ref[...], jnp.zeros((8, KV_LORA_RANK), F32)], axis=0).astype(BF16)
    lhs = jnp.concatenate([wukt_ref[...], qa16], axis=0)
    res = _dot_nt(lhs, cc)
    for hd in range(N_HEADS):
        x = res[hd * QK_NOPE_DIM:(hd + 1) * QK_NOPE_DIM]
        ssq_ref[hd:hd + 1, :] = jnp.sum(x * x, axis=0, keepdims=True)
    sn = res[N_HEADS * QK_NOPE_DIM:N_HEADS * QK_NOPE_DIM + N_HEADS]

    krt = kr_buf[slot]
    kr2 = jnp.sum(krt * krt, axis=0, keepdims=True)
    krg = krt * gkr_ref[...]
    x1, x2 = krg[:ROPE_HALF], krg[ROPE_HALF:]
    cos, sin = cos_ref[...], sin_ref[...]
    rk = jnp.concatenate([x1 * cos - x2 * sin, x2 * cos + x1 * sin], axis=0).astype(BF16)
    sr = _dot(qr_ref[...][:, :QK_ROPE_DIM].astype(BF16), rk)

    rinv = lax.rsqrt((ssq_ref[...] + kr2) / QK_DIM + RMS_EPS)
    s = (sn + sr) * rinv * SM_SCALE
    m_old = m_ref[...]
    m_new = jnp.maximum(m_old, jnp.max(s, axis=-1, keepdims=True))
    p = jnp.exp(s - m_new)
    alpha = jnp.exp(m_old - m_new)
    l_ref[...] = alpha * l_ref[...] + jnp.sum(p, axis=-1, keepdims=True)
    acc_ref[...] = alpha * acc_ref[...] + _dot(p.astype(BF16), cc)
    m_ref[...] = m_new

    @pl.when(c == SA_STEPS - 1)
    def _():
        s_new = jnp.sum(qs_ref[...] * knew_ref[...], axis=-1, keepdims=True) * SM_SCALE
        m_o = m_ref[...]
        m_n = jnp.maximum(m_o, s_new)
        p_new = jnp.exp(s_new - m_n)
        a = jnp.exp(m_o - m_n)
        l = a * l_ref[...] + p_new
        acc = a * acc_ref[...] + p_new * cnew_ref[...]
        o_ref[...] = acc / l


def _attn_sample(page_table, cache_kv, cache_kr, wukt_flat, qa, qr, qs, knew, cnew, gkr, cos_t, sin_t):
    b = qa.shape[0]

    per_seq = pl.BlockSpec((None, N_HEADS, HEAD_PAD), lambda s, c, pt: (s, 0, 0))
    in_specs = [
        pl.BlockSpec(memory_space=pl.ANY), pl.BlockSpec(memory_space=pl.ANY),
        pl.BlockSpec(wukt_flat.shape, lambda s, c, pt: (0, 0)),
        per_seq, per_seq, per_seq, per_seq,
        pl.BlockSpec((None, 1, KV_LORA_RANK), lambda s, c, pt: (s, 0, 0)),
        pl.BlockSpec(gkr.shape, lambda s, c, pt: (0, 0)),
        pl.BlockSpec((ROPE_HALF, SA_POS), lambda s, c, pt: (0, c)),
        pl.BlockSpec((ROPE_HALF, SA_POS), lambda s, c, pt: (0, c)),
    ]
    grid_spec = pltpu.PrefetchScalarGridSpec(
        num_scalar_prefetch=1,
        grid=(b, SA_STEPS),
        in_specs=in_specs,
        out_specs=pl.BlockSpec((None, N_HEADS, KV_LORA_RANK), lambda s, c, pt: (s, 0, 0)),
        scratch_shapes=[
            pltpu.VMEM((2, SA_PAGES, PAGE_SIZE, KV_LORA_RANK), F32),
            pltpu.VMEM((2, QK_ROPE_DIM, SA_POS), F32),
            pltpu.SemaphoreType.DMA((2, 2)),
            pltpu.VMEM((N_HEADS, 1), F32), pltpu.VMEM((N_HEADS, 1), F32),
            pltpu.VMEM((N_HEADS, KV_LORA_RANK), F32), pltpu.VMEM((N_HEADS, SA_POS), F32),
        ],
    )
    return pl.pallas_call(
        _attn_sample_kernel,
        grid_spec=grid_spec,
        out_shape=jax.ShapeDtypeStruct((b, N_HEADS, KV_LORA_RANK), F32),
        compiler_params=_cparams("arbitrary", "arbitrary"),
        name="attn_sample",
    )(page_table, cache_kv, cache_kr, wukt_flat, qa, qr, qs, knew, cnew, gkr, cos_t, sin_t)


def _mla_post_kernel(ol_ref, wuv_ref, wo_ref, y_ref, gate_ref, o_ref):
    parts = [_dot(ol_ref[:, hd * KV_LORA_RANK:(hd + 1) * KV_LORA_RANK], wuv_ref[hd])
             for hd in range(N_HEADS)]
    o = jnp.concatenate(parts, axis=-1).astype(BF16)
    o_ref[...] = y_ref[...] + gate_ref[...] * _dot(o, wo_ref[...])


def _mla_post(o_lat, wuv, wo, y, gate, tm, tiles_per_seq):
    t = y.shape[0]
    tok = lambda n: pl.BlockSpec((tm, n), lambda i: (i, 0))
    return pl.pallas_call(
        _mla_post_kernel,
        grid=(t // tm,),
        in_specs=[tok(N_HEADS * KV_LORA_RANK), _full(wuv.shape), _full(wo.shape), tok(D_MODEL),
                  _mod_spec(gate, tiles_per_seq)],
        out_specs=tok(D_MODEL),
        out_shape=jax.ShapeDtypeStruct((t, D_MODEL), F32),
        compiler_params=_cparams("parallel"),
        name="mla_post",
    )(o_lat, wuv, wo, y, gate)


def _sgu_kernel(y_ref, sh_ref, sc_ref, gate_ref, g1_ref, win_ref, bin_ref, gv_ref, sp_a_ref, sp_b_ref,
                wout_ref, o_ref, v_ref, *, tm, spatial):
    y = y_ref[...]
    h = _rms(y) * g1_ref[...]
    h = h * (1.0 + sc_ref[...]) + sh_ref[...]
    z = _gelu(_dot(h.astype(BF16), win_ref[...]) + bin_ref[...])
    u = z[:, :SG_HALF]
    v = _rms(z[:, SG_HALF:]) * gv_ref[...]
    if spatial:
        r = lax.broadcasted_iota(jnp.int32, (CHUNK, CHUNK), 0)
        cidx = lax.broadcasted_iota(jnp.int32, (CHUNK, CHUNK), 1)
        vb = v.astype(BF16)
        cols = []
        for g in range(SG_GROUPS):
            wc = jnp.where(cidx <= r, sp_a_ref[g], 0.0).astype(BF16)
            rows = [_dot(wc, vb[c * CHUNK:(c + 1) * CHUNK, g * CHUNK:(g + 1) * CHUNK]) + sp_b_ref[g]
                    for c in range(tm // CHUNK)]
            cols.append(jnp.concatenate(rows, axis=0))
        s = jnp.concatenate(cols, axis=-1)
        v_ref[...] = v[tm - CHUNK:, :]
    else:
        s = v * sp_a_ref[...] + sp_b_ref[...]
        v_ref[...] = v
    o = _dot((u * s).astype(BF16), wout_ref[...])
    o_ref[...] = y + gate_ref[...] * o


def _sgu(y, sh, sc, gate, g1, w, sp_a, sp_b, tm, tiles_per_seq, spatial):
    t = y.shape[0]
    tok = lambda n: pl.BlockSpec((tm, n), lambda i: (i, 0))
    if spatial:
        n_seq = t // (tm * tiles_per_seq)
        v_spec = pl.BlockSpec((None, CHUNK, SG_HALF), lambda i: (i // tiles_per_seq, 0, 0))
        v_shape = jax.ShapeDtypeStruct((n_seq, CHUNK, SG_HALF), F32)
    else:
        v_spec = tok(SG_HALF)
        v_shape = jax.ShapeDtypeStruct((t, SG_HALF), F32)
    return pl.pallas_call(
        functools.partial(_sgu_kernel, tm=tm, spatial=spatial),
        grid=(t // tm,),
        in_specs=[tok(D_MODEL), _mod_spec(sh, tiles_per_seq), _mod_spec(sc, tiles_per_seq),
                  _mod_spec(gate, tiles_per_seq), _full((1, D_MODEL)), _full(w["w_in"].shape),
                  _full((1, 2 * SG_HALF)), _full((1, SG_HALF)), _full(sp_a.shape), _full(sp_b.shape),
                  _full(w["w_out"].shape)],
        out_specs=[tok(D_MODEL), v_spec],
        out_shape=[jax.ShapeDtypeStruct((t, D_MODEL), F32), v_shape],
        compiler_params=_cparams("arbitrary"),
        name="sgu",
    )(y, sh, sc, gate, g1, w["w_in"], w["b_in"], w["g_v"], sp_a, sp_b, w["w_out"])


N_TOP = PEER_TOPK + 1
TOP_ROWS = 24
N_CAND = TOP_ROWS + 7 * 8 + (TOP_ROWS - 8)


def _top_values(x, n, emit):
    prev = None
    for r in range(n):
        cand = x if prev is None else jnp.where(x < prev, x, NEG_BIG)
        prev = jnp.max(cand, axis=0, keepdims=True)
        emit(r, prev)


def _pair_words(x):
    bits = pltpu.bitcast(x.astype(BF16).astype(F32), jnp.uint32)
    return bits | (bits >> 16)


def _peer_pre_kernel(y_ref, sh_ref, sc_ref, g2_ref, wq_ref, sk_ref, ht_ref, rank_ref, eb_ref, cnt_ref,
                     cc_ref, s_scr, t_scr, cand_scr, tv_scr):
    h = _rms(y_ref[...]) * g2_ref[...]
    h = h * (1.0 + sc_ref[...]) + sh_ref[...]
    ht_ref[...] = h.T.astype(BF16)
    q = _dot(h.astype(BF16), wq_ref[...])
    for hd in range(PEER_HEADS):
        for side in range(2):
            o = (hd * 2 + side) * N_KEYS
            s_scr[side, hd] = _dot_nt(sk_ref[side], q[:, o:o + N_KEYS].astype(BF16))
    t_scr[...] = jnp.full_like(t_scr, NEG_BIG)

    def tops(idx, _):
        side, hd = idx // PEER_HEADS, idx % PEER_HEADS

        def emit(r, row):
            t_scr[side, hd, r:r + 1, :] = row

        _top_values(s_scr[side, hd], N_TOP, emit)
        return 0

    lax.fori_loop(0, 2 * PEER_HEADS, tops, 0)

    def finish(hd, _):
        t1, t2 = t_scr[0, hd], t_scr[1, hd]
        cand_scr[0:TOP_ROWS] = t1[0:1] + t2
        for a in range(1, 8):
            cand_scr[TOP_ROWS + (a - 1) * 8:TOP_ROWS + a * 8] = t1[a:a + 1] + t2[0:8]
        cand_scr[TOP_ROWS + 56:] = t1[8:] + t2[0:1]
        cand = cand_scr[...]

        def emit(r, row):
            tv_scr[r:r + 1, :] = row

        _top_values(cand, N_TOP, emit)
        tau = 0.5 * (tv_scr[PEER_TOPK - 1:PEER_TOPK, :] + tv_scr[PEER_TOPK:PEER_TOPK + 1, :])
        top = t1[:1] + t2[:1]
        z = jnp.sum(jnp.where(cand >= tau, jnp.exp(cand - top), 0.0), axis=0, keepdims=True)
        s1, s2 = s_scr[0, hd], s_scr[1, hd]
        eb = jnp.exp(s2 - t2[:1])
        th = tau - s1
        cc = jnp.exp(s1 - t1[:1]) * (0.5 / z)
        rank = jnp.zeros_like(s2)
        cnt = jnp.zeros_like(s1)
        for b in range(PEER_TOPK):
            tb = t2[b:b + 1]
            rank = jnp.where(tb > s2, b + 1.0, rank)
            cnt = jnp.where(tb >= th, b + 1.0, cnt)
        cntw = _pair_words(cnt)
        ccw = _pair_words(cc)
        for col in range(s1.shape[1] // 128):
            cs = slice(col * 128, (col + 1) * 128)
            rank_ref[hd, col] = pltpu.bitcast(rank[:, cs].astype(BF16), jnp.uint32)
            eb_ref[hd, col] = pltpu.bitcast(eb[:, cs].astype(BF16), jnp.uint32)
            cnt_ref[hd, col] = cntw[:, cs]
            cc_ref[hd, col] = ccw[:, cs]
        return 0

    lax.fori_loop(0, PEER_HEADS, finish, 0)


def _peer_pre(y, sh, sc, g2, wq, sk, tm, tiles_per_seq):
    t = y.shape[0]
    hk = pl.BlockSpec((PEER_HEADS, tm // 128, N_KEYS, 128), lambda i: (0, i, 0, 0))
    hp = pl.BlockSpec((PEER_HEADS, tm // 128, N_KEYS // 2, 128), lambda i: (0, i, 0, 0))
    packed = jax.ShapeDtypeStruct((PEER_HEADS, t // 128, N_KEYS // 2, 128), jnp.uint32)
    words = jax.ShapeDtypeStruct((PEER_HEADS, t // 128, N_KEYS, 128), jnp.uint32)
    return pl.pallas_call(
        _peer_pre_kernel,
        grid=(t // tm,),
        in_specs=[pl.BlockSpec((tm, D_MODEL), lambda i: (i, 0)), _mod_spec(sh, tiles_per_seq),
                  _mod_spec(sc, tiles_per_seq), _full((1, D_MODEL)), _full(wq.shape), _full(sk.shape)],
        out_specs=[pl.BlockSpec((D_MODEL, tm), lambda i: (0, i)), hp, hp, hk, hk],
        out_shape=[jax.ShapeDtypeStruct((D_MODEL, t), BF16), packed, packed, words, words],
        scratch_shapes=[pltpu.VMEM((2, PEER_HEADS, N_KEYS, tm), F32),
                        pltpu.VMEM((2, PEER_HEADS, TOP_ROWS, tm), F32),
                        pltpu.VMEM((N_CAND, tm), F32),
                        pltpu.VMEM((TOP_ROWS, tm), F32)],
        compiler_params=_cparams("parallel"),
        name="peer_pre",
    )(y, sh, sc, g2, wq, sk)


PEER_EB = 1024
PEER_ROWS = PEER_EB // N_KEYS
PEER_BLOCKS = N_EXPERTS // PEER_EB


BF16_ROWS = 16


def _selection_weights(rank_ref, eb_ref, cnt_ref, cc_ref, col):
    def row_of(ref, hd, ii):
        word = jnp.broadcast_to(ref[hd, col, ii:ii + 1, :], (8, 128))
        tile = pltpu.bitcast(word, BF16)
        return jnp.concatenate([tile] * (N_KEYS // BF16_ROWS), axis=0)

    rows = []
    for ii in range(PEER_ROWS):
        w = None
        for hd in range(PEER_HEADS):
            eb = pltpu.bitcast(eb_ref[hd, col], BF16)
            rank = pltpu.bitcast(rank_ref[hd, col], BF16)
            t = jnp.where(rank < row_of(cnt_ref, hd, ii), eb, jnp.zeros_like(eb))
            t = t * row_of(cc_ref, hd, ii)
            w = t if w is None else w + t
        rows.append(w)
    return jnp.concatenate(rows, axis=0)


def _peer_dense_kernel(ht_ref, u_ref, vt_ref, rank_ref, eb_ref, cnt_ref, cc_ref, y_ref, gate_ref, o_ref,
                       acc_ref, *, tm):
    e = pl.program_id(1)

    @pl.when(e == 0)
    def _():
        acc_ref[...] = jnp.zeros_like(acc_ref)

    act = _gelu_twice(_dot(u_ref[...], ht_ref[...]))
    w = jnp.concatenate([_selection_weights(rank_ref, eb_ref, cnt_ref, cc_ref, col)
                         for col in range(tm // 128)], axis=1)
    acc_ref[...] += _dot(vt_ref[...], act.astype(BF16) * w)

    @pl.when(e == PEER_BLOCKS - 1)
    def _():
        o_ref[...] = y_ref[...] + gate_ref[...] * acc_ref[...].T


def _peer_dense(ht, u_b, vt_b, rank, eb, cnt, cc, y, gate, tm, tiles_per_seq):
    t = y.shape[0]
    cols = tm // 128
    hk = pl.BlockSpec((PEER_HEADS, cols, N_KEYS // 2, 128), lambda i, e: (0, i, 0, 0))
    hr = pl.BlockSpec((PEER_HEADS, cols, PEER_ROWS, 128), lambda i, e: (0, i, e, 0))
    return pl.pallas_call(
        functools.partial(_peer_dense_kernel, tm=tm),
        grid=(t // tm, PEER_BLOCKS),
        in_specs=[pl.BlockSpec((D_MODEL, tm), lambda i, e: (0, i)),
                  pl.BlockSpec((PEER_EB, D_MODEL), lambda i, e: (e, 0)),
                  pl.BlockSpec((D_MODEL, PEER_EB), lambda i, e: (0, e)),
                  hk, hk, hr, hr,
                  pl.BlockSpec((tm, D_MODEL), lambda i, e: (i, 0)),
                  pl.BlockSpec((None,) + gate.shape[1:], lambda i, e: (i // tiles_per_seq, 0, 0))],
        out_specs=pl.BlockSpec((tm, D_MODEL), lambda i, e: (i, 0)),
        out_shape=jax.ShapeDtypeStruct((t, D_MODEL), F32),
        scratch_shapes=[pltpu.VMEM((D_MODEL, tm), F32)],
        compiler_params=_cparams("parallel", "arbitrary"),
        name="peer_dense",
    )(ht, u_b, vt_b, rank, eb, cnt, cc, y, gate)


def _peer(y, sh, sc, gate, g2, wq, sk, u_b, vt_b, tm, tiles_per_seq):
    ht, rank, eb, cnt, cc = _peer_pre(y, sh, sc, g2, wq, sk, tm, tiles_per_seq)
    return _peer_dense(ht, u_b, vt_b, rank, eb, cnt, cc, y, gate, tm, tiles_per_seq)


def _pad_heads(w, used):
    w = jnp.pad(w, [(0, 0)] * (w.ndim - 1) + [(0, HEAD_PAD - used)])
    return w.reshape(w.shape[:-2] + (N_HEADS * HEAD_PAD,))


def _pad_gain(g):
    return jnp.pad(g, (0, HEAD_PAD - QK_DIM)).reshape(1, HEAD_PAD)


def _rope_tables(pos):
    inv_freq = ROPE_THETA ** (-jnp.arange(ROPE_HALF, dtype=F32) * 2.0 / QK_ROPE_DIM)
    ang = pos.astype(F32)[:, None] * inv_freq[None, :]
    cos, sin = jnp.cos(ang), jnp.sin(ang)
    t = pos.shape[0]
    one = jnp.ones((t, QK_NOPE_DIM), F32)
    z64 = jnp.zeros((t, QK_NOPE_DIM), F32)
    z16 = jnp.zeros((t, ROPE_HALF), F32)
    z32 = jnp.zeros((t, HEAD_PAD - QK_DIM), F32)
    cos_t = jnp.concatenate([one, cos, cos, z32], axis=1)
    sa = jnp.concatenate([z64, -sin, z16, z32], axis=1)
    sb = jnp.concatenate([z64, z16, sin, z32], axis=1)
    return cos_t, sa, sb, cos, sin


def _mla_weights(w_in, g_q_a, g_kv_a, w_uq, g_q, w_uk):
    lo = Q_LORA_RANK + KV_LORA_RANK
    w_in_p = jnp.concatenate(
        [w_in[:, :lo], jnp.zeros((D_MODEL, QK_NOPE_DIM), F32), w_in[:, lo:],
         jnp.zeros((D_MODEL, HEAD_PAD - QK_DIM), F32)], axis=1).astype(BF16)
    return {
        "w_in": w_in_p,
        "g_q_a": g_q_a.reshape(1, -1),
        "g_kv_a": g_kv_a.reshape(1, -1),
        "w_uq": _pad_heads(w_uq.reshape(Q_LORA_RANK, N_HEADS, QK_DIM), QK_DIM).astype(BF16),
        "g_q": _pad_gain(g_q),
        "w_uk": _pad_heads(w_uk, QK_NOPE_DIM).astype(BF16),
    }


def _split_mod(m, n_prompt):
    m = m.reshape(m.shape[0], 6, D_MODEL)
    mp = [m[:n_prompt, j].reshape(n_prompt, 1, D_MODEL) for j in range(6)]
    ms = [m[n_prompt:, j].reshape(1, -1, D_MODEL) for j in range(6)]
    return mp, ms


def kernel(x_prompt, x_sample, c_prompt, c_sample, cache_kv_latent, cache_k_rope, page_table, norm1_g, norm2_g, w_mod, b_mod, mla_w_in, mla_g_q_a, mla_g_kv_a, mla_w_uq, mla_g_q, mla_w_uk, mla_w_uv, mla_g_k, mla_w_o, sg_w_in, sg_b_in, sg_g_v, sg_w_s, sg_b_s, sg_w_out, peer_w_q, peer_sub_keys, peer_u, peer_v):
    yp = x_prompt.reshape(BATCH * SEQ, D_MODEL)
    ys = x_sample.reshape(DEC_BATCH, D_MODEL)
    mods = _modulation(jnp.concatenate([c_prompt, c_sample], axis=0), w_mod, b_mod)

    tm_p, tps_p = 512, SEQ // 512
    tm_s, tps_s = DEC_BATCH, 1

    outs = {}
    for layer in range(DEPTH):
        mp, ms = _split_mod(mods[layer], BATCH)
        g1 = norm1_g[layer].reshape(1, D_MODEL)
        g2 = norm2_g[layer].reshape(1, D_MODEL)
        if layer % 2 == 0:
            a = layer // 2
            w = _mla_weights(mla_w_in[a], mla_g_q_a[a], mla_g_kv_a[a], mla_w_uq[a], mla_g_q[a], mla_w_uk[a])
            gk = _pad_gain(mla_g_k[a])
            w["g_k"] = gk
            wuv = jnp.transpose(mla_w_uv[a], (1, 0, 2)).astype(BF16)
            wo = mla_w_o[a].astype(BF16)
            cos_p, sa_p, sb_p, _, _ = _rope_tables(jnp.arange(SEQ))
            cos_s, sa_s, sb_s, _, _ = _rope_tables(jnp.full((1,), PAST_LEN))
            _, _, _, cos_c, sin_c = _rope_tables(jnp.arange(PAST_LEN))

            q, k, ckv_p, krp_p = _mla_pre(yp, mp[0], mp[1], g1, w, (cos_p, sa_p, sb_p), tm_p, tps_p, True, BF16)
            o_lat = _attn_prompt(q, k, ckv_p)
            yp = _mla_post(o_lat, wuv, wo, yp, mp[2], tm_p, tps_p)

            q_s, k_s, ckv_s, krp_s = _mla_pre(ys, ms[0], ms[1], g1, w, (cos_s, sa_s, sb_s), tm_s, tps_s, False, F32)
            wukt = jnp.pad(jnp.transpose(mla_w_uk[a], (1, 2, 0)),
                           ((0, 0), (0, HEAD_PAD - QK_NOPE_DIM), (0, 0))).astype(BF16)
            gk_nope = gk * (jnp.arange(HEAD_PAD) < QK_NOPE_DIM)[None, :]
            qa, qr = (jnp.transpose(x, (1, 0, 2)) for x in _absorb(q_s, gk_nope, wukt))
            wukt_flat = jnp.transpose(mla_w_uk[a], (1, 2, 0)).reshape(N_HEADS * QK_NOPE_DIM, KV_LORA_RANK).astype(BF16)
            gkr = mla_g_k[a][QK_NOPE_DIM:].reshape(QK_ROPE_DIM, 1)
            o_lat_s = _attn_sample(
                page_table, cache_kv_latent[a], jnp.swapaxes(cache_k_rope[a], 1, 2), wukt_flat, qa, qr,
                q_s.reshape(DEC_BATCH, N_HEADS, HEAD_PAD), k_s.reshape(DEC_BATCH, N_HEADS, HEAD_PAD),
                ckv_s.reshape(DEC_BATCH, 1, KV_LORA_RANK), gkr, cos_c.T, sin_c.T)
            ys = _mla_post(o_lat_s.reshape(DEC_BATCH, N_HEADS * KV_LORA_RANK).astype(BF16), wuv, wo, ys,
                           ms[2], tm_s, tps_s)

            outs.setdefault("kv_p", []).append(ckv_p.reshape(BATCH, SEQ, KV_LORA_RANK))
            outs.setdefault("kr_p", []).append(krp_p[:, QK_NOPE_DIM:QK_DIM].reshape(BATCH, SEQ, QK_ROPE_DIM))
            outs.setdefault("kv_s", []).append(ckv_s.reshape(DEC_BATCH, 1, KV_LORA_RANK))
            outs.setdefault("kr_s", []).append(krp_s[:, QK_NOPE_DIM:QK_DIM].reshape(DEC_BATCH, 1, QK_ROPE_DIM))
        else:
            bidx = layer // 2
            w = {"w_in": sg_w_in[bidx].astype(BF16), "b_in": sg_b_in[bidx].reshape(1, -1),
                 "g_v": sg_g_v[bidx].reshape(1, -1), "w_out": sg_w_out[bidx].astype(BF16)}
            sp_a = sg_w_s[bidx]
            sp_b = sg_b_s[bidx].reshape(SG_GROUPS, CHUNK, 1)
            yp, vp = _sgu(yp, mp[0], mp[1], mp[2], g1, w, sp_a, sp_b, 256, SEQ // 256, True)
            coef = jnp.repeat(sg_w_s[bidx][:, 0, 0], CHUNK).reshape(1, SG_HALF)
            bias = jnp.repeat(sg_b_s[bidx][:, 0], CHUNK).reshape(1, SG_HALF)
            ys, vs = _sgu(ys, ms[0], ms[1], ms[2], g1, w, coef, bias, tm_s, tps_s, False)
            outs.setdefault("v_p", []).append(vp)
            outs.setdefault("v_s", []).append(vs.reshape(DEC_BATCH, 1, SG_HALF))

        wq = peer_w_q[layer].astype(BF16)
        sk = peer_sub_keys[layer].astype(BF16)
        u_b = peer_u[layer].astype(BF16)
        vt_b = peer_v[layer].T.astype(BF16)
        yp = _peer(yp, mp[3], mp[4], mp[5], g2, wq, sk, u_b, vt_b, tm_p, tps_p)
        ys = _peer(ys, ms[3], ms[4], ms[5], g2, wq, sk, u_b, vt_b, tm_s, tps_s)

    return (yp.reshape(BATCH, SEQ, D_MODEL), ys.reshape(DEC_BATCH, 1, D_MODEL),
            jnp.stack(outs["kv_p"]), jnp.stack(outs["kr_p"]), jnp.stack(outs["kv_s"]), jnp.stack(outs["kr_s"]),
            jnp.stack(outs["v_p"]), jnp.stack(outs["v_s"]))
```

```python
import functools

import jax
import jax.numpy as jnp
from jax import lax
from jax.experimental import pallas as pl
from jax.experimental.pallas import tpu as pltpu

F32 = jnp.float32
BF16 = jnp.bfloat16

D_MODEL = 1024
BATCH = 8
SEQ = 2048
DEPTH = 2
DEC_BATCH = 128
PAST_LEN = 16384
PAGE_SIZE = 128
N_PAGES = PAST_LEN // PAGE_SIZE

N_HEADS = 8
Q_LORA_RANK = 256
KV_LORA_RANK = 128
QK_NOPE_DIM = 64
QK_ROPE_DIM = 32
QK_DIM = QK_NOPE_DIM + QK_ROPE_DIM
V_HEAD_DIM = 128
ROPE_THETA = 10000.0
SM_SCALE = QK_DIM ** -0.5
HEAD_PAD = 128
ROPE_HALF = QK_ROPE_DIM // 2

CHUNK = 128
SG_HALF = 1536
SG_GROUPS = 12

PEER_HEADS = 8
N_KEYS = 128
N_EXPERTS = N_KEYS * N_KEYS
PEER_TOPK = 16
RMS_EPS = 1e-6

NEG_BIG = -1e30
VMEM_LIMIT = 56 * 1024 * 1024


def _cparams(*sem):
    return pltpu.CompilerParams(dimension_semantics=sem, vmem_limit_bytes=VMEM_LIMIT)


def _rms(x):
    return x * lax.rsqrt(jnp.mean(x * x, axis=-1, keepdims=True) + RMS_EPS)


def _gelu(x):
    cdf = 0.5 * (1.0 + jnp.tanh(0.7978845608028654 * (x + 0.044715 * (x * x * x))))
    return x * cdf


def _gelu_twice(x):
    c = 0.7978845608028654
    return x * (1.0 + jnp.tanh(x * (c + (0.044715 * c) * (x * x))))


def _dot(a, b):
    return jnp.dot(a, b, preferred_element_type=F32)


def _dot_nt(a, b):
    return lax.dot_general(a, b, (((1,), (1,)), ((), ())), preferred_element_type=F32)


def _full(shape):
    n = len(shape)
    return pl.BlockSpec(shape, lambda *_: (0,) * n)


def _mod_spec(arr, tiles_per_seq):
    return pl.BlockSpec((None,) + arr.shape[1:], lambda t, *_: (t // tiles_per_seq, 0, 0))


def _mod_kernel(c_ref, w_ref, b_ref, o_ref):
    c = c_ref[...]
    a = (c * jax.nn.sigmoid(c)).astype(BF16)
    o_ref[...] = _dot(a, w_ref[...].astype(BF16)) + b_ref[...]


def _modulation(c_all, w_mod, b_mod):
    r = c_all.shape[0]
    tn = 1536
    return pl.pallas_call(
        _mod_kernel,
        grid=(DEPTH, 6 * D_MODEL // tn),
        in_specs=[
            pl.BlockSpec((r, D_MODEL), lambda l, n: (0, 0)),
            pl.BlockSpec((None, D_MODEL, tn), lambda l, n: (l, 0, n)),
            pl.BlockSpec((None, 1, tn), lambda l, n: (l, 0, n)),
        ],
        out_specs=pl.BlockSpec((None, r, tn), lambda l, n: (l, 0, n)),
        out_shape=jax.ShapeDtypeStruct((DEPTH, r, 6 * D_MODEL), F32),
        compiler_params=_cparams("parallel", "parallel"),
        name="modulation",
    )(c_all, w_mod, b_mod.reshape(DEPTH, 1, 6 * D_MODEL))


def _rope(x, cos, sa, sb):
    return x * cos + pltpu.roll(x, HEAD_PAD - ROPE_HALF, 1) * sa + pltpu.roll(x, ROPE_HALF, 1) * sb


def _mla_pre_kernel(y_ref, sh_ref, sc_ref, g1_ref, win_ref, gqa_ref, gkva_ref, wuq_ref, gq_ref,
                    wuk_ref, gk_ref, cos_ref, sa_ref, sb_ref, q_ref, k_ref, ckv_ref, kr_ref):
    h = _rms(y_ref[...]) * g1_ref[...]
    h = h * (1.0 + sc_ref[...]) + sh_ref[...]
    z = _dot(h.astype(BF16), win_ref[...])
    cq = _rms(z[:, :Q_LORA_RANK]) * gqa_ref[...]
    ckv = _rms(z[:, Q_LORA_RANK:Q_LORA_RANK + KV_LORA_RANK]) * gkva_ref[...]
    krp = z[:, Q_LORA_RANK + KV_LORA_RANK:]
    ckv_ref[...] = ckv
    kr_ref[...] = krp
    q = _dot(cq.astype(BF16), wuq_ref[...])
    kn = _dot(ckv.astype(BF16), wuk_ref[...])
    cos, sa, sb = cos_ref[...], sa_ref[...], sb_ref[...]
    gq, gk = gq_ref[...], gk_ref[...]
    for hd in range(N_HEADS):
        sl = slice(hd * HEAD_PAD, (hd + 1) * HEAD_PAD)
        qh = q[:, sl]
        qh = qh * lax.rsqrt(jnp.sum(qh * qh, axis=-1, keepdims=True) / QK_DIM + RMS_EPS) * gq
        q_ref[:, sl] = _rope(qh, cos, sa, sb).astype(q_ref.dtype)
        kh = kn[:, sl] + krp
        kh = kh * lax.rsqrt(jnp.sum(kh * kh, axis=-1, keepdims=True) / QK_DIM + RMS_EPS) * gk
        k_ref[:, sl] = _rope(kh, cos, sa, sb).astype(k_ref.dtype)


def _mla_pre(y, sh, sc, g1, w, rope_tabs, tm, tiles_per_seq, rope_tiled, qk_dtype):
    t = y.shape[0]
    cos, sa, sb = rope_tabs
    if rope_tiled:
        rspec = pl.BlockSpec((tm, HEAD_PAD), lambda i: (i % tiles_per_seq, 0))
    else:
        rspec = _full((1, HEAD_PAD))
    wide = N_HEADS * HEAD_PAD
    tok = lambda n: pl.BlockSpec((tm, n), lambda i: (i, 0))
    return pl.pallas_call(
        _mla_pre_kernel,
        grid=(t // tm,),
        in_specs=[
            tok(D_MODEL), _mod_spec(sh, tiles_per_seq), _mod_spec(sc, tiles_per_seq),
            _full((1, D_MODEL)), _full(w["w_in"].shape), _full((1, Q_LORA_RANK)),
            _full((1, KV_LORA_RANK)), _full(w["w_uq"].shape), _full((1, HEAD_PAD)),
            _full(w["w_uk"].shape), _full((1, HEAD_PAD)), rspec, rspec, rspec,
        ],
        out_specs=[tok(wide), tok(wide), tok(KV_LORA_RANK), tok(HEAD_PAD)],
        out_shape=[
            jax.ShapeDtypeStruct((t, wide), qk_dtype),
            jax.ShapeDtypeStruct((t, wide), qk_dtype),
            jax.ShapeDtypeStruct((t, KV_LORA_RANK), F32),
            jax.ShapeDtypeStruct((t, HEAD_PAD), F32),
        ],
        compiler_params=_cparams("parallel"),
        name="mla_pre",
    )(y, sh, sc, g1, w["w_in"], w["g_q_a"], w["g_kv_a"], w["w_uq"], w["g_q"], w["w_uk"], w["g_k"],
      cos, sa, sb)


ATT_TQ = 256
ATT_TK = 256


def _attn_prompt_kernel(q_ref, k_ref, ckv_ref, o_ref, m_ref, acc_ref):
    i = pl.program_id(1)
    m_ref[...] = jnp.full_like(m_ref, NEG_BIG)
    acc_ref[...] = jnp.zeros_like(acc_ref)
    ones = jnp.ones((ATT_TK, KV_LORA_RANK), BF16)

    def block(j, diagonal):
        off = pl.multiple_of(j * ATT_TK, ATT_TK)
        cb = jnp.concatenate([ckv_ref[pl.ds(off, ATT_TK), :].astype(BF16), ones], axis=1)
        for hd in range(N_HEADS):
            sl = slice(hd * HEAD_PAD, (hd + 1) * HEAD_PAD)
            s = _dot_nt(q_ref[:, sl], k_ref[pl.ds(off, ATT_TK), sl]) * SM_SCALE
            if diagonal:
                row = lax.broadcasted_iota(jnp.int32, (ATT_TQ, ATT_TK), 0)
                col = lax.broadcasted_iota(jnp.int32, (ATT_TQ, ATT_TK), 1)
                s = jnp.where(col <= row, s, NEG_BIG)
            m_old = m_ref[hd]
            s_max = jnp.max(jnp.maximum(s[:, :128], s[:, 128:]), axis=-1, keepdims=True)
            m_new = jnp.maximum(m_old, jnp.broadcast_to(s_max, m_old.shape))
            p = jnp.concatenate([jnp.exp(s[:, :128] - m_new), jnp.exp(s[:, 128:] - m_new)], axis=1)
            alpha = jnp.exp(m_old - m_new)
            pv = _dot(p.astype(BF16), cb)
            acc_ref[hd] = jnp.concatenate([alpha, alpha], axis=1) * acc_ref[hd] + pv
            m_ref[hd] = m_new

    def body(j, carry):
        block(j, False)
        return carry

    lax.fori_loop(0, i, body, 0)
    block(i, True)
    for hd in range(N_HEADS):
        acc = acc_ref[hd]
        o_ref[:, hd * HEAD_PAD:(hd + 1) * HEAD_PAD] = (
            acc[:, :KV_LORA_RANK] / acc[:, KV_LORA_RANK:]).astype(o_ref.dtype)


def _attn_prompt(q, k, ckv):
    nq = SEQ // ATT_TQ
    wide = N_HEADS * HEAD_PAD
    return pl.pallas_call(
        _attn_prompt_kernel,
        grid=(BATCH, nq),
        in_specs=[
            pl.BlockSpec((ATT_TQ, wide), lambda b, i: (b * nq + i, 0)),
            pl.BlockSpec((SEQ, wide), lambda b, i: (b, 0)),
            pl.BlockSpec((SEQ, KV_LORA_RANK), lambda b, i: (b, 0)),
        ],
        out_specs=pl.BlockSpec((ATT_TQ, wide), lambda b, i: (b * nq + i, 0)),
        out_shape=jax.ShapeDtypeStruct((BATCH * SEQ, wide), BF16),
        scratch_shapes=[pltpu.VMEM((N_HEADS, ATT_TQ, 128), F32),
                        pltpu.VMEM((N_HEADS, ATT_TQ, 2 * KV_LORA_RANK), F32)],
        compiler_params=_cparams("parallel", "parallel"),
        name="attn_prompt",
    )(q, k, ckv)


SA_PAGES = 32
SA_POS = SA_PAGES * PAGE_SIZE
SA_STEPS = N_PAGES // SA_PAGES


def _absorb_kernel(q_ref, gk_ref, wukt_ref, qa_ref, qr_ref):
    gk = gk_ref[...]
    for hd in range(N_HEADS):
        qh = q_ref[:, hd * HEAD_PAD:(hd + 1) * HEAD_PAD]
        qa_ref[hd] = _dot((qh * gk).astype(BF16), wukt_ref[hd])
        qr_ref[hd] = pltpu.roll(qh, HEAD_PAD - QK_NOPE_DIM, 1)


def _absorb(q_s, gk_nope, wukt):
    b = q_s.shape[0]
    shp = jax.ShapeDtypeStruct((N_HEADS, b, HEAD_PAD), F32)
    return pl.pallas_call(
        _absorb_kernel,
        in_specs=[_full(q_s.shape), _full(gk_nope.shape), _full(wukt.shape)],
        out_specs=[_full(shp.shape), _full(shp.shape)],
        out_shape=[shp, shp],
        grid=(1,),
        compiler_params=_cparams("arbitrary"),
        name="absorb_q",
    )(q_s, gk_nope, wukt)


def _attn_sample_kernel(pt_ref, kv_hbm, kr_hbm, wukt_ref, qa_ref, qr_ref, qs_ref, knew_ref, cnew_ref,
                        gkr_ref, cos_ref, sin_ref, o_ref, kv_buf, kr_buf, sems, m_ref, l_ref, acc_ref, ssq_ref):
    s, c = pl.program_id(0), pl.program_id(1)
    g = s * SA_STEPS + c
    last = pl.num_programs(0) * SA_STEPS - 1

    def page_copies(step, slot):
        seq, chunk = step // SA_STEPS, step % SA_STEPS
        for p in range(SA_PAGES):
            page = pt_ref[seq, chunk * SA_PAGES + p]
            yield pltpu.make_async_copy(kv_hbm.at[page], kv_buf.at[slot, p], sems.at[slot, 0])
            yield pltpu.make_async_copy(kr_hbm.at[page], kr_buf.at[slot, :, pl.ds(p * PAGE_SIZE, PAGE_SIZE)],
                                        sems.at[slot, 1])

    @pl.when(g == 0)
    def _():
        for cp in page_copies(g, 0):
            cp.start()

    @pl.when(g < last)
    def _():
        for cp in page_copies(g + 1, (g + 1) % 2):
            cp.start()

    slot = g % 2
    for cp in page_copies(g, slot):
        cp.wait()

    @pl.when(c == 0)
    def _():
        m_ref[...] = jnp.full_like(m_ref, NEG_BIG)
        l_ref[...] = jnp.zeros_like(l_ref)
        acc_ref[...] = jnp.zeros_like(acc_ref)

    cc = kv_buf[slot].reshape(SA_POS, KV_LORA_RANK).astype(BF16)
    qa16 = jnp.concatenate([qa_ref[...], jnp.zeros((8, KV_LORA_RANK), F32)], axis=0).astype(BF16)
    lhs = jnp.concatenate([wukt_ref[...], qa16], axis=0)
    res = _dot_nt(lhs, cc)
    for hd in range(N_HEADS):
        x = res[hd * QK_NOPE_DIM:(hd + 1) * QK_NOPE_DIM]
        ssq_ref[hd:hd + 1, :] = jnp.sum(x * x, axis=0, keepdims=True)
    sn = res[N_HEADS * QK_NOPE_DIM:N_HEADS * QK_NOPE_DIM + N_HEADS]

    krt = kr_buf[slot]
    kr2 = jnp.sum(krt * krt, axis=0, keepdims=True)
    krg = krt * gkr_ref[...]
    x1, x2 = krg[:ROPE_HALF], krg[ROPE_HALF:]
    cos, sin = cos_ref[...], sin_ref[...]
    rk = jnp.concatenate([x1 * cos - x2 * sin, x2 * cos + x1 * sin], axis=0).astype(BF16)
    sr = _dot(qr_ref[...][:, :QK_ROPE_DIM].astype(BF16), rk)

    rinv = lax.rsqrt((ssq_ref[...] + kr2) / QK_DIM + RMS_EPS)
    s = (sn + sr) * rinv * SM_SCALE
    m_old = m_ref[...]
    m_new = jnp.maximum(m_old, jnp.max(s, axis=-1, keepdims=True))
    p = jnp.exp(s - m_new)
    alpha = jnp.exp(m_old - m_new)
    l_ref[...] = alpha * l_ref[...] + jnp.sum(p, axis=-1, keepdims=True)
    acc_ref[...] = alpha * acc_ref[...] + _dot(p.astype(BF16), cc)
    m_ref[...] = m_new

    @pl.when(c == SA_STEPS - 1)
    def _():
        s_new = jnp.sum(qs_ref[...] * knew_ref[...], axis=-1, keepdims=True) * SM_SCALE
        m_o = m_ref[...]
        m_n = jnp.maximum(m_o, s_new)
        p_new = jnp.exp(s_new - m_n)
        a = jnp.exp(m_o - m_n)
        l = a * l_ref[...] + p_new
        acc = a * acc_ref[...] + p_new * cnew_ref[...]
        o_ref[...] = acc / l


def _attn_sample(page_table, cache_kv, cache_kr, wukt_flat, qa, qr, qs, knew, cnew, gkr, cos_t, sin_t):
    b = qa.shape[0]

    per_seq = pl.BlockSpec((None, N_HEADS, HEAD_PAD), lambda s, c, pt: (s, 0, 0))
    in_specs = [
        pl.BlockSpec(memory_space=pl.ANY), pl.BlockSpec(memory_space=pl.ANY),
        pl.BlockSpec(wukt_flat.shape, lambda s, c, pt: (0, 0)),
        per_seq, per_seq, per_seq, per_seq,
        pl.BlockSpec((None, 1, KV_LORA_RANK), lambda s, c, pt: (s, 0, 0)),
        pl.BlockSpec(gkr.shape, lambda s, c, pt: (0, 0)),
        pl.BlockSpec((ROPE_HALF, SA_POS), lambda s, c, pt: (0, c)),
        pl.BlockSpec((ROPE_HALF, SA_POS), lambda s, c, pt: (0, c)),
    ]
    grid_spec = pltpu.PrefetchScalarGridSpec(
        num_scalar_prefetch=1,
        grid=(b, SA_STEPS),
        in_specs=in_specs,
        out_specs=pl.BlockSpec((None, N_HEADS, KV_LORA_RANK), lambda s, c, pt: (s, 0, 0)),
        scratch_shapes=[
            pltpu.VMEM((2, SA_PAGES, PAGE_SIZE, KV_LORA_RANK), F32),
            pltpu.VMEM((2, QK_ROPE_DIM, SA_POS), F32),
            pltpu.SemaphoreType.DMA((2, 2)),
            pltpu.VMEM((N_HEADS, 1), F32), pltpu.VMEM((N_HEADS, 1), F32),
            pltpu.VMEM((N_HEADS, KV_LORA_RANK), F32), pltpu.VMEM((N_HEADS, SA_POS), F32),
        ],
    )
    return pl.pallas_call(
        _attn_sample_kernel,
        grid_spec=grid_spec,
        out_shape=jax.ShapeDtypeStruct((b, N_HEADS, KV_LORA_RANK), F32),
        compiler_params=_cparams("arbitrary", "arbitrary"),
        name="attn_sample",
    )(page_table, cache_kv, cache_kr, wukt_flat, qa, qr, qs, knew, cnew, gkr, cos_t, sin_t)


def _mla_post_kernel(ol_ref, wuv_ref, wo_ref, y_ref, gate_ref, o_ref):
    parts = [_dot(ol_ref[:, hd * KV_LORA_RANK:(hd + 1) * KV_LORA_RANK], wuv_ref[hd])
             for hd in range(N_HEADS)]
    o = jnp.concatenate(parts, axis=-1).astype(BF16)
    o_ref[...] = y_ref[...] + gate_ref[...] * _dot(o, wo_ref[...])


def _mla_post(o_lat, wuv, wo, y, gate, tm, tiles_per_seq):
    t = y.shape[0]
    tok = lambda n: pl.BlockSpec((tm, n), lambda i: (i, 0))
    return pl.pallas_call(
        _mla_post_kernel,
        grid=(t // tm,),
        in_specs=[tok(N_HEADS * KV_LORA_RANK), _full(wuv.shape), _full(wo.shape), tok(D_MODEL),
                  _mod_spec(gate, tiles_per_seq)],
        out_specs=tok(D_MODEL),
        out_shape=jax.ShapeDtypeStruct((t, D_MODEL), F32),
        compiler_params=_cparams("parallel"),
        name="mla_post",
    )(o_lat, wuv, wo, y, gate)


def _sgu_kernel(y_ref, sh_ref, sc_ref, gate_ref, g1_ref, win_ref, bin_ref, gv_ref, sp_a_ref, sp_b_ref,
                wout_ref, o_ref, v_ref, *, tm, spatial):
    y = y_ref[...]
    h = _rms(y) * g1_ref[...]
    h = h * (1.0 + sc_ref[...]) + sh_ref[...]
    z = _gelu(_dot(h.astype(BF16), win_ref[...]) + bin_ref[...])
    u = z[:, :SG_HALF]
    v = _rms(z[:, SG_HALF:]) * gv_ref[...]
    if spatial:
        r = lax.broadcasted_iota(jnp.int32, (CHUNK, CHUNK), 0)
        cidx = lax.broadcasted_iota(jnp.int32, (CHUNK, CHUNK), 1)
        vb = v.astype(BF16)
        cols = []
        for g in range(SG_GROUPS):
            wc = jnp.where(cidx <= r, sp_a_ref[g], 0.0).astype(BF16)
            rows = [_dot(wc, vb[c * CHUNK:(c + 1) * CHUNK, g * CHUNK:(g + 1) * CHUNK]) + sp_b_ref[g]
                    for c in range(tm // CHUNK)]
            cols.append(jnp.concatenate(rows, axis=0))
        s = jnp.concatenate(cols, axis=-1)
        v_ref[...] = v[tm - CHUNK:, :]
    else:
        s = v * sp_a_ref[...] + sp_b_ref[...]
        v_ref[...] = v
    o = _dot((u * s).astype(BF16), wout_ref[...])
    o_ref[...] = y + gate_ref[...] * o


def _sgu(y, sh, sc, gate, g1, w, sp_a, sp_b, tm, tiles_per_seq, spatial):
    t = y.shape[0]
    tok = lambda n: pl.BlockSpec((tm, n), lambda i: (i, 0))
    if spatial:
        n_seq = t // (tm * tiles_per_seq)
        v_spec = pl.BlockSpec((None, CHUNK, SG_HALF), lambda i: (i // tiles_per_seq, 0, 0))
        v_shape = jax.ShapeDtypeStruct((n_seq, CHUNK, SG_HALF), F32)
    else:
        v_spec = tok(SG_HALF)
        v_shape = jax.ShapeDtypeStruct((t, SG_HALF), F32)
    return pl.pallas_call(
        functools.partial(_sgu_kernel, tm=tm, spatial=spatial),
        grid=(t // tm,),
        in_specs=[tok(D_MODEL), _mod_spec(sh, tiles_per_seq), _mod_spec(sc, tiles_per_seq),
                  _mod_spec(gate, tiles_per_seq), _full((1, D_MODEL)), _full(w["w_in"].shape),
                  _full((1, 2 * SG_HALF)), _full((1, SG_HALF)), _full(sp_a.shape), _full(sp_b.shape),
                  _full(w["w_out"].shape)],
        out_specs=[tok(D_MODEL), v_spec],
        out_shape=[jax.ShapeDtypeStruct((t, D_MODEL), F32), v_shape],
        compiler_params=_cparams("arbitrary"),
        name="sgu",
    )(y, sh, sc, gate, g1, w["w_in"], w["b_in"], w["g_v"], sp_a, sp_b, w["w_out"])


N_TOP = PEER_TOPK + 1
TOP_ROWS = 24
N_CAND = TOP_ROWS + 7 * 8 + (TOP_ROWS - 8)


def _top_values(x, n, emit):
    prev = None
    for r in range(n):
        cand = x if prev is None else jnp.where(x < prev, x, NEG_BIG)
        prev = jnp.max(cand, axis=0, keepdims=True)
        emit(r, prev)


def _pair_words(x):
    bits = pltpu.bitcast(x.astype(BF16).astype(F32), jnp.uint32)
    return bits | (bits >> 16)


def _peer_pre_kernel(y_ref, sh_ref, sc_ref, g2_ref, wq_ref, sk_ref, ht_ref, rank_ref, eb_ref, cnt_ref,
                     cc_ref, s_scr, t_scr, cand_scr, tv_scr):
    h = _rms(y_ref[...]) * g2_ref[...]
    h = h * (1.0 + sc_ref[...]) + sh_ref[...]
    ht_ref[...] = h.T.astype(BF16)
    q = _dot(h.astype(BF16), wq_ref[...])
    for hd in range(PEER_HEADS):
        for side in range(2):
            o = (hd * 2 + side) * N_KEYS
            s_scr[side, hd] = _dot_nt(sk_ref[side], q[:, o:o + N_KEYS].astype(BF16))
    t_scr[...] = jnp.full_like(t_scr, NEG_BIG)

    def tops(idx, _):
        side, hd = idx // PEER_HEADS, idx % PEER_HEADS

        def emit(r, row):
            t_scr[side, hd, r:r + 1, :] = row

        _top_values(s_scr[side, hd], N_TOP, emit)
        return 0

    lax.fori_loop(0, 2 * PEER_HEADS, tops, 0)

    def finish(hd, _):
        t1, t2 = t_scr[0, hd], t_scr[1, hd]
        cand_scr[0:TOP_ROWS] = t1[0:1] + t2
        for a in range(1, 8):
            cand_scr[TOP_ROWS + (a - 1) * 8:TOP_ROWS + a * 8] = t1[a:a + 1] + t2[0:8]
        cand_scr[TOP_ROWS + 56:] = t1[8:] + t2[0:1]
        cand = cand_scr[...]

        def emit(r, row):
            tv_scr[r:r + 1, :] = row

        _top_values(cand, N_TOP, emit)
        tau = 0.5 * (tv_scr[PEER_TOPK - 1:PEER_TOPK, :] + tv_scr[PEER_TOPK:PEER_TOPK + 1, :])
        top = t1[:1] + t2[:1]
        z = jnp.sum(jnp.where(cand >= tau, jnp.exp(cand - top), 0.0), axis=0, keepdims=True)
        s1, s2 = s_scr[0, hd], s_scr[1, hd]
        eb = jnp.exp(s2 - t2[:1])
        th = tau - s1
        cc = jnp.exp(s1 - t1[:1]) * (0.5 / z)
        rank = jnp.zeros_like(s2)
        cnt = jnp.zeros_like(s1)
        for b in range(PEER_TOPK):
            tb = t2[b:b + 1]
            rank = jnp.where(tb > s2, b + 1.0, rank)
            cnt = jnp.where(tb >= th, b + 1.0, cnt)
        cntw = _pair_words(cnt)
        ccw = _pair_words(cc)
        for col in range(s1.shape[1] // 128):
            cs = slice(col * 128, (col + 1) * 128)
            rank_ref[hd, col] = pltpu.bitcast(rank[:, cs].astype(BF16), jnp.uint32)
            eb_ref[hd, col] = pltpu.bitcast(eb[:, cs].astype(BF16), jnp.uint32)
            cnt_ref[hd, col] = cntw[:, cs]
            cc_ref[hd, col] = ccw[:, cs]
        return 0

    lax.fori_loop(0, PEER_HEADS, finish, 0)


def _peer_pre(y, sh, sc, g2, wq, sk, tm, tiles_per_seq):
    t = y.shape[0]
    hk = pl.BlockSpec((PEER_HEADS, tm // 128, N_KEYS, 128), lambda i: (0, i, 0, 0))
    hp = pl.BlockSpec((PEER_HEADS, tm // 128, N_KEYS // 2, 128), lambda i: (0, i, 0, 0))
    packed = jax.ShapeDtypeStruct((PEER_HEADS, t // 128, N_KEYS // 2, 128), jnp.uint32)
    words = jax.ShapeDtypeStruct((PEER_HEADS, t // 128, N_KEYS, 128), jnp.uint32)
    return pl.pallas_call(
        _peer_pre_kernel,
        grid=(t // tm,),
        in_specs=[pl.BlockSpec((tm, D_MODEL), lambda i: (i, 0)), _mod_spec(sh, tiles_per_seq),
                  _mod_spec(sc, tiles_per_seq), _full((1, D_MODEL)), _full(wq.shape), _full(sk.shape)],
        out_specs=[pl.BlockSpec((D_MODEL, tm), lambda i: (0, i)), hp, hp, hk, hk],
        out_shape=[jax.ShapeDtypeStruct((D_MODEL, t), BF16), packed, packed, words, words],
        scratch_shapes=[pltpu.VMEM((2, PEER_HEADS, N_KEYS, tm), F32),
                        pltpu.VMEM((2, PEER_HEADS, TOP_ROWS, tm), F32),
                        pltpu.VMEM((N_CAND, tm), F32),
                        pltpu.VMEM((TOP_ROWS, tm), F32)],
        compiler_params=_cparams("parallel"),
        name="peer_pre",
    )(y, sh, sc, g2, wq, sk)


PEER_EB = 1024
PEER_ROWS = PEER_EB // N_KEYS
PEER_BLOCKS = N_EXPERTS // PEER_EB


BF16_ROWS = 16


def _selection_weights(rank_ref, eb_ref, cnt_ref, cc_ref, col):
    def row_of(ref, hd, ii):
        word = jnp.broadcast_to(ref[hd, col, ii:ii + 1, :], (8, 128))
        tile = pltpu.bitcast(word, BF16)
        return jnp.concatenate([tile] * (N_KEYS // BF16_ROWS), axis=0)

    rows = []
    for ii in range(PEER_ROWS):
        w = None
        for hd in range(PEER_HEADS):
            eb = pltpu.bitcast(eb_ref[hd, col], BF16)
            rank = pltpu.bitcast(rank_ref[hd, col], BF16)
            t = jnp.where(rank < row_of(cnt_ref, hd, ii), eb, jnp.zeros_like(eb))
            t = t * row_of(cc_ref, hd, ii)
            w = t if w is None else w + t
        rows.append(w)
    return jnp.concatenate(rows, axis=0)


def _peer_dense_kernel(ht_ref, u_ref, vt_ref, rank_ref, eb_ref, cnt_ref, cc_ref, y_ref, gate_ref, o_ref,
                       acc_ref, *, tm):
    e = pl.program_id(1)

    @pl.when(e == 0)
    def _():
        acc_ref[...] = jnp.zeros_like(acc_ref)

    act = _gelu_twice(_dot(u_ref[...], ht_ref[...]))
    w = jnp.concatenate([_selection_weights(rank_ref, eb_ref, cnt_ref, cc_ref, col)
                         for col in range(tm // 128)], axis=1)
    acc_ref[...] += _dot(vt_ref[...], act.astype(BF16) * w)

    @pl.when(e == PEER_BLOCKS - 1)
    def _():
        o_ref[...] = y_ref[...] + gate_ref[...] * acc_ref[...].T


def _peer_dense(ht, u_b, vt_b, layer, rank, eb, cnt, cc, y, gate, tm, tiles_per_seq):
    t = y.shape[0]
    cols = tm // 128
    hk = pl.BlockSpec((PEER_HEADS, cols, N_KEYS // 2, 128), lambda i, e: (0, i, 0, 0))
    hr = pl.BlockSpec((PEER_HEADS, cols, PEER_ROWS, 128), lambda i, e: (0, i, e, 0))
    return pl.pallas_call(
        functools.partial(_peer_dense_kernel, tm=tm),
        grid=(t // tm, PEER_BLOCKS),
        in_specs=[pl.BlockSpec((D_MODEL, tm), lambda i, e: (0, i)),
                  pl.BlockSpec((None, PEER_EB, D_MODEL), lambda i, e: (layer, e, 0)),
                  pl.BlockSpec((None, D_MODEL, PEER_EB), lambda i, e: (layer, 0, e)),
                  hk, hk, hr, hr,
                  pl.BlockSpec((tm, D_MODEL), lambda i, e: (i, 0)),
                  pl.BlockSpec((None,) + gate.shape[1:], lambda i, e: (i // tiles_per_seq, 0, 0))],
        out_specs=pl.BlockSpec((tm, D_MODEL), lambda i, e: (i, 0)),
        out_shape=jax.ShapeDtypeStruct((t, D_MODEL), F32),
        scratch_shapes=[pltpu.VMEM((D_MODEL, tm), F32)],
        compiler_params=_cparams("parallel", "arbitrary"),
        name="peer_dense",
    )(ht, u_b, vt_b, rank, eb, cnt, cc, y, gate)


def _peer(y, sh, sc, gate, g2, wq, sk, u_b, vt_b, layer, tm, tiles_per_seq, dense_tiles=1):
    ht, rank, eb, cnt, cc = _peer_pre(y, sh, sc, g2, wq, sk, tm, tiles_per_seq)
    return _peer_dense(ht, u_b, vt_b, layer, rank, eb, cnt, cc, y, gate, tm * dense_tiles,
                       tiles_per_seq // dense_tiles)


def _pad_heads(w, used):
    w = jnp.pad(w, [(0, 0)] * (w.ndim - 1) + [(0, HEAD_PAD - used)])
    return w.reshape(w.shape[:-2] + (N_HEADS * HEAD_PAD,))


def _pad_gain(g):
    return jnp.pad(g, (0, HEAD_PAD - QK_DIM)).reshape(1, HEAD_PAD)


def _rope_tables(pos):
    inv_freq = ROPE_THETA ** (-jnp.arange(ROPE_HALF, dtype=F32) * 2.0 / QK_ROPE_DIM)
    ang = pos.astype(F32)[:, None] * inv_freq[None, :]
    cos, sin = jnp.cos(ang), jnp.sin(ang)
    t = pos.shape[0]
    one = jnp.ones((t, QK_NOPE_DIM), F32)
    z64 = jnp.zeros((t, QK_NOPE_DIM), F32)
    z16 = jnp.zeros((t, ROPE_HALF), F32)
    z32 = jnp.zeros((t, HEAD_PAD - QK_DIM), F32)
    cos_t = jnp.concatenate([one, cos, cos, z32], axis=1)
    sa = jnp.concatenate([z64, -sin, z16, z32], axis=1)
    sb = jnp.concatenate([z64, z16, sin, z32], axis=1)
    return cos_t, sa, sb, cos, sin


def _mla_weights(w_in, g_q_a, g_kv_a, w_uq, g_q, w_uk):
    lo = Q_LORA_RANK + KV_LORA_RANK
    w_in_p = jnp.concatenate(
        [w_in[:, :lo], jnp.zeros((D_MODEL, QK_NOPE_DIM), F32), w_in[:, lo:],
         jnp.zeros((D_MODEL, HEAD_PAD - QK_DIM), F32)], axis=1).astype(BF16)
    return {
        "w_in": w_in_p,
        "g_q_a": g_q_a.reshape(1, -1),
        "g_kv_a": g_kv_a.reshape(1, -1),
        "w_uq": _pad_heads(w_uq.reshape(Q_LORA_RANK, N_HEADS, QK_DIM), QK_DIM).astype(BF16),
        "g_q": _pad_gain(g_q),
        "w_uk": _pad_heads(w_uk, QK_NOPE_DIM).astype(BF16),
    }


def _split_mod(m, n_prompt):
    m = m.reshape(m.shape[0], 6, D_MODEL)
    mp = [m[:n_prompt, j].reshape(n_prompt, 1, D_MODEL) for j in range(6)]
    ms = [m[n_prompt:, j].reshape(1, -1, D_MODEL) for j in range(6)]
    return mp, ms


def kernel(x_prompt, x_sample, c_prompt, c_sample, cache_kv_latent, cache_k_rope, page_table, norm1_g, norm2_g, w_mod, b_mod, mla_w_in, mla_g_q_a, mla_g_kv_a, mla_w_uq, mla_g_q, mla_w_uk, mla_w_uv, mla_g_k, mla_w_o, sg_w_in, sg_b_in, sg_g_v, sg_w_s, sg_b_s, sg_w_out, peer_w_q, peer_sub_keys, peer_u, peer_v):
    yp = x_prompt.reshape(BATCH * SEQ, D_MODEL)
    ys = x_sample.reshape(DEC_BATCH, D_MODEL)
    mods = _modulation(jnp.concatenate([c_prompt, c_sample], axis=0), w_mod, b_mod)

    tm_p, tps_p = 512, SEQ // 512
    tm_s, tps_s = DEC_BATCH, 1

    u_b = peer_u.astype(BF16)
    vt_b = jnp.swapaxes(peer_v, 1, 2).astype(BF16)

    outs = {}
    for layer in range(DEPTH):
        mp, ms = _split_mod(mods[layer], BATCH)
        g1 = norm1_g[layer].reshape(1, D_MODEL)
        g2 = norm2_g[layer].reshape(1, D_MODEL)
        if layer % 2 == 0:
            a = layer // 2
            w = _mla_weights(mla_w_in[a], mla_g_q_a[a], mla_g_kv_a[a], mla_w_uq[a], mla_g_q[a], mla_w_uk[a])
            gk = _pad_gain(mla_g_k[a])
            w["g_k"] = gk
            wuv = jnp.transpose(mla_w_uv[a], (1, 0, 2)).astype(BF16)
            wo = mla_w_o[a].astype(BF16)
            cos_p, sa_p, sb_p, _, _ = _rope_tables(jnp.arange(SEQ))
            cos_s, sa_s, sb_s, _, _ = _rope_tables(jnp.full((1,), PAST_LEN))
            _, _, _, cos_c, sin_c = _rope_tables(jnp.arange(PAST_LEN))

            q, k, ckv_p, krp_p = _mla_pre(yp, mp[0], mp[1], g1, w, (cos_p, sa_p, sb_p), tm_p, tps_p, True, BF16)
            o_lat = _attn_prompt(q, k, ckv_p)
            yp = _mla_post(o_lat, wuv, wo, yp, mp[2], tm_p, tps_p)

            q_s, k_s, ckv_s, krp_s = _mla_pre(ys, ms[0], ms[1], g1, w, (cos_s, sa_s, sb_s), tm_s, tps_s, False, F32)
            wukt = jnp.pad(jnp.transpose(mla_w_uk[a], (1, 2, 0)),
                           ((0, 0), (0, HEAD_PAD - QK_NOPE_DIM), (0, 0))).astype(BF16)
            gk_nope = gk * (jnp.arange(HEAD_PAD) < QK_NOPE_DIM)[None, :]
            qa, qr = (jnp.transpose(x, (1, 0, 2)) for x in _absorb(q_s, gk_nope, wukt))
            wukt_flat = jnp.transpose(mla_w_uk[a], (1, 2, 0)).reshape(N_HEADS * QK_NOPE_DIM, KV_LORA_RANK).astype(BF16)
            gkr = mla_g_k[a][QK_NOPE_DIM:].reshape(QK_ROPE_DIM, 1)
            o_lat_s = _attn_sample(
                page_table, cache_kv_latent[a], jnp.swapaxes(cache_k_rope[a], 1, 2), wukt_flat, qa, qr,
                q_s.reshape(DEC_BATCH, N_HEADS, HEAD_PAD), k_s.reshape(DEC_BATCH, N_HEADS, HEAD_PAD),
                ckv_s.reshape(DEC_BATCH, 1, KV_LORA_RANK), gkr, cos_c.T, sin_c.T)
            ys = _mla_post(o_lat_s.reshape(DEC_BATCH, N_HEADS * KV_LORA_RANK).astype(BF16), wuv, wo, ys,
                           ms[2], tm_s, tps_s)

            outs.setdefault("kv_p", []).append(ckv_p.reshape(BATCH, SEQ, KV_LORA_RANK))
            outs.setdefault("kr_p", []).append(krp_p[:, QK_NOPE_DIM:QK_DIM].reshape(BATCH, SEQ, QK_ROPE_DIM))
            outs.setdefault("kv_s", []).append(ckv_s.reshape(DEC_BATCH, 1, KV_LORA_RANK))
            outs.setdefault("kr_s", []).append(krp_s[:, QK_NOPE_DIM:QK_DIM].reshape(DEC_BATCH, 1, QK_ROPE_DIM))
        else:
            bidx = layer // 2
            w = {"w_in": sg_w_in[bidx].astype(BF16), "b_in": sg_b_in[bidx].reshape(1, -1),
                 "g_v": sg_g_v[bidx].reshape(1, -1), "w_out": sg_w_out[bidx].astype(BF16)}
            sp_a = sg_w_s[bidx]
            sp_b = sg_b_s[bidx].reshape(SG_GROUPS, CHUNK, 1)
            yp, vp = _sgu(yp, mp[0], mp[1], mp[2], g1, w, sp_a, sp_b, tm_p, tps_p, True)
            coef = jnp.repeat(sg_w_s[bidx][:, 0, 0], CHUNK).reshape(1, SG_HALF)
            bias = jnp.repeat(sg_b_s[bidx][:, 0], CHUNK).reshape(1, SG_HALF)
            ys, vs = _sgu(ys, ms[0], ms[1], ms[2], g1, w, coef, bias, tm_s, tps_s, False)
            outs.setdefault("v_p", []).append(vp)
            outs.setdefault("v_s", []).append(vs.reshape(DEC_BATCH, 1, SG_HALF))

        wq = peer_w_q[layer].astype(BF16)
        sk = peer_sub_keys[layer].astype(BF16)
        yp = _peer(yp, mp[3], mp[4], mp[5], g2, wq, sk, u_b, vt_b, layer, tm_p, tps_p, dense_tiles=2)
        ys = _peer(ys, ms[3], ms[4], ms[5], g2, wq, sk, u_b, vt_b, layer, tm_s, tps_s)

    return (yp.reshape(BATCH, SEQ, D_MODEL), ys.reshape(DEC_BATCH, 1, D_MODEL),
            jnp.stack(outs["kv_p"]), jnp.stack(outs["kr_p"]), jnp.stack(outs["kv_s"]), jnp.stack(outs["kr_s"]),
            jnp.stack(outs["v_p"]), jnp.stack(outs["v_s"]))
```

```python
import functools

import jax
import jax.numpy as jnp
from jax import lax
from jax.experimental import pallas as pl
from jax.experimental.pallas import tpu as pltpu

F32 = jnp.float32
BF16 = jnp.bfloat16

D_MODEL = 1024
BATCH = 8
SEQ = 2048
DEPTH = 2
DEC_BATCH = 128
PAST_LEN = 16384
PAGE_SIZE = 128
N_PAGES = PAST_LEN // PAGE_SIZE

N_HEADS = 8
Q_LORA_RANK = 256
KV_LORA_RANK = 128
QK_NOPE_DIM = 64
QK_ROPE_DIM = 32
QK_DIM = QK_NOPE_DIM + QK_ROPE_DIM
V_HEAD_DIM = 128
ROPE_THETA = 10000.0
SM_SCALE = QK_DIM ** -0.5
HEAD_PAD = 128
ROPE_HALF = QK_ROPE_DIM // 2

CHUNK = 128
SG_HALF = 1536
SG_GROUPS = 12

PEER_HEADS = 8
N_KEYS = 128
N_EXPERTS = N_KEYS * N_KEYS
PEER_TOPK = 16
RMS_EPS = 1e-6

NEG_BIG = -1e30
VMEM_LIMIT = 56 * 1024 * 1024


def _cparams(*sem):
    return pltpu.CompilerParams(dimension_semantics=sem, vmem_limit_bytes=VMEM_LIMIT)


def _rms(x):
    return x * lax.rsqrt(jnp.mean(x * x, axis=-1, keepdims=True) + RMS_EPS)


def _gelu(x):
    cdf = 0.5 * (1.0 + jnp.tanh(0.7978845608028654 * (x + 0.044715 * (x * x * x))))
    return x * cdf


def _gelu_twice(x):
    c = 0.7978845608028654
    return x * (1.0 + jnp.tanh(x * (c + (0.044715 * c) * (x * x))))


def _dot(a, b):
    return jnp.dot(a, b, preferred_element_type=F32)


def _dot_nt(a, b):
    return lax.dot_general(a, b, (((1,), (1,)), ((), ())), preferred_element_type=F32)


def _full(shape):
    n = len(shape)
    return pl.BlockSpec(shape, lambda *_: (0,) * n)


def _mod_spec(arr, tiles_per_seq):
    return pl.BlockSpec((None,) + arr.shape[1:], lambda t, *_: (t // tiles_per_seq, 0, 0))


def _mod_kernel(c_ref, w_ref, b_ref, o_ref):
    c = c_ref[...]
    a = (c * jax.nn.sigmoid(c)).astype(BF16)
    o_ref[...] = _dot(a, w_ref[...].astype(BF16)) + b_ref[...]


def _modulation(c_all, w_mod, b_mod):
    r = c_all.shape[0]
    tn = 1536
    return pl.pallas_call(
        _mod_kernel,
        grid=(DEPTH, 6 * D_MODEL // tn),
        in_specs=[
            pl.BlockSpec((r, D_MODEL), lambda l, n: (0, 0)),
            pl.BlockSpec((None, D_MODEL, tn), lambda l, n: (l, 0, n)),
            pl.BlockSpec((None, 1, tn), lambda l, n: (l, 0, n)),
        ],
        out_specs=pl.BlockSpec((None, r, tn), lambda l, n: (l, 0, n)),
        out_shape=jax.ShapeDtypeStruct((DEPTH, r, 6 * D_MODEL), F32),
        compiler_params=_cparams("parallel", "parallel"),
        name="modulation",
    )(c_all, w_mod, b_mod.reshape(DEPTH, 1, 6 * D_MODEL))


def _rope(x, cos, sa, sb):
    return x * cos + pltpu.roll(x, HEAD_PAD - ROPE_HALF, 1) * sa + pltpu.roll(x, ROPE_HALF, 1) * sb


def _mla_pre_kernel(y_ref, sh_ref, sc_ref, g1_ref, win_ref, gqa_ref, gkva_ref, wuq_ref, gq_ref,
                    wuk_ref, gk_ref, cos_ref, sa_ref, sb_ref, q_ref, k_ref, ckv_ref, kr_ref):
    h = _rms(y_ref[...]) * g1_ref[...]
    h = h * (1.0 + sc_ref[...]) + sh_ref[...]
    z = _dot(h.astype(BF16), win_ref[...])
    cq = _rms(z[:, :Q_LORA_RANK]) * gqa_ref[...]
    ckv = _rms(z[:, Q_LORA_RANK:Q_LORA_RANK + KV_LORA_RANK]) * gkva_ref[...]
    krp = z[:, Q_LORA_RANK + KV_LORA_RANK:]
    ckv_ref[...] = ckv
    kr_ref[...] = krp
    q = _dot(cq.astype(BF16), wuq_ref[...])
    kn = _dot(ckv.astype(BF16), wuk_ref[...])
    cos, sa, sb = cos_ref[...], sa_ref[...], sb_ref[...]
    gq, gk = gq_ref[...], gk_ref[...]
    for hd in range(N_HEADS):
        sl = slice(hd * HEAD_PAD, (hd + 1) * HEAD_PAD)
        qh = q[:, sl]
        qh = qh * lax.rsqrt(jnp.sum(qh * qh, axis=-1, keepdims=True) / QK_DIM + RMS_EPS) * gq
        q_ref[:, sl] = _rope(qh, cos, sa, sb).astype(q_ref.dtype)
        kh = kn[:, sl] + krp
        kh = kh * lax.rsqrt(jnp.sum(kh * kh, axis=-1, keepdims=True) / QK_DIM + RMS_EPS) * gk
        k_ref[:, sl] = _rope(kh, cos, sa, sb).astype(k_ref.dtype)


def _mla_pre(y, sh, sc, g1, w, rope_tabs, tm, tiles_per_seq, rope_tiled, qk_dtype):
    t = y.shape[0]
    cos, sa, sb = rope_tabs
    if rope_tiled:
        rspec = pl.BlockSpec((tm, HEAD_PAD), lambda i: (i % tiles_per_seq, 0))
    else:
        rspec = _full((1, HEAD_PAD))
    wide = N_HEADS * HEAD_PAD
    tok = lambda n: pl.BlockSpec((tm, n), lambda i: (i, 0))
    return pl.pallas_call(
        _mla_pre_kernel,
        grid=(t // tm,),
        in_specs=[
            tok(D_MODEL), _mod_spec(sh, tiles_per_seq), _mod_spec(sc, tiles_per_seq),
            _full((1, D_MODEL)), _full(w["w_in"].shape), _full((1, Q_LORA_RANK)),
            _full((1, KV_LORA_RANK)), _full(w["w_uq"].shape), _full((1, HEAD_PAD)),
            _full(w["w_uk"].shape), _full((1, HEAD_PAD)), rspec, rspec, rspec,
        ],
        out_specs=[tok(wide), tok(wide), tok(KV_LORA_RANK), tok(HEAD_PAD)],
        out_shape=[
            jax.ShapeDtypeStruct((t, wide), qk_dtype),
            jax.ShapeDtypeStruct((t, wide), qk_dtype),
            jax.ShapeDtypeStruct((t, KV_LORA_RANK), F32),
            jax.ShapeDtypeStruct((t, HEAD_PAD), F32),
        ],
        compiler_params=_cparams("parallel"),
        name="mla_pre",
    )(y, sh, sc, g1, w["w_in"], w["g_q_a"], w["g_kv_a"], w["w_uq"], w["g_q"], w["w_uk"], w["g_k"],
      cos, sa, sb)


ATT_TQ = 256
ATT_TK = 256


def _attn_prompt_kernel(q_ref, k_ref, ckv_ref, o_ref, m_ref, acc_ref):
    i = pl.program_id(1)
    m_ref[...] = jnp.full_like(m_ref, NEG_BIG)
    acc_ref[...] = jnp.zeros_like(acc_ref)
    ones = jnp.ones((ATT_TK, KV_LORA_RANK), BF16)

    def block(j, diagonal):
        off = pl.multiple_of(j * ATT_TK, ATT_TK)
        cb = jnp.concatenate([ckv_ref[pl.ds(off, ATT_TK), :].astype(BF16), ones], axis=1)
        for hd in range(N_HEADS):
            sl = slice(hd * HEAD_PAD, (hd + 1) * HEAD_PAD)
            s = _dot_nt(q_ref[:, sl], k_ref[pl.ds(off, ATT_TK), sl]) * SM_SCALE
            if diagonal:
                row = lax.broadcasted_iota(jnp.int32, (ATT_TQ, ATT_TK), 0)
                col = lax.broadcasted_iota(jnp.int32, (ATT_TQ, ATT_TK), 1)
                s = jnp.where(col <= row, s, NEG_BIG)
            m_old = m_ref[hd]
            s_max = jnp.max(jnp.maximum(s[:, :128], s[:, 128:]), axis=-1, keepdims=True)
            m_new = jnp.maximum(m_old, jnp.broadcast_to(s_max, m_old.shape))
            p = jnp.concatenate([jnp.exp(s[:, :128] - m_new), jnp.exp(s[:, 128:] - m_new)], axis=1)
            alpha = jnp.exp(m_old - m_new)
            pv = _dot(p.astype(BF16), cb)
            acc_ref[hd] = jnp.concatenate([alpha, alpha], axis=1) * acc_ref[hd] + pv
            m_ref[hd] = m_new

    def body(j, carry):
        block(j, False)
        return carry

    lax.fori_loop(0, i, body, 0)
    block(i, True)
    for hd in range(N_HEADS):
        acc = acc_ref[hd]
        o_ref[:, hd * HEAD_PAD:(hd + 1) * HEAD_PAD] = (
            acc[:, :KV_LORA_RANK] / acc[:, KV_LORA_RANK:]).astype(o_ref.dtype)


def _attn_prompt(q, k, ckv):
    nq = SEQ // ATT_TQ
    wide = N_HEADS * HEAD_PAD
    return pl.pallas_call(
        _attn_prompt_kernel,
        grid=(BATCH, nq),
        in_specs=[
            pl.BlockSpec((ATT_TQ, wide), lambda b, i: (b * nq + i, 0)),
            pl.BlockSpec((SEQ, wide), lambda b, i: (b, 0)),
            pl.BlockSpec((SEQ, KV_LORA_RANK), lambda b, i: (b, 0)),
        ],
        out_specs=pl.BlockSpec((ATT_TQ, wide), lambda b, i: (b * nq + i, 0)),
        out_shape=jax.ShapeDtypeStruct((BATCH * SEQ, wide), BF16),
        scratch_shapes=[pltpu.VMEM((N_HEADS, ATT_TQ, 128), F32),
                        pltpu.VMEM((N_HEADS, ATT_TQ, 2 * KV_LORA_RANK), F32)],
        compiler_params=_cparams("parallel", "parallel"),
        name="attn_prompt",
    )(q, k, ckv)


SA_PAGES = 32
SA_POS = SA_PAGES * PAGE_SIZE
SA_STEPS = N_PAGES // SA_PAGES


def _absorb_kernel(q_ref, gk_ref, wukt_ref, qa_ref, qr_ref):
    gk = gk_ref[...]
    for hd in range(N_HEADS):
        qh = q_ref[:, hd * HEAD_PAD:(hd + 1) * HEAD_PAD]
        qa_ref[hd] = _dot((qh * gk).astype(BF16), wukt_ref[hd])
        qr_ref[hd] = pltpu.roll(qh, HEAD_PAD - QK_NOPE_DIM, 1)


def _absorb(q_s, gk_nope, wukt):
    b = q_s.shape[0]
    shp = jax.ShapeDtypeStruct((N_HEADS, b, HEAD_PAD), F32)
    return pl.pallas_call(
        _absorb_kernel,
        in_specs=[_full(q_s.shape), _full(gk_nope.shape), _full(wukt.shape)],
        out_specs=[_full(shp.shape), _full(shp.shape)],
        out_shape=[shp, shp],
        grid=(1,),
        compiler_params=_cparams("arbitrary"),
        name="absorb_q",
    )(q_s, gk_nope, wukt)


def _attn_sample_kernel(pt_ref, kv_hbm, kr_hbm, wukt_ref, qa_ref, qr_ref, qs_ref, knew_ref, cnew_ref,
                        gkr_ref, cos_ref, sin_ref, o_ref, kv_buf, kr_buf, sems, m_ref, l_ref, acc_ref, ssq_ref):
    s, c = pl.program_id(0), pl.program_id(1)
    g = s * SA_STEPS + c
    last = pl.num_programs(0) * SA_STEPS - 1

    def page_copies(step, slot):
        seq, chunk = step // SA_STEPS, step % SA_STEPS
        for p in range(SA_PAGES):
            page = pt_ref[seq, chunk * SA_PAGES + p]
            yield pltpu.make_async_copy(kv_hbm.at[page], kv_buf.at[slot, p], sems.at[slot, 0])
            yield pltpu.make_async_copy(kr_hbm.at[page], kr_buf.at[slot, :, pl.ds(p * PAGE_SIZE, PAGE_SIZE)],
                                        sems.at[slot, 1])

    @pl.when(g == 0)
    def _():
        for cp in page_copies(g, 0):
            cp.start()

    @pl.when(g < last)
    def _():
        for cp in page_copies(g + 1, (g + 1) % 2):
            cp.start()

    slot = g % 2
    for cp in page_copies(g, slot):
        cp.wait()

    @pl.when(c == 0)
    def _():
        m_ref[...] = jnp.full_like(m_ref, NEG_BIG)
        l_ref[...] = jnp.zeros_like(l_ref)
        acc_ref[...] = jnp.zeros_like(acc_ref)

    cc = kv_buf[slot].reshape(SA_POS, KV_LORA_RANK).astype(BF16)
    qa16 = jnp.concatenate([qa_ref[...], jnp.zeros((8, KV_LORA_RANK), F32)], axis=0).astype(BF16)
    lhs = jnp.concatenate([wukt_ref[...], qa16], axis=0)
    res = _dot_nt(lhs, cc)
    for hd in range(N_HEADS):
        x = res[hd * QK_NOPE_DIM:(hd + 1) * QK_NOPE_DIM]
        ssq_ref[hd:hd + 1, :] = jnp.sum(x * x, axis=0, keepdims=True)
    sn = res[N_HEADS * QK_NOPE_DIM:N_HEADS * QK_NOPE_DIM + N_HEADS]

    krt = kr_buf[slot]
    kr2 = jnp.sum(krt * krt, axis=0, keepdims=True)
    krg = krt * gkr_ref[...]
    x1, x2 = krg[:ROPE_HALF], krg[ROPE_HALF:]
    cos, sin = cos_ref[...], sin_ref[...]
    rk = jnp.concatenate([x1 * cos - x2 * sin, x2 * cos + x1 * sin], axis=0).astype(BF16)
    sr = _dot(qr_ref[...][:, :QK_ROPE_DIM].astype(BF16), rk)

    rinv = lax.rsqrt((ssq_ref[...] + kr2) / QK_DIM + RMS_EPS)
    s = (sn + sr) * rinv * SM_SCALE
    m_old = m_ref[...]
    m_new = jnp.maximum(m_old, jnp.max(s, axis=-1, keepdims=True))
    p = jnp.exp(s - m_new)
    alpha = jnp.exp(m_old - m_new)
    l_ref[...] = alpha * l_ref[...] + jnp.sum(p, axis=-1, keepdims=True)
    acc_ref[...] = alpha * acc_ref[...] + _dot(p.astype(BF16), cc)
    m_ref[...] = m_new

    @pl.when(c == SA_STEPS - 1)
    def _():
        s_new = jnp.sum(qs_ref[...] * knew_ref[...], axis=-1, keepdims=True) * SM_SCALE
        m_o = m_ref[...]
        m_n = jnp.maximum(m_o, s_new)
        p_new = jnp.exp(s_new - m_n)
        a = jnp.exp(m_o - m_n)
        l = a * l_ref[...] + p_new
        acc = a * acc_ref[...] + p_new * cnew_ref[...]
        o_ref[...] = acc / l


def _attn_sample(page_table, cache_kv, cache_kr, wukt_flat, qa, qr, qs, knew, cnew, gkr, cos_t, sin_t):
    b = qa.shape[0]

    per_seq = pl.BlockSpec((None, N_HEADS, HEAD_PAD), lambda s, c, pt: (s, 0, 0))
    in_specs = [
        pl.BlockSpec(memory_space=pl.ANY), pl.BlockSpec(memory_space=pl.ANY),
        pl.BlockSpec(wukt_flat.shape, lambda s, c, pt: (0, 0)),
        per_seq, per_seq, per_seq, per_seq,
        pl.BlockSpec((None, 1, KV_LORA_RANK), lambda s, c, pt: (s, 0, 0)),
        pl.BlockSpec(gkr.shape, lambda s, c, pt: (0, 0)),
        pl.BlockSpec((ROPE_HALF, SA_POS), lambda s, c, pt: (0, c)),
        pl.BlockSpec((ROPE_HALF, SA_POS), lambda s, c, pt: (0, c)),
    ]
    grid_spec = pltpu.PrefetchScalarGridSpec(
        num_scalar_prefetch=1,
        grid=(b, SA_STEPS),
        in_specs=in_specs,
        out_specs=pl.BlockSpec((None, N_HEADS, KV_LORA_RANK), lambda s, c, pt: (s, 0, 0)),
        scratch_shapes=[
            pltpu.VMEM((2, SA_PAGES, PAGE_SIZE, KV_LORA_RANK), F32),
            pltpu.VMEM((2, QK_ROPE_DIM, SA_POS), F32),
            pltpu.SemaphoreType.DMA((2, 2)),
            pltpu.VMEM((N_HEADS, 1), F32), pltpu.VMEM((N_HEADS, 1), F32),
            pltpu.VMEM((N_HEADS, KV_LORA_RANK), F32), pltpu.VMEM((N_HEADS, SA_POS), F32),
        ],
    )
    return pl.pallas_call(
        _attn_sample_kernel,
        grid_spec=grid_spec,
        out_shape=jax.ShapeDtypeStruct((b, N_HEADS, KV_LORA_RANK), F32),
        compiler_params=_cparams("arbitrary", "arbitrary"),
        name="attn_sample",
    )(page_table, cache_kv, cache_kr, wukt_flat, qa, qr, qs, knew, cnew, gkr, cos_t, sin_t)


def _mla_post_kernel(ol_ref, wuv_ref, wo_ref, y_ref, gate_ref, o_ref):
    parts = [_dot(ol_ref[:, hd * KV_LORA_RANK:(hd + 1) * KV_LORA_RANK], wuv_ref[hd])
             for hd in range(N_HEADS)]
    o = jnp.concatenate(parts, axis=-1).astype(BF16)
    o_ref[...] = y_ref[...] + gate_ref[...] * _dot(o, wo_ref[...])


def _mla_post(o_lat, wuv, wo, y, gate, tm, tiles_per_seq):
    t = y.shape[0]
    tok = lambda n: pl.BlockSpec((tm, n), lambda i: (i, 0))
    return pl.pallas_call(
        _mla_post_kernel,
        grid=(t // tm,),
        in_specs=[tok(N_HEADS * KV_LORA_RANK), _full(wuv.shape), _full(wo.shape), tok(D_MODEL),
                  _mod_spec(gate, tiles_per_seq)],
        out_specs=tok(D_MODEL),
        out_shape=jax.ShapeDtypeStruct((t, D_MODEL), F32),
        compiler_params=_cparams("parallel"),
        name="mla_post",
    )(o_lat, wuv, wo, y, gate)


def _sgu_kernel(y_ref, sh_ref, sc_ref, gate_ref, g1_ref, win_ref, bin_ref, gv_ref, sp_a_ref, sp_b_ref,
                wout_ref, o_ref, v_ref, *, tm, spatial):
    y = y_ref[...]
    h = _rms(y) * g1_ref[...]
    h = h * (1.0 + sc_ref[...]) + sh_ref[...]
    z = _gelu(_dot(h.astype(BF16), win_ref[...]) + bin_ref[...])
    u = z[:, :SG_HALF]
    v = _rms(z[:, SG_HALF:]) * gv_ref[...]
    if spatial:
        r = lax.broadcasted_iota(jnp.int32, (CHUNK, CHUNK), 0)
        cidx = lax.broadcasted_iota(jnp.int32, (CHUNK, CHUNK), 1)
        vb = v.astype(BF16)
        cols = []
        for g in range(SG_GROUPS):
            wc = jnp.where(cidx <= r, sp_a_ref[g], 0.0).astype(BF16)
            rows = [_dot(wc, vb[c * CHUNK:(c + 1) * CHUNK, g * CHUNK:(g + 1) * CHUNK]) + sp_b_ref[g]
                    for c in range(tm // CHUNK)]
            cols.append(jnp.concatenate(rows, axis=0))
        s = jnp.concatenate(cols, axis=-1)
        v_ref[...] = v[tm - CHUNK:, :]
    else:
        s = v * sp_a_ref[...] + sp_b_ref[...]
        v_ref[...] = v
    o = _dot((u * s).astype(BF16), wout_ref[...])
    o_ref[...] = y + gate_ref[...] * o


def _sgu(y, sh, sc, gate, g1, w, sp_a, sp_b, tm, tiles_per_seq, spatial):
    t = y.shape[0]
    tok = lambda n: pl.BlockSpec((tm, n), lambda i: (i, 0))
    if spatial:
        n_seq = t // (tm * tiles_per_seq)
        v_spec = pl.BlockSpec((None, CHUNK, SG_HALF), lambda i: (i // tiles_per_seq, 0, 0))
        v_shape = jax.ShapeDtypeStruct((n_seq, CHUNK, SG_HALF), F32)
    else:
        v_spec = tok(SG_HALF)
        v_shape = jax.ShapeDtypeStruct((t, SG_HALF), F32)
    return pl.pallas_call(
        functools.partial(_sgu_kernel, tm=tm, spatial=spatial),
        grid=(t // tm,),
        in_specs=[tok(D_MODEL), _mod_spec(sh, tiles_per_seq), _mod_spec(sc, tiles_per_seq),
                  _mod_spec(gate, tiles_per_seq), _full((1, D_MODEL)), _full(w["w_in"].shape),
                  _full((1, 2 * SG_HALF)), _full((1, SG_HALF)), _full(sp_a.shape), _full(sp_b.shape),
                  _full(w["w_out"].shape)],
        out_specs=[tok(D_MODEL), v_spec],
        out_shape=[jax.ShapeDtypeStruct((t, D_MODEL), F32), v_shape],
        compiler_params=_cparams("arbitrary"),
        name="sgu",
    )(y, sh, sc, gate, g1, w["w_in"], w["b_in"], w["g_v"], sp_a, sp_b, w["w_out"])


N_TOP = PEER_TOPK + 1
TOP_ROWS = 24
N_CAND = TOP_ROWS + 7 * 8 + (TOP_ROWS - 8)


def _sorting_network(n):
    def merge(lo, hi, r):
        step = r * 2
        if step < hi - lo:
            yield from merge(lo, hi, step)
            yield from merge(lo + r, hi, step)
            yield from [(i, i + r) for i in range(lo + r, hi - r, step)]
        else:
            yield (lo, lo + r)

    def sort(lo, hi):
        if hi - lo >= 1:
            mid = lo + (hi - lo) // 2
            yield from sort(lo, mid)
            yield from sort(mid + 1, hi)
            yield from merge(lo, hi, 1)

    return list(sort(0, n - 1))


def _top_values(x, n, emit):
    nb = x.shape[0] // 8
    v = [x[8 * k:8 * k + 8] for k in range(nb)]
    for i, j in _sorting_network(16):
        if j < nb:
            v[i], v[j] = jnp.maximum(v[i], v[j]), jnp.minimum(v[i], v[j])
    for r in range(n):
        head = jnp.max(v[0], axis=0, keepdims=True)
        emit(r, head)
        hit = v[0] == head
        for k in range(min(nb, n - 1 - r)):
            below = v[k + 1] if k + 1 < nb else NEG_BIG
            v[k] = jnp.where(hit, below, v[k])


def _pair_words(x):
    bits = pltpu.bitcast(x.astype(BF16).astype(F32), jnp.uint32)
    return bits | (bits >> 16)


def _peer_pre_kernel(y_ref, sh_ref, sc_ref, g2_ref, wq_ref, sk_ref, ht_ref, rank_ref, eb_ref, cnt_ref,
                     cc_ref, s_scr, t_scr, cand_scr, tv_scr):
    h = _rms(y_ref[...]) * g2_ref[...]
    h = h * (1.0 + sc_ref[...]) + sh_ref[...]
    ht_ref[...] = h.T.astype(BF16)
    q = _dot(h.astype(BF16), wq_ref[...])
    for hd in range(PEER_HEADS):
        for side in range(2):
            o = (hd * 2 + side) * N_KEYS
            s_scr[side, hd] = _dot_nt(sk_ref[side], q[:, o:o + N_KEYS].astype(BF16))
    t_scr[...] = jnp.full_like(t_scr, NEG_BIG)

    def tops(idx, _):
        side, hd = idx // PEER_HEADS, idx % PEER_HEADS

        def emit(r, row):
            t_scr[side, hd, r:r + 1, :] = row

        _top_values(s_scr[side, hd], N_TOP, emit)
        return 0

    lax.fori_loop(0, 2 * PEER_HEADS, tops, 0)

    def finish(hd, _):
        t1, t2 = t_scr[0, hd], t_scr[1, hd]
        cand_scr[0:TOP_ROWS] = t1[0:1] + t2
        for a in range(1, 8):
            cand_scr[TOP_ROWS + (a - 1) * 8:TOP_ROWS + a * 8] = t1[a:a + 1] + t2[0:8]
        cand_scr[TOP_ROWS + 56:] = t1[8:] + t2[0:1]
        cand = cand_scr[...]

        def emit(r, row):
            tv_scr[r:r + 1, :] = row

        _top_values(cand, N_TOP, emit)
        tau = 0.5 * (tv_scr[PEER_TOPK - 1:PEER_TOPK, :] + tv_scr[PEER_TOPK:PEER_TOPK + 1, :])
        top = t1[:1] + t2[:1]
        z = jnp.sum(jnp.where(cand >= tau, jnp.exp(cand - top), 0.0), axis=0, keepdims=True)
        s1, s2 = s_scr[0, hd], s_scr[1, hd]
        eb = jnp.exp(s2 - t2[:1])
        th = tau - s1
        cc = jnp.exp(s1 - t1[:1]) * (0.5 / z)
        rank = jnp.zeros_like(s2)
        cnt = jnp.zeros_like(s1)
        for b in range(PEER_TOPK):
            tb = t2[b:b + 1]
            rank = jnp.where(tb > s2, b + 1.0, rank)
            cnt = jnp.where(tb >= th, b + 1.0, cnt)
        cntw = _pair_words(cnt)
        ccw = _pair_words(cc)
        for col in range(s1.shape[1] // 128):
            cs = slice(col * 128, (col + 1) * 128)
            rank_ref[hd, col] = pltpu.bitcast(rank[:, cs].astype(BF16), jnp.uint32)
            eb_ref[hd, col] = pltpu.bitcast(eb[:, cs].astype(BF16), jnp.uint32)
            cnt_ref[hd, col] = cntw[:, cs]
            cc_ref[hd, col] = ccw[:, cs]
        return 0

    lax.fori_loop(0, PEER_HEADS, finish, 0)


def _peer_pre(y, sh, sc, g2, wq, sk, tm, tiles_per_seq):
    t = y.shape[0]
    hk = pl.BlockSpec((PEER_HEADS, tm // 128, N_KEYS, 128), lambda i: (0, i, 0, 0))
    hp = pl.BlockSpec((PEER_HEADS, tm // 128, N_KEYS // 2, 128), lambda i: (0, i, 0, 0))
    packed = jax.ShapeDtypeStruct((PEER_HEADS, t // 128, N_KEYS // 2, 128), jnp.uint32)
    words = jax.ShapeDtypeStruct((PEER_HEADS, t // 128, N_KEYS, 128), jnp.uint32)
    return pl.pallas_call(
        _peer_pre_kernel,
        grid=(t // tm,),
        in_specs=[pl.BlockSpec((tm, D_MODEL), lambda i: (i, 0)), _mod_spec(sh, tiles_per_seq),
                  _mod_spec(sc, tiles_per_seq), _full((1, D_MODEL)), _full(wq.shape), _full(sk.shape)],
        out_specs=[pl.BlockSpec((D_MODEL, tm), lambda i: (0, i)), hp, hp, hk, hk],
        out_shape=[jax.ShapeDtypeStruct((D_MODEL, t), BF16), packed, packed, words, words],
        scratch_shapes=[pltpu.VMEM((2, PEER_HEADS, N_KEYS, tm), F32),
                        pltpu.VMEM((2, PEER_HEADS, TOP_ROWS, tm), F32),
                        pltpu.VMEM((N_CAND, tm), F32),
                        pltpu.VMEM((TOP_ROWS, tm), F32)],
        compiler_params=_cparams("parallel"),
        name="peer_pre",
    )(y, sh, sc, g2, wq, sk)


PEER_EB = 1024
PEER_ROWS = PEER_EB // N_KEYS
PEER_BLOCKS = N_EXPERTS // PEER_EB


BF16_ROWS = 16


def _selection_weights(rank_ref, eb_ref, cnt_ref, cc_ref, col):
    def row_of(ref, hd, ii):
        word = jnp.broadcast_to(ref[hd, col, ii:ii + 1, :], (8, 128))
        tile = pltpu.bitcast(word, BF16)
        return jnp.concatenate([tile] * (N_KEYS // BF16_ROWS), axis=0)

    rows = []
    for ii in range(PEER_ROWS):
        w = None
        for hd in range(PEER_HEADS):
            eb = pltpu.bitcast(eb_ref[hd, col], BF16)
            rank = pltpu.bitcast(rank_ref[hd, col], BF16)
            t = jnp.where(rank < row_of(cnt_ref, hd, ii), eb, jnp.zeros_like(eb))
            t = t * row_of(cc_ref, hd, ii)
            w = t if w is None else w + t
        rows.append(w)
    return jnp.concatenate(rows, axis=0)


def _peer_dense_kernel(ht_ref, u_ref, vt_ref, rank_ref, eb_ref, cnt_ref, cc_ref, y_ref, gate_ref, o_ref,
                       acc_ref, *, tm):
    e = pl.program_id(1)

    @pl.when(e == 0)
    def _():
        acc_ref[...] = jnp.zeros_like(acc_ref)

    act = _gelu_twice(_dot(u_ref[...], ht_ref[...]))
    w = jnp.concatenate([_selection_weights(rank_ref, eb_ref, cnt_ref, cc_ref, col)
                         for col in range(tm // 128)], axis=1)
    acc_ref[...] += _dot(vt_ref[...], act.astype(BF16) * w)

    @pl.when(e == PEER_BLOCKS - 1)
    def _():
        o_ref[...] = y_ref[...] + gate_ref[...] * acc_ref[...].T


def _peer_dense(ht, u_b, vt_b, layer, rank, eb, cnt, cc, y, gate, tm, tiles_per_seq):
    t = y.shape[0]
    cols = tm // 128
    hk = pl.BlockSpec((PEER_HEADS, cols, N_KEYS // 2, 128), lambda i, e: (0, i, 0, 0))
    hr = pl.BlockSpec((PEER_HEADS, cols, PEER_ROWS, 128), lambda i, e: (0, i, e, 0))
    return pl.pallas_call(
        functools.partial(_peer_dense_kernel, tm=tm),
        grid=(t // tm, PEER_BLOCKS),
        in_specs=[pl.BlockSpec((D_MODEL, tm), lambda i, e: (0, i)),
                  pl.BlockSpec((None, PEER_EB, D_MODEL), lambda i, e: (layer, e, 0)),
                  pl.BlockSpec((None, D_MODEL, PEER_EB), lambda i, e: (layer, 0, e)),
                  hk, hk, hr, hr,
                  pl.BlockSpec((tm, D_MODEL), lambda i, e: (i, 0)),
                  pl.BlockSpec((None,) + gate.shape[1:], lambda i, e: (i // tiles_per_seq, 0, 0))],
        out_specs=pl.BlockSpec((tm, D_MODEL), lambda i, e: (i, 0)),
        out_shape=jax.ShapeDtypeStruct((t, D_MODEL), F32),
        scratch_shapes=[pltpu.VMEM((D_MODEL, tm), F32)],
        compiler_params=_cparams("parallel", "arbitrary"),
        name="peer_dense",
    )(ht, u_b, vt_b, rank, eb, cnt, cc, y, gate)


def _peer(y, sh, sc, gate, g2, wq, sk, u_b, vt_b, layer, tm, tiles_per_seq, dense_tiles=1):
    ht, rank, eb, cnt, cc = _peer_pre(y, sh, sc, g2, wq, sk, tm, tiles_per_seq)
    return _peer_dense(ht, u_b, vt_b, layer, rank, eb, cnt, cc, y, gate, tm * dense_tiles,
                       tiles_per_seq // dense_tiles)


def _pad_heads(w, used):
    w = jnp.pad(w, [(0, 0)] * (w.ndim - 1) + [(0, HEAD_PAD - used)])
    return w.reshape(w.shape[:-2] + (N_HEADS * HEAD_PAD,))


def _pad_gain(g):
    return jnp.pad(g, (0, HEAD_PAD - QK_DIM)).reshape(1, HEAD_PAD)


def _rope_tables(pos):
    inv_freq = ROPE_THETA ** (-jnp.arange(ROPE_HALF, dtype=F32) * 2.0 / QK_ROPE_DIM)
    ang = pos.astype(F32)[:, None] * inv_freq[None, :]
    cos, sin = jnp.cos(ang), jnp.sin(ang)
    t = pos.shape[0]
    one = jnp.ones((t, QK_NOPE_DIM), F32)
    z64 = jnp.zeros((t, QK_NOPE_DIM), F32)
    z16 = jnp.zeros((t, ROPE_HALF), F32)
    z32 = jnp.zeros((t, HEAD_PAD - QK_DIM), F32)
    cos_t = jnp.concatenate([one, cos, cos, z32], axis=1)
    sa = jnp.concatenate([z64, -sin, z16, z32], axis=1)
    sb = jnp.concatenate([z64, z16, sin, z32], axis=1)
    return cos_t, sa, sb, cos, sin


def _mla_weights(w_in, g_q_a, g_kv_a, w_uq, g_q, w_uk):
    lo = Q_LORA_RANK + KV_LORA_RANK
    w_in_p = jnp.concatenate(
        [w_in[:, :lo], jnp.zeros((D_MODEL, QK_NOPE_DIM), F32), w_in[:, lo:],
         jnp.zeros((D_MODEL, HEAD_PAD - QK_DIM), F32)], axis=1).astype(BF16)
    return {
        "w_in": w_in_p,
        "g_q_a": g_q_a.reshape(1, -1),
        "g_kv_a": g_kv_a.reshape(1, -1),
        "w_uq": _pad_heads(w_uq.reshape(Q_LORA_RANK, N_HEADS, QK_DIM), QK_DIM).astype(BF16),
        "g_q": _pad_gain(g_q),
        "w_uk": _pad_heads(w_uk, QK_NOPE_DIM).astype(BF16),
    }


def _split_mod(m, n_prompt):
    m = m.reshape(m.shape[0], 6, D_MODEL)
    mp = [m[:n_prompt, j].reshape(n_prompt, 1, D_MODEL) for j in range(6)]
    ms = [m[n_prompt:, j].reshape(1, -1, D_MODEL) for j in range(6)]
    return mp, ms


def kernel(x_prompt, x_sample, c_prompt, c_sample, cache_kv_latent, cache_k_rope, page_table, norm1_g, norm2_g, w_mod, b_mod, mla_w_in, mla_g_q_a, mla_g_kv_a, mla_w_uq, mla_g_q, mla_w_uk, mla_w_uv, mla_g_k, mla_w_o, sg_w_in, sg_b_in, sg_g_v, sg_w_s, sg_b_s, sg_w_out, peer_w_q, peer_sub_keys, peer_u, peer_v):
    yp = x_prompt.reshape(BATCH * SEQ, D_MODEL)
    ys = x_sample.reshape(DEC_BATCH, D_MODEL)
    mods = _modulation(jnp.concatenate([c_prompt, c_sample], axis=0), w_mod, b_mod)

    tm_p, tps_p = 512, SEQ // 512
    tm_s, tps_s = DEC_BATCH, 1

    u_b = peer_u.astype(BF16)
    vt_b = jnp.swapaxes(peer_v, 1, 2).astype(BF16)

    outs = {}
    for layer in range(DEPTH):
        mp, ms = _split_mod(mods[layer], BATCH)
        g1 = norm1_g[layer].reshape(1, D_MODEL)
        g2 = norm2_g[layer].reshape(1, D_MODEL)
        if layer % 2 == 0:
            a = layer // 2
            w = _mla_weights(mla_w_in[a], mla_g_q_a[a], mla_g_kv_a[a], mla_w_uq[a], mla_g_q[a], mla_w_uk[a])
            gk = _pad_gain(mla_g_k[a])
            w["g_k"] = gk
            wuv = jnp.transpose(mla_w_uv[a], (1, 0, 2)).astype(BF16)
            wo = mla_w_o[a].astype(BF16)
            cos_p, sa_p, sb_p, _, _ = _rope_tables(jnp.arange(SEQ))
            cos_s, sa_s, sb_s, _, _ = _rope_tables(jnp.full((1,), PAST_LEN))
            _, _, _, cos_c, sin_c = _rope_tables(jnp.arange(PAST_LEN))

            q, k, ckv_p, krp_p = _mla_pre(yp, mp[0], mp[1], g1, w, (cos_p, sa_p, sb_p), tm_p, tps_p, True, BF16)
            o_lat = _attn_prompt(q, k, ckv_p)
            yp = _mla_post(o_lat, wuv, wo, yp, mp[2], tm_p, tps_p)

            q_s, k_s, ckv_s, krp_s = _mla_pre(ys, ms[0], ms[1], g1, w, (cos_s, sa_s, sb_s), tm_s, tps_s, False, F32)
            wukt = jnp.pad(jnp.transpose(mla_w_uk[a], (1, 2, 0)),
                           ((0, 0), (0, HEAD_PAD - QK_NOPE_DIM), (0, 0))).astype(BF16)
            gk_nope = gk * (jnp.arange(HEAD_PAD) < QK_NOPE_DIM)[None, :]
            qa, qr = (jnp.transpose(x, (1, 0, 2)) for x in _absorb(q_s, gk_nope, wukt))
            wukt_flat = jnp.transpose(mla_w_uk[a], (1, 2, 0)).reshape(N_HEADS * QK_NOPE_DIM, KV_LORA_RANK).astype(BF16)
            gkr = mla_g_k[a][QK_NOPE_DIM:].reshape(QK_ROPE_DIM, 1)
            o_lat_s = _attn_sample(
                page_table, cache_kv_latent[a], jnp.swapaxes(cache_k_rope[a], 1, 2), wukt_flat, qa, qr,
                q_s.reshape(DEC_BATCH, N_HEADS, HEAD_PAD), k_s.reshape(DEC_BATCH, N_HEADS, HEAD_PAD),
                ckv_s.reshape(DEC_BATCH, 1, KV_LORA_RANK), gkr, cos_c.T, sin_c.T)
            ys = _mla_post(o_lat_s.reshape(DEC_BATCH, N_HEADS * KV_LORA_RANK).astype(BF16), wuv, wo, ys,
                           ms[2], tm_s, tps_s)

            outs.setdefault("kv_p", []).append(ckv_p.reshape(BATCH, SEQ, KV_LORA_RANK))
            outs.setdefault("kr_p", []).append(krp_p[:, QK_NOPE_DIM:QK_DIM].reshape(BATCH, SEQ, QK_ROPE_DIM))
            outs.setdefault("kv_s", []).append(ckv_s.reshape(DEC_BATCH, 1, KV_LORA_RANK))
            outs.setdefault("kr_s", []).append(krp_s[:, QK_NOPE_DIM:QK_DIM].reshape(DEC_BATCH, 1, QK_ROPE_DIM))
        else:
            bidx = layer // 2
            w = {"w_in": sg_w_in[bidx].astype(BF16), "b_in": sg_b_in[bidx].reshape(1, -1),
                 "g_v": sg_g_v[bidx].reshape(1, -1), "w_out": sg_w_out[bidx].astype(BF16)}
            sp_a = sg_w_s[bidx]
            sp_b = sg_b_s[bidx].reshape(SG_GROUPS, CHUNK, 1)
            yp, vp = _sgu(yp, mp[0], mp[1], mp[2], g1, w, sp_a, sp_b, tm_p, tps_p, True)
            coef = jnp.repeat(sg_w_s[bidx][:, 0, 0], CHUNK).reshape(1, SG_HALF)
            bias = jnp.repeat(sg_b_s[bidx][:, 0], CHUNK).reshape(1, SG_HALF)
            ys, vs = _sgu(ys, ms[0], ms[1], ms[2], g1, w, coef, bias, tm_s, tps_s, False)
            outs.setdefault("v_p", []).append(vp)
            outs.setdefault("v_s", []).append(vs.reshape(DEC_BATCH, 1, SG_HALF))

        wq = peer_w_q[layer].astype(BF16)
        sk = peer_sub_keys[layer].astype(BF16)
        yp = _peer(yp, mp[3], mp[4], mp[5], g2, wq, sk, u_b, vt_b, layer, tm_p, tps_p, dense_tiles=2)
        ys = _peer(ys, ms[3], ms[4], ms[5], g2, wq, sk, u_b, vt_b, layer, tm_s, tps_s)

    return (yp.reshape(BATCH, SEQ, D_MODEL), ys.reshape(DEC_BATCH, 1, D_MODEL),
            jnp.stack(outs["kv_p"]), jnp.stack(outs["kr_p"]), jnp.stack(outs["kv_s"]), jnp.stack(outs["kr_s"]),
            jnp.stack(outs["v_p"]), jnp.stack(outs["v_s"]))
```

```python
import functools

import jax
import jax.numpy as jnp
from jax import lax
from jax.experimental import pallas as pl
from jax.experimental.pallas import tpu as pltpu

F32 = jnp.float32
BF16 = jnp.bfloat16

D_MODEL = 1024
BATCH = 8
SEQ = 2048
DEPTH = 2
DEC_BATCH = 128
PAST_LEN = 16384
PAGE_SIZE = 128
N_PAGES = PAST_LEN // PAGE_SIZE

N_HEADS = 8
Q_LORA_RANK = 256
KV_LORA_RANK = 128
QK_NOPE_DIM = 64
QK_ROPE_DIM = 32
QK_DIM = QK_NOPE_DIM + QK_ROPE_DIM
V_HEAD_DIM = 128
ROPE_THETA = 10000.0
SM_SCALE = QK_DIM ** -0.5
HEAD_PAD = 128
ROPE_HALF = QK_ROPE_DIM // 2

CHUNK = 128
SG_HALF = 1536
SG_GROUPS = 12

PEER_HEADS = 8
N_KEYS = 128
N_EXPERTS = N_KEYS * N_KEYS
PEER_TOPK = 16
RMS_EPS = 1e-6

NEG_BIG = -1e30
VMEM_LIMIT = 56 * 1024 * 1024


def _cparams(*sem):
    return pltpu.CompilerParams(dimension_semantics=sem, vmem_limit_bytes=VMEM_LIMIT)


def _rms(x):
    return x * lax.rsqrt(jnp.mean(x * x, axis=-1, keepdims=True) + RMS_EPS)


def _gelu(x):
    cdf = 0.5 * (1.0 + jnp.tanh(0.7978845608028654 * (x + 0.044715 * (x * x * x))))
    return x * cdf


def _gelu_twice(x):
    c = 0.7978845608028654
    return x * (1.0 + jnp.tanh(x * (c + (0.044715 * c) * (x * x))))


def _dot(a, b):
    return jnp.dot(a, b, preferred_element_type=F32)


def _dot_nt(a, b):
    return lax.dot_general(a, b, (((1,), (1,)), ((), ())), preferred_element_type=F32)


def _full(shape):
    n = len(shape)
    return pl.BlockSpec(shape, lambda *_: (0,) * n)


def _mod_spec(arr, tiles_per_seq):
    return pl.BlockSpec((None,) + arr.shape[1:], lambda t, *_: (t // tiles_per_seq, 0, 0))


def _mod_kernel(c_ref, w_ref, b_ref, o_ref):
    c = c_ref[...]
    a = (c * jax.nn.sigmoid(c)).astype(BF16)
    o_ref[...] = _dot(a, w_ref[...].astype(BF16)) + b_ref[...]


def _modulation(c_all, w_mod, b_mod):
    r = c_all.shape[0]
    tn = 1536
    return pl.pallas_call(
        _mod_kernel,
        grid=(DEPTH, 6 * D_MODEL // tn),
        in_specs=[
            pl.BlockSpec((r, D_MODEL), lambda l, n: (0, 0)),
            pl.BlockSpec((None, D_MODEL, tn), lambda l, n: (l, 0, n)),
            pl.BlockSpec((None, 1, tn), lambda l, n: (l, 0, n)),
        ],
        out_specs=pl.BlockSpec((None, r, tn), lambda l, n: (l, 0, n)),
        out_shape=jax.ShapeDtypeStruct((DEPTH, r, 6 * D_MODEL), F32),
        compiler_params=_cparams("parallel", "parallel"),
        name="modulation",
    )(c_all, w_mod, b_mod.reshape(DEPTH, 1, 6 * D_MODEL))


def _rope(x, cos, sa, sb):
    return x * cos + pltpu.roll(x, HEAD_PAD - ROPE_HALF, 1) * sa + pltpu.roll(x, ROPE_HALF, 1) * sb


def _mla_pre_kernel(y_ref, sh_ref, sc_ref, g1_ref, win_ref, gqa_ref, gkva_ref, wuq_ref, gq_ref,
                    wuk_ref, gk_ref, cos_ref, sa_ref, sb_ref, q_ref, k_ref, ckv_ref, kr_ref):
    h = _rms(y_ref[...]) * g1_ref[...]
    h = h * (1.0 + sc_ref[...]) + sh_ref[...]
    z = _dot(h.astype(BF16), win_ref[...])
    cq = _rms(z[:, :Q_LORA_RANK]) * gqa_ref[...]
    ckv = _rms(z[:, Q_LORA_RANK:Q_LORA_RANK + KV_LORA_RANK]) * gkva_ref[...]
    krp = z[:, Q_LORA_RANK + KV_LORA_RANK:]
    ckv_ref[...] = ckv
    kr_ref[...] = krp
    q = _dot(cq.astype(BF16), wuq_ref[...])
    kn = _dot(ckv.astype(BF16), wuk_ref[...])
    cos, sa, sb = cos_ref[...], sa_ref[...], sb_ref[...]
    gq, gk = gq_ref[...], gk_ref[...]
    ones = jnp.ones((HEAD_PAD, HEAD_PAD), BF16)

    def head_norm(x, g):
        ssq = _dot((x * x).astype(BF16), ones)
        return x * lax.rsqrt(ssq / QK_DIM + RMS_EPS) * g

    for hd in range(N_HEADS):
        sl = slice(hd * HEAD_PAD, (hd + 1) * HEAD_PAD)
        q_ref[:, sl] = _rope(head_norm(q[:, sl], gq), cos, sa, sb).astype(q_ref.dtype)
        k_ref[:, sl] = _rope(head_norm(kn[:, sl] + krp, gk), cos, sa, sb).astype(k_ref.dtype)


def _mla_pre(y, sh, sc, g1, w, rope_tabs, tm, tiles_per_seq, rope_tiled, qk_dtype):
    t = y.shape[0]
    cos, sa, sb = rope_tabs
    if rope_tiled:
        rspec = pl.BlockSpec((tm, HEAD_PAD), lambda i: (i % tiles_per_seq, 0))
    else:
        rspec = _full((1, HEAD_PAD))
    wide = N_HEADS * HEAD_PAD
    tok = lambda n: pl.BlockSpec((tm, n), lambda i: (i, 0))
    return pl.pallas_call(
        _mla_pre_kernel,
        grid=(t // tm,),
        in_specs=[
            tok(D_MODEL), _mod_spec(sh, tiles_per_seq), _mod_spec(sc, tiles_per_seq),
            _full((1, D_MODEL)), _full(w["w_in"].shape), _full((1, Q_LORA_RANK)),
            _full((1, KV_LORA_RANK)), _full(w["w_uq"].shape), _full((1, HEAD_PAD)),
            _full(w["w_uk"].shape), _full((1, HEAD_PAD)), rspec, rspec, rspec,
        ],
        out_specs=[tok(wide), tok(wide), tok(KV_LORA_RANK), tok(HEAD_PAD)],
        out_shape=[
            jax.ShapeDtypeStruct((t, wide), qk_dtype),
            jax.ShapeDtypeStruct((t, wide), qk_dtype),
            jax.ShapeDtypeStruct((t, KV_LORA_RANK), F32),
            jax.ShapeDtypeStruct((t, HEAD_PAD), F32),
        ],
        compiler_params=_cparams("parallel"),
        name="mla_pre",
    )(y, sh, sc, g1, w["w_in"], w["g_q_a"], w["g_kv_a"], w["w_uq"], w["g_q"], w["w_uk"], w["g_k"],
      cos, sa, sb)


ATT_TQ = 256
ATT_TK = 256


def _attn_prompt_kernel(q_ref, k_ref, ckv_ref, o_ref, m_ref, acc_ref):
    i = pl.program_id(1)
    m_ref[...] = jnp.full_like(m_ref, NEG_BIG)
    acc_ref[...] = jnp.zeros_like(acc_ref)
    ones = jnp.ones((ATT_TK, KV_LORA_RANK), BF16)

    def block(j, diagonal):
        off = pl.multiple_of(j * ATT_TK, ATT_TK)
        cb = jnp.concatenate([ckv_ref[pl.ds(off, ATT_TK), :].astype(BF16), ones], axis=1)
        for hd in range(N_HEADS):
            sl = slice(hd * HEAD_PAD, (hd + 1) * HEAD_PAD)
            s = _dot_nt(q_ref[:, sl], k_ref[pl.ds(off, ATT_TK), sl]) * SM_SCALE
            if diagonal:
                row = lax.broadcasted_iota(jnp.int32, (ATT_TQ, ATT_TK), 0)
                col = lax.broadcasted_iota(jnp.int32, (ATT_TQ, ATT_TK), 1)
                s = jnp.where(col <= row, s, NEG_BIG)
            m_old = m_ref[hd]
            s_max = jnp.max(jnp.maximum(s[:, :128], s[:, 128:]), axis=-1, keepdims=True)
            m_new = jnp.maximum(m_old, jnp.broadcast_to(s_max, m_old.shape))
            p = jnp.concatenate([jnp.exp(s[:, :128] - m_new), jnp.exp(s[:, 128:] - m_new)], axis=1)
            alpha = jnp.exp(m_old - m_new)
            pv = _dot(p.astype(BF16), cb)
            acc_ref[hd] = jnp.concatenate([alpha, alpha], axis=1) * acc_ref[hd] + pv
            m_ref[hd] = m_new

    def body(j, carry):
        block(j, False)
        return carry

    lax.fori_loop(0, i, body, 0)
    block(i, True)
    for hd in range(N_HEADS):
        acc = acc_ref[hd]
        o_ref[:, hd * HEAD_PAD:(hd + 1) * HEAD_PAD] = (
            acc[:, :KV_LORA_RANK] / acc[:, KV_LORA_RANK:]).astype(o_ref.dtype)


def _attn_prompt(q, k, ckv):
    nq = SEQ // ATT_TQ
    wide = N_HEADS * HEAD_PAD
    return pl.pallas_call(
        _attn_prompt_kernel,
        grid=(BATCH, nq),
        in_specs=[
            pl.BlockSpec((ATT_TQ, wide), lambda b, i: (b * nq + i, 0)),
            pl.BlockSpec((SEQ, wide), lambda b, i: (b, 0)),
            pl.BlockSpec((SEQ, KV_LORA_RANK), lambda b, i: (b, 0)),
        ],
        out_specs=pl.BlockSpec((ATT_TQ, wide), lambda b, i: (b * nq + i, 0)),
        out_shape=jax.ShapeDtypeStruct((BATCH * SEQ, wide), BF16),
        scratch_shapes=[pltpu.VMEM((N_HEADS, ATT_TQ, 128), F32),
                        pltpu.VMEM((N_HEADS, ATT_TQ, 2 * KV_LORA_RANK), F32)],
        compiler_params=_cparams("parallel", "parallel"),
        name="attn_prompt",
    )(q, k, ckv)


SA_PAGES = 64
SA_POS = SA_PAGES * PAGE_SIZE
SA_STEPS = N_PAGES // SA_PAGES
SA_PART_PAGES = 32
SA_PART_POS = SA_PART_PAGES * PAGE_SIZE


def _absorb_kernel(q_ref, gk_ref, wukt_ref, qa_ref, qr_ref):
    gk = gk_ref[...]
    for hd in range(N_HEADS):
        qh = q_ref[:, hd * HEAD_PAD:(hd + 1) * HEAD_PAD]
        qa_ref[hd] = _dot((qh * gk).astype(BF16), wukt_ref[hd])
        qr_ref[hd] = pltpu.roll(qh, HEAD_PAD - QK_NOPE_DIM, 1)


def _absorb(q_s, gk_nope, wukt):
    b = q_s.shape[0]
    shp = jax.ShapeDtypeStruct((N_HEADS, b, HEAD_PAD), F32)
    return pl.pallas_call(
        _absorb_kernel,
        in_specs=[_full(q_s.shape), _full(gk_nope.shape), _full(wukt.shape)],
        out_specs=[_full(shp.shape), _full(shp.shape)],
        out_shape=[shp, shp],
        grid=(1,),
        compiler_params=_cparams("arbitrary"),
        name="absorb_q",
    )(q_s, gk_nope, wukt)


def _attn_sample_kernel(pt_ref, kv_hbm, kr_hbm, wukt_ref, qa_ref, qr_ref, qs_ref, knew_ref, cnew_ref,
                        gkr_ref, cos_ref, sin_ref, o_ref, kv_buf, kr_buf, sems, m_ref, l_ref, acc_ref, ssq_ref):
    s, c = pl.program_id(0), pl.program_id(1)
    g = s * SA_STEPS + c
    last = pl.num_programs(0) * SA_STEPS - 1

    def page_copies(step, slot):
        seq, chunk = step // SA_STEPS, step % SA_STEPS
        for p in range(SA_PAGES):
            page = pt_ref[seq, chunk * SA_PAGES + p]
            yield pltpu.make_async_copy(kv_hbm.at[page], kv_buf.at[slot, p], sems.at[slot, 0])
            yield pltpu.make_async_copy(kr_hbm.at[page], kr_buf.at[slot, :, pl.ds(p * PAGE_SIZE, PAGE_SIZE)],
                                        sems.at[slot, 1])

    @pl.when(g == 0)
    def _():
        for cp in page_copies(g, 0):
            cp.start()

    @pl.when(g < last)
    def _():
        for cp in page_copies(g + 1, (g + 1) % 2):
            cp.start()

    slot = g % 2
    for cp in page_copies(g, slot):
        cp.wait()

    @pl.when(c == 0)
    def _():
        m_ref[...] = jnp.full_like(m_ref, NEG_BIG)
        l_ref[...] = jnp.zeros_like(l_ref)
        acc_ref[...] = jnp.zeros_like(acc_ref)

    qa16 = jnp.concatenate([qa_ref[...], jnp.zeros((8, KV_LORA_RANK), F32)], axis=0).astype(BF16)
    lhs = jnp.concatenate([wukt_ref[...], qa16], axis=0)
    for part in range(SA_PAGES // SA_PART_PAGES):
        pages = slice(part * SA_PART_PAGES, (part + 1) * SA_PART_PAGES)
        pos = slice(part * SA_PART_POS, (part + 1) * SA_PART_POS)
        cc = kv_buf[slot, pages].reshape(SA_PART_POS, KV_LORA_RANK).astype(BF16)
        res = _dot_nt(lhs, cc)
        for hd in range(N_HEADS):
            x = res[hd * QK_NOPE_DIM:(hd + 1) * QK_NOPE_DIM]
            ssq_ref[hd:hd + 1, :] = jnp.sum(x * x, axis=0, keepdims=True)
        sn = res[N_HEADS * QK_NOPE_DIM:N_HEADS * QK_NOPE_DIM + N_HEADS]

        krt = kr_buf[slot, :, pos]
        kr2 = jnp.sum(krt * krt, axis=0, keepdims=True)
        krg = krt * gkr_ref[...]
        x1, x2 = krg[:ROPE_HALF], krg[ROPE_HALF:]
        cos, sin = cos_ref[:, pos], sin_ref[:, pos]
        rk = jnp.concatenate([x1 * cos - x2 * sin, x2 * cos + x1 * sin], axis=0).astype(BF16)
        sr = _dot(qr_ref[...][:, :QK_ROPE_DIM].astype(BF16), rk)

        rinv = lax.rsqrt((ssq_ref[...] + kr2) / QK_DIM + RMS_EPS)
        s = (sn + sr) * rinv * SM_SCALE
        m_old = m_ref[...]
        m_new = jnp.maximum(m_old, jnp.max(s, axis=-1, keepdims=True))
        p = jnp.exp(s - m_new)
        alpha = jnp.exp(m_old - m_new)
        l_ref[...] = alpha * l_ref[...] + jnp.sum(p, axis=-1, keepdims=True)
        acc_ref[...] = alpha * acc_ref[...] + _dot(p.astype(BF16), cc)
        m_ref[...] = m_new

    @pl.when(c == SA_STEPS - 1)
    def _():
        s_new = jnp.sum(qs_ref[...] * knew_ref[...], axis=-1, keepdims=True) * SM_SCALE
        m_o = m_ref[...]
        m_n = jnp.maximum(m_o, s_new)
        p_new = jnp.exp(s_new - m_n)
        a = jnp.exp(m_o - m_n)
        l = a * l_ref[...] + p_new
        acc = a * acc_ref[...] + p_new * cnew_ref[...]
        o_ref[...] = acc / l


def _attn_sample(page_table, cache_kv, cache_kr, wukt_flat, qa, qr, qs, knew, cnew, gkr, cos_t, sin_t):
    b = qa.shape[0]

    per_seq = pl.BlockSpec((None, N_HEADS, HEAD_PAD), lambda s, c, pt: (s, 0, 0))
    in_specs = [
        pl.BlockSpec(memory_space=pl.ANY), pl.BlockSpec(memory_space=pl.ANY),
        pl.BlockSpec(wukt_flat.shape, lambda s, c, pt: (0, 0)),
        per_seq, per_seq, per_seq, per_seq,
        pl.BlockSpec((None, 1, KV_LORA_RANK), lambda s, c, pt: (s, 0, 0)),
        pl.BlockSpec(gkr.shape, lambda s, c, pt: (0, 0)),
        pl.BlockSpec((ROPE_HALF, SA_POS), lambda s, c, pt: (0, c)),
        pl.BlockSpec((ROPE_HALF, SA_POS), lambda s, c, pt: (0, c)),
    ]
    grid_spec = pltpu.PrefetchScalarGridSpec(
        num_scalar_prefetch=1,
        grid=(b, SA_STEPS),
        in_specs=in_specs,
        out_specs=pl.BlockSpec((None, N_HEADS, KV_LORA_RANK), lambda s, c, pt: (s, 0, 0)),
        scratch_shapes=[
            pltpu.VMEM((2, SA_PAGES, PAGE_SIZE, KV_LORA_RANK), F32),
            pltpu.VMEM((2, QK_ROPE_DIM, SA_POS), F32),
            pltpu.SemaphoreType.DMA((2, 2)),
            pltpu.VMEM((N_HEADS, 1), F32), pltpu.VMEM((N_HEADS, 1), F32),
            pltpu.VMEM((N_HEADS, KV_LORA_RANK), F32), pltpu.VMEM((N_HEADS, SA_PART_POS), F32),
        ],
    )
    return pl.pallas_call(
        _attn_sample_kernel,
        grid_spec=grid_spec,
        out_shape=jax.ShapeDtypeStruct((b, N_HEADS, KV_LORA_RANK), F32),
        compiler_params=_cparams("arbitrary", "arbitrary"),
        name="attn_sample",
    )(page_table, cache_kv, cache_kr, wukt_flat, qa, qr, qs, knew, cnew, gkr, cos_t, sin_t)


def _mla_post_kernel(ol_ref, wuv_ref, wo_ref, y_ref, gate_ref, o_ref):
    parts = [_dot(ol_ref[:, hd * KV_LORA_RANK:(hd + 1) * KV_LORA_RANK], wuv_ref[hd])
             for hd in range(N_HEADS)]
    o = jnp.concatenate(parts, axis=-1).astype(BF16)
    o_ref[...] = y_ref[...] + gate_ref[...] * _dot(o, wo_ref[...])


def _mla_post(o_lat, wuv, wo, y, gate, tm, tiles_per_seq):
    t = y.shape[0]
    tok = lambda n: pl.BlockSpec((tm, n), lambda i: (i, 0))
    return pl.pallas_call(
        _mla_post_kernel,
        grid=(t // tm,),
        in_specs=[tok(N_HEADS * KV_LORA_RANK), _full(wuv.shape), _full(wo.shape), tok(D_MODEL),
                  _mod_spec(gate, tiles_per_seq)],
        out_specs=tok(D_MODEL),
        out_shape=jax.ShapeDtypeStruct((t, D_MODEL), F32),
        compiler_params=_cparams("parallel"),
        name="mla_post",
    )(o_lat, wuv, wo, y, gate)


def _sgu_kernel(y_ref, sh_ref, sc_ref, gate_ref, g1_ref, win_ref, bin_ref, gv_ref, sp_a_ref, sp_b_ref,
                wout_ref, o_ref, v_ref, *, tm, spatial):
    y = y_ref[...]
    h = _rms(y) * g1_ref[...]
    h = h * (1.0 + sc_ref[...]) + sh_ref[...]
    z = _gelu(_dot(h.astype(BF16), win_ref[...]) + bin_ref[...])
    u = z[:, :SG_HALF]
    v = _rms(z[:, SG_HALF:]) * gv_ref[...]
    if spatial:
        r = lax.broadcasted_iota(jnp.int32, (CHUNK, CHUNK), 0)
        cidx = lax.broadcasted_iota(jnp.int32, (CHUNK, CHUNK), 1)
        vb = v.astype(BF16)
        cols = []
        for g in range(SG_GROUPS):
            wc = jnp.where(cidx <= r, sp_a_ref[g], 0.0).astype(BF16)
            rows = [_dot(wc, vb[c * CHUNK:(c + 1) * CHUNK, g * CHUNK:(g + 1) * CHUNK]) + sp_b_ref[g]
                    for c in range(tm // CHUNK)]
            cols.append(jnp.concatenate(rows, axis=0))
        s = jnp.concatenate(cols, axis=-1)
        v_ref[...] = v[tm - CHUNK:, :]
    else:
        s = v * sp_a_ref[...] + sp_b_ref[...]
        v_ref[...] = v
    o = _dot((u * s).astype(BF16), wout_ref[...])
    o_ref[...] = y + gate_ref[...] * o


def _sgu(y, sh, sc, gate, g1, w, sp_a, sp_b, tm, tiles_per_seq, spatial):
    t = y.shape[0]
    tok = lambda n: pl.BlockSpec((tm, n), lambda i: (i, 0))
    if spatial:
        n_seq = t // (tm * tiles_per_seq)
        v_spec = pl.BlockSpec((None, CHUNK, SG_HALF), lambda i: (i // tiles_per_seq, 0, 0))
        v_shape = jax.ShapeDtypeStruct((n_seq, CHUNK, SG_HALF), F32)
    else:
        v_spec = tok(SG_HALF)
        v_shape = jax.ShapeDtypeStruct((t, SG_HALF), F32)
    return pl.pallas_call(
        functools.partial(_sgu_kernel, tm=tm, spatial=spatial),
        grid=(t // tm,),
        in_specs=[tok(D_MODEL), _mod_spec(sh, tiles_per_seq), _mod_spec(sc, tiles_per_seq),
                  _mod_spec(gate, tiles_per_seq), _full((1, D_MODEL)), _full(w["w_in"].shape),
                  _full((1, 2 * SG_HALF)), _full((1, SG_HALF)), _full(sp_a.shape), _full(sp_b.shape),
                  _full(w["w_out"].shape)],
        out_specs=[tok(D_MODEL), v_spec],
        out_shape=[jax.ShapeDtypeStruct((t, D_MODEL), F32), v_shape],
        compiler_params=_cparams("arbitrary"),
        name="sgu",
    )(y, sh, sc, gate, g1, w["w_in"], w["b_in"], w["g_v"], sp_a, sp_b, w["w_out"])


N_TOP = PEER_TOPK + 1
TOP_ROWS = 24
N_CAND = TOP_ROWS + 7 * 8 + (TOP_ROWS - 8)


def _sorting_network(n):
    def merge(lo, hi, r):
        step = r * 2
        if step < hi - lo:
            yield from merge(lo, hi, step)
            yield from merge(lo + r, hi, step)
            yield from [(i, i + r) for i in range(lo + r, hi - r, step)]
        else:
            yield (lo, lo + r)

    def sort(lo, hi):
        if hi - lo >= 1:
            mid = lo + (hi - lo) // 2
            yield from sort(lo, mid)
            yield from sort(mid + 1, hi)
            yield from merge(lo, hi, 1)

    return list(sort(0, n - 1))


def _top_values(x, n, emit):
    nb = x.shape[0] // 8
    v = [x[8 * k:8 * k + 8] for k in range(nb)]
    for i, j in _sorting_network(16):
        if j < nb:
            v[i], v[j] = jnp.maximum(v[i], v[j]), jnp.minimum(v[i], v[j])
    for r in range(n):
        head = jnp.max(v[0], axis=0, keepdims=True)
        emit(r, head)
        hit = v[0] == head
        for k in range(min(nb, n - 1 - r)):
            below = v[k + 1] if k + 1 < nb else NEG_BIG
            v[k] = jnp.where(hit, below, v[k])


def _pair_words(x):
    bits = pltpu.bitcast(x.astype(BF16).astype(F32), jnp.uint32)
    return bits | (bits >> 16)


def _peer_pre_kernel(y_ref, sh_ref, sc_ref, g2_ref, wq_ref, sk_ref, ht_ref, rank_ref, eb_ref, cnt_ref,
                     cc_ref, s_scr, t_scr, cand_scr, tv_scr):
    h = _rms(y_ref[...]) * g2_ref[...]
    h = h * (1.0 + sc_ref[...]) + sh_ref[...]
    ht_ref[...] = h.T.astype(BF16)
    q = _dot(h.astype(BF16), wq_ref[...])
    for hd in range(PEER_HEADS):
        for side in range(2):
            o = (hd * 2 + side) * N_KEYS
            s_scr[side, hd] = _dot_nt(sk_ref[side], q[:, o:o + N_KEYS].astype(BF16))
    t_scr[...] = jnp.full_like(t_scr, NEG_BIG)

    def tops(idx, _):
        side, hd = idx // PEER_HEADS, idx % PEER_HEADS

        def emit(r, row):
            t_scr[side, hd, r:r + 1, :] = row

        _top_values(s_scr[side, hd], N_TOP, emit)
        return 0

    lax.fori_loop(0, 2 * PEER_HEADS, tops, 0)

    def finish(hd, _):
        t1, t2 = t_scr[0, hd], t_scr[1, hd]
        cand_scr[0:TOP_ROWS] = t1[0:1] + t2
        for a in range(1, 8):
            cand_scr[TOP_ROWS + (a - 1) * 8:TOP_ROWS + a * 8] = t1[a:a + 1] + t2[0:8]
        cand_scr[TOP_ROWS + 56:] = t1[8:] + t2[0:1]
        cand = cand_scr[...]

        def emit(r, row):
            tv_scr[r:r + 1, :] = row

        _top_values(cand, N_TOP, emit)
        tau = 0.5 * (tv_scr[PEER_TOPK - 1:PEER_TOPK, :] + tv_scr[PEER_TOPK:PEER_TOPK + 1, :])
        top = t1[:1] + t2[:1]
        z = jnp.sum(jnp.where(cand >= tau, jnp.exp(cand - top), 0.0), axis=0, keepdims=True)
        s1, s2 = s_scr[0, hd], s_scr[1, hd]
        eb = jnp.exp(s2 - t2[:1])
        th = tau - s1
        cc = jnp.exp(s1 - t1[:1]) * (0.5 / z)
        rank = jnp.zeros_like(s2)
        cnt = jnp.zeros_like(s1)
        for b in range(PEER_TOPK):
            tb = t2[b:b + 1]
            rank = jnp.where(tb > s2, b + 1.0, rank)
            cnt = jnp.where(tb >= th, b + 1.0, cnt)
        cntw = _pair_words(cnt)
        ccw = _pair_words(cc)
        for col in range(s1.shape[1] // 128):
            cs = slice(col * 128, (col + 1) * 128)
            rank_ref[hd, col] = pltpu.bitcast(rank[:, cs].astype(BF16), jnp.uint32)
            eb_ref[hd, col] = pltpu.bitcast(eb[:, cs].astype(BF16), jnp.uint32)
            cnt_ref[hd, col] = cntw[:, cs]
            cc_ref[hd, col] = ccw[:, cs]
        return 0

    lax.fori_loop(0, PEER_HEADS, finish, 0)


def _peer_pre(y, sh, sc, g2, wq, sk, tm, tiles_per_seq):
    t = y.shape[0]
    hk = pl.BlockSpec((PEER_HEADS, tm // 128, N_KEYS, 128), lambda i: (0, i, 0, 0))
    hp = pl.BlockSpec((PEER_HEADS, tm // 128, N_KEYS // 2, 128), lambda i: (0, i, 0, 0))
    packed = jax.ShapeDtypeStruct((PEER_HEADS, t // 128, N_KEYS // 2, 128), jnp.uint32)
    words = jax.ShapeDtypeStruct((PEER_HEADS, t // 128, N_KEYS, 128), jnp.uint32)
    return pl.pallas_call(
        _peer_pre_kernel,
        grid=(t // tm,),
        in_specs=[pl.BlockSpec((tm, D_MODEL), lambda i: (i, 0)), _mod_spec(sh, tiles_per_seq),
                  _mod_spec(sc, tiles_per_seq), _full((1, D_MODEL)), _full(wq.shape), _full(sk.shape)],
        out_specs=[pl.BlockSpec((D_MODEL, tm), lambda i: (0, i)), hp, hp, hk, hk],
        out_shape=[jax.ShapeDtypeStruct((D_MODEL, t), BF16), packed, packed, words, words],
        scratch_shapes=[pltpu.VMEM((2, PEER_HEADS, N_KEYS, tm), F32),
                        pltpu.VMEM((2, PEER_HEADS, TOP_ROWS, tm), F32),
                        pltpu.VMEM((N_CAND, tm), F32),
                        pltpu.VMEM((TOP_ROWS, tm), F32)],
        compiler_params=_cparams("parallel"),
        name="peer_pre",
    )(y, sh, sc, g2, wq, sk)


PEER_EB = 1024
PEER_ROWS = PEER_EB // N_KEYS
PEER_BLOCKS = N_EXPERTS // PEER_EB


BF16_ROWS = 16


def _selection_weights(rank_ref, eb_ref, cnt_ref, cc_ref, col):
    def row_of(ref, hd, ii):
        word = jnp.broadcast_to(ref[hd, col, ii:ii + 1, :], (8, 128))
        tile = pltpu.bitcast(word, BF16)
        return jnp.concatenate([tile] * (N_KEYS // BF16_ROWS), axis=0)

    rows = []
    for ii in range(PEER_ROWS):
        w = None
        for hd in range(PEER_HEADS):
            eb = pltpu.bitcast(eb_ref[hd, col], BF16)
            rank = pltpu.bitcast(rank_ref[hd, col], BF16)
            t = jnp.where(rank < row_of(cnt_ref, hd, ii), eb, jnp.zeros_like(eb))
            t = t * row_of(cc_ref, hd, ii)
            w = t if w is None else w + t
        rows.append(w)
    return jnp.concatenate(rows, axis=0)


def _peer_dense_kernel(ht_ref, u_ref, vt_ref, rank_ref, eb_ref, cnt_ref, cc_ref, y_ref, gate_ref, o_ref,
                       acc_ref, *, tm):
    e = pl.program_id(1)

    @pl.when(e == 0)
    def _():
        acc_ref[...] = jnp.zeros_like(acc_ref)

    act = _gelu_twice(_dot(u_ref[...], ht_ref[...]))
    w = jnp.concatenate([_selection_weights(rank_ref, eb_ref, cnt_ref, cc_ref, col)
                         for col in range(tm // 128)], axis=1)
    acc_ref[...] += _dot(vt_ref[...], act.astype(BF16) * w)

    @pl.when(e == PEER_BLOCKS - 1)
    def _():
        o_ref[...] = y_ref[...] + gate_ref[...] * acc_ref[...].T


def _peer_dense(ht, u_b, vt_b, layer, rank, eb, cnt, cc, y, gate, tm, tiles_per_seq):
    t = y.shape[0]
    cols = tm // 128
    hk = pl.BlockSpec((PEER_HEADS, cols, N_KEYS // 2, 128), lambda i, e: (0, i, 0, 0))
    hr = pl.BlockSpec((PEER_HEADS, cols, PEER_ROWS, 128), lambda i, e: (0, i, e, 0))
    return pl.pallas_call(
        functools.partial(_peer_dense_kernel, tm=tm),
        grid=(t // tm, PEER_BLOCKS),
        in_specs=[pl.BlockSpec((D_MODEL, tm), lambda i, e: (0, i)),
                  pl.BlockSpec((None, PEER_EB, D_MODEL), lambda i, e: (layer, e, 0)),
                  pl.BlockSpec((None, D_MODEL, PEER_EB), lambda i, e: (layer, 0, e)),
                  hk, hk, hr, hr,
                  pl.BlockSpec((tm, D_MODEL), lambda i, e: (i, 0)),
                  pl.BlockSpec((None,) + gate.shape[1:], lambda i, e: (i // tiles_per_seq, 0, 0))],
        out_specs=pl.BlockSpec((tm, D_MODEL), lambda i, e: (i, 0)),
        out_shape=jax.ShapeDtypeStruct((t, D_MODEL), F32),
        scratch_shapes=[pltpu.VMEM((D_MODEL, tm), F32)],
        compiler_params=_cparams("parallel", "arbitrary"),
        name="peer_dense",
    )(ht, u_b, vt_b, rank, eb, cnt, cc, y, gate)


def _peer(y, sh, sc, gate, g2, wq, sk, u_b, vt_b, layer, tm, tiles_per_seq, dense_tiles=1):
    ht, rank, eb, cnt, cc = _peer_pre(y, sh, sc, g2, wq, sk, tm, tiles_per_seq)
    return _peer_dense(ht, u_b, vt_b, layer, rank, eb, cnt, cc, y, gate, tm * dense_tiles,
                       tiles_per_seq // dense_tiles)


def _pad_heads(w, used):
    w = jnp.pad(w, [(0, 0)] * (w.ndim - 1) + [(0, HEAD_PAD - used)])
    return w.reshape(w.shape[:-2] + (N_HEADS * HEAD_PAD,))


def _pad_gain(g):
    return jnp.pad(g, (0, HEAD_PAD - QK_DIM)).reshape(1, HEAD_PAD)


def _rope_tables(pos):
    inv_freq = ROPE_THETA ** (-jnp.arange(ROPE_HALF, dtype=F32) * 2.0 / QK_ROPE_DIM)
    ang = pos.astype(F32)[:, None] * inv_freq[None, :]
    cos, sin = jnp.cos(ang), jnp.sin(ang)
    t = pos.shape[0]
    one = jnp.ones((t, QK_NOPE_DIM), F32)
    z64 = jnp.zeros((t, QK_NOPE_DIM), F32)
    z16 = jnp.zeros((t, ROPE_HALF), F32)
    z32 = jnp.zeros((t, HEAD_PAD - QK_DIM), F32)
    cos_t = jnp.concatenate([one, cos, cos, z32], axis=1)
    sa = jnp.concatenate([z64, -sin, z16, z32], axis=1)
    sb = jnp.concatenate([z64, z16, sin, z32], axis=1)
    return cos_t, sa, sb, cos, sin


def _mla_weights(w_in, g_q_a, g_kv_a, w_uq, g_q, w_uk):
    lo = Q_LORA_RANK + KV_LORA_RANK
    w_in_p = jnp.concatenate(
        [w_in[:, :lo], jnp.zeros((D_MODEL, QK_NOPE_DIM), F32), w_in[:, lo:],
         jnp.zeros((D_MODEL, HEAD_PAD - QK_DIM), F32)], axis=1).astype(BF16)
    return {
        "w_in": w_in_p,
        "g_q_a": g_q_a.reshape(1, -1),
        "g_kv_a": g_kv_a.reshape(1, -1),
        "w_uq": _pad_heads(w_uq.reshape(Q_LORA_RANK, N_HEADS, QK_DIM), QK_DIM).astype(BF16),
        "g_q": _pad_gain(g_q),
        "w_uk": _pad_heads(w_uk, QK_NOPE_DIM).astype(BF16),
    }


def _split_mod(m, n_prompt):
    m = m.reshape(m.shape[0], 6, D_MODEL)
    mp = [m[:n_prompt, j].reshape(n_prompt, 1, D_MODEL) for j in range(6)]
    ms = [m[n_prompt:, j].reshape(1, -1, D_MODEL) for j in range(6)]
    return mp, ms


def kernel(x_prompt, x_sample, c_prompt, c_sample, cache_kv_latent, cache_k_rope, page_table, norm1_g, norm2_g, w_mod, b_mod, mla_w_in, mla_g_q_a, mla_g_kv_a, mla_w_uq, mla_g_q, mla_w_uk, mla_w_uv, mla_g_k, mla_w_o, sg_w_in, sg_b_in, sg_g_v, sg_w_s, sg_b_s, sg_w_out, peer_w_q, peer_sub_keys, peer_u, peer_v):
    yp = x_prompt.reshape(BATCH * SEQ, D_MODEL)
    ys = x_sample.reshape(DEC_BATCH, D_MODEL)
    mods = _modulation(jnp.concatenate([c_prompt, c_sample], axis=0), w_mod, b_mod)

    tm_p, tps_p = 512, SEQ // 512
    tm_s, tps_s = DEC_BATCH, 1

    u_b = peer_u.astype(BF16)
    vt_b = jnp.swapaxes(peer_v, 1, 2).astype(BF16)

    outs = {}
    for layer in range(DEPTH):
        mp, ms = _split_mod(mods[layer], BATCH)
        g1 = norm1_g[layer].reshape(1, D_MODEL)
        g2 = norm2_g[layer].reshape(1, D_MODEL)
        if layer % 2 == 0:
            a = layer // 2
            w = _mla_weights(mla_w_in[a], mla_g_q_a[a], mla_g_kv_a[a], mla_w_uq[a], mla_g_q[a], mla_w_uk[a])
            gk = _pad_gain(mla_g_k[a])
            w["g_k"] = gk
            wuv = jnp.transpose(mla_w_uv[a], (1, 0, 2)).astype(BF16)
            wo = mla_w_o[a].astype(BF16)
            cos_p, sa_p, sb_p, _, _ = _rope_tables(jnp.arange(SEQ))
            cos_s, sa_s, sb_s, _, _ = _rope_tables(jnp.full((1,), PAST_LEN))
            _, _, _, cos_c, sin_c = _rope_tables(jnp.arange(PAST_LEN))

            q, k, ckv_p, krp_p = _mla_pre(yp, mp[0], mp[1], g1, w, (cos_p, sa_p, sb_p), tm_p, tps_p, True, BF16)
            o_lat = _attn_prompt(q, k, ckv_p)
            yp = _mla_post(o_lat, wuv, wo, yp, mp[2], tm_p, tps_p)

            q_s, k_s, ckv_s, krp_s = _mla_pre(ys, ms[0], ms[1], g1, w, (cos_s, sa_s, sb_s), tm_s, tps_s, False, F32)
            wukt = jnp.pad(jnp.transpose(mla_w_uk[a], (1, 2, 0)),
                           ((0, 0), (0, HEAD_PAD - QK_NOPE_DIM), (0, 0))).astype(BF16)
            gk_nope = gk * (jnp.arange(HEAD_PAD) < QK_NOPE_DIM)[None, :]
            qa, qr = (jnp.transpose(x, (1, 0, 2)) for x in _absorb(q_s, gk_nope, wukt))
            wukt_flat = jnp.transpose(mla_w_uk[a], (1, 2, 0)).reshape(N_HEADS * QK_NOPE_DIM, KV_LORA_RANK).astype(BF16)
            gkr = mla_g_k[a][QK_NOPE_DIM:].reshape(QK_ROPE_DIM, 1)
            o_lat_s = _attn_sample(
                page_table, cache_kv_latent[a], jnp.swapaxes(cache_k_rope[a], 1, 2), wukt_flat, qa, qr,
                q_s.reshape(DEC_BATCH, N_HEADS, HEAD_PAD), k_s.reshape(DEC_BATCH, N_HEADS, HEAD_PAD),
                ckv_s.reshape(DEC_BATCH, 1, KV_LORA_RANK), gkr, cos_c.T, sin_c.T)
            ys = _mla_post(o_lat_s.reshape(DEC_BATCH, N_HEADS * KV_LORA_RANK).astype(BF16), wuv, wo, ys,
                           ms[2], tm_s, tps_s)

            outs.setdefault("kv_p", []).append(ckv_p.reshape(BATCH, SEQ, KV_LORA_RANK))
            outs.setdefault("kr_p", []).append(krp_p[:, QK_NOPE_DIM:QK_DIM].reshape(BATCH, SEQ, QK_ROPE_DIM))
            outs.setdefault("kv_s", []).append(ckv_s.reshape(DEC_BATCH, 1, KV_LORA_RANK))
            outs.setdefault("kr_s", []).append(krp_s[:, QK_NOPE_DIM:QK_DIM].reshape(DEC_BATCH, 1, QK_ROPE_DIM))
        else:
            bidx = layer // 2
            w = {"w_in": sg_w_in[bidx].astype(BF16), "b_in": sg_b_in[bidx].reshape(1, -1),
                 "g_v": sg_g_v[bidx].reshape(1, -1), "w_out": sg_w_out[bidx].astype(BF16)}
            sp_a = sg_w_s[bidx]
            sp_b = sg_b_s[bidx].reshape(SG_GROUPS, CHUNK, 1)
            yp, vp = _sgu(yp, mp[0], mp[1], mp[2], g1, w, sp_a, sp_b, tm_p, tps_p, True)
            coef = jnp.repeat(sg_w_s[bidx][:, 0, 0], CHUNK).reshape(1, SG_HALF)
            bias = jnp.repeat(sg_b_s[bidx][:, 0], CHUNK).reshape(1, SG_HALF)
            ys, vs = _sgu(ys, ms[0], ms[1], ms[2], g1, w, coef, bias, tm_s, tps_s, False)
            outs.setdefault("v_p", []).append(vp)
            outs.setdefault("v_s", []).append(vs.reshape(DEC_BATCH, 1, SG_HALF))

        wq = peer_w_q[layer].astype(BF16)
        sk = peer_sub_keys[layer].astype(BF16)
        yp = _peer(yp, mp[3], mp[4], mp[5], g2, wq, sk, u_b, vt_b, layer, tm_p, tps_p, dense_tiles=2)
        ys = _peer(ys, ms[3], ms[4], ms[5], g2, wq, sk, u_b, vt_b, layer, tm_s, tps_s)

    return (yp.reshape(BATCH, SEQ, D_MODEL), ys.reshape(DEC_BATCH, 1, D_MODEL),
            jnp.stack(outs["kv_p"]), jnp.stack(outs["kr_p"]), jnp.stack(outs["kv_s"]), jnp.stack(outs["kr_s"]),
            jnp.stack(outs["v_p"]), jnp.stack(outs["v_s"]))
```

```python
import functools

import jax
import jax.numpy as jnp
from jax import lax
from jax.experimental import pallas as pl
from jax.experimental.pallas import tpu as pltpu

F32 = jnp.float32
BF16 = jnp.bfloat16

D_MODEL = 1024
BATCH = 8
SEQ = 2048
DEPTH = 2
DEC_BATCH = 128
PAST_LEN = 16384
PAGE_SIZE = 128
N_PAGES = PAST_LEN // PAGE_SIZE

N_HEADS = 8
Q_LORA_RANK = 256
KV_LORA_RANK = 128
QK_NOPE_DIM = 64
QK_ROPE_DIM = 32
QK_DIM = QK_NOPE_DIM + QK_ROPE_DIM
V_HEAD_DIM = 128
ROPE_THETA = 10000.0
SM_SCALE = QK_DIM ** -0.5
HEAD_PAD = 128
ROPE_HALF = QK_ROPE_DIM // 2

CHUNK = 128
SG_HALF = 1536
SG_GROUPS = 12

PEER_HEADS = 8
N_KEYS = 128
N_EXPERTS = N_KEYS * N_KEYS
PEER_TOPK = 16
RMS_EPS = 1e-6

NEG_BIG = -1e30
VMEM_LIMIT = 56 * 1024 * 1024


def _cparams(*sem):
    return pltpu.CompilerParams(dimension_semantics=sem, vmem_limit_bytes=VMEM_LIMIT)


def _rms(x):
    return x * lax.rsqrt(jnp.mean(x * x, axis=-1, keepdims=True) + RMS_EPS)


def _gelu(x):
    cdf = 0.5 * (1.0 + jnp.tanh(0.7978845608028654 * (x + 0.044715 * (x * x * x))))
    return x * cdf


def _gelu_twice(x):
    c = 0.7978845608028654
    return x * (1.0 + jnp.tanh(x * (c + (0.044715 * c) * (x * x))))


def _dot(a, b):
    return jnp.dot(a, b, preferred_element_type=F32)


def _dot_nt(a, b):
    return lax.dot_general(a, b, (((1,), (1,)), ((), ())), preferred_element_type=F32)


def _full(shape):
    n = len(shape)
    return pl.BlockSpec(shape, lambda *_: (0,) * n)


def _mod_spec(arr, tiles_per_seq):
    return pl.BlockSpec((None,) + arr.shape[1:], lambda t, *_: (t // tiles_per_seq, 0, 0))


def _mod_kernel(c_ref, w_ref, b_ref, o_ref):
    c = c_ref[...]
    a = (c * jax.nn.sigmoid(c)).astype(BF16)
    o_ref[...] = _dot(a, w_ref[...].astype(BF16)) + b_ref[...]


def _modulation(c_all, w_mod, b_mod):
    r = c_all.shape[0]
    tn = 1536
    return pl.pallas_call(
        _mod_kernel,
        grid=(DEPTH, 6 * D_MODEL // tn),
        in_specs=[
            pl.BlockSpec((r, D_MODEL), lambda l, n: (0, 0)),
            pl.BlockSpec((None, D_MODEL, tn), lambda l, n: (l, 0, n)),
            pl.BlockSpec((None, 1, tn), lambda l, n: (l, 0, n)),
        ],
        out_specs=pl.BlockSpec((None, r, tn), lambda l, n: (l, 0, n)),
        out_shape=jax.ShapeDtypeStruct((DEPTH, r, 6 * D_MODEL), F32),
        compiler_params=_cparams("parallel", "parallel"),
        name="modulation",
    )(c_all, w_mod, b_mod.reshape(DEPTH, 1, 6 * D_MODEL))


def _rope(x, cos, sa, sb):
    return x * cos + pltpu.roll(x, HEAD_PAD - ROPE_HALF, 1) * sa + pltpu.roll(x, ROPE_HALF, 1) * sb


def _mla_pre_kernel(y_ref, sh_ref, sc_ref, g1_ref, win_ref, gqa_ref, gkva_ref, wuq_ref, gq_ref,
                    wuk_ref, gk_ref, cos_ref, sa_ref, sb_ref, q_ref, k_ref, ckv_ref, kr_ref):
    h = _rms(y_ref[...]) * g1_ref[...]
    h = h * (1.0 + sc_ref[...]) + sh_ref[...]
    z = _dot(h.astype(BF16), win_ref[...])
    cq = _rms(z[:, :Q_LORA_RANK]) * gqa_ref[...]
    ckv = _rms(z[:, Q_LORA_RANK:Q_LORA_RANK + KV_LORA_RANK]) * gkva_ref[...]
    krp = z[:, Q_LORA_RANK + KV_LORA_RANK:]
    ckv_ref[...] = ckv
    kr_ref[...] = krp
    q = _dot(cq.astype(BF16), wuq_ref[...])
    kn = _dot(ckv.astype(BF16), wuk_ref[...])
    cos, sa, sb = cos_ref[...], sa_ref[...], sb_ref[...]
    gq, gk = gq_ref[...], gk_ref[...]
    ones = jnp.ones((HEAD_PAD, HEAD_PAD), BF16)

    def head_norm(x, g):
        ssq = _dot((x * x).astype(BF16), ones)
        return x * lax.rsqrt(ssq / QK_DIM + RMS_EPS) * g

    for hd in range(N_HEADS):
        sl = slice(hd * HEAD_PAD, (hd + 1) * HEAD_PAD)
        q_ref[:, sl] = _rope(head_norm(q[:, sl], gq), cos, sa, sb).astype(q_ref.dtype)
        k_ref[:, sl] = _rope(head_norm(kn[:, sl] + krp, gk), cos, sa, sb).astype(k_ref.dtype)


def _mla_pre(y, sh, sc, g1, w, rope_tabs, tm, tiles_per_seq, rope_tiled, qk_dtype):
    t = y.shape[0]
    cos, sa, sb = rope_tabs
    if rope_tiled:
        rspec = pl.BlockSpec((tm, HEAD_PAD), lambda i: (i % tiles_per_seq, 0))
    else:
        rspec = _full((1, HEAD_PAD))
    wide = N_HEADS * HEAD_PAD
    tok = lambda n: pl.BlockSpec((tm, n), lambda i: (i, 0))
    return pl.pallas_call(
        _mla_pre_kernel,
        grid=(t // tm,),
        in_specs=[
            tok(D_MODEL), _mod_spec(sh, tiles_per_seq), _mod_spec(sc, tiles_per_seq),
            _full((1, D_MODEL)), _full(w["w_in"].shape), _full((1, Q_LORA_RANK)),
            _full((1, KV_LORA_RANK)), _full(w["w_uq"].shape), _full((1, HEAD_PAD)),
            _full(w["w_uk"].shape), _full((1, HEAD_PAD)), rspec, rspec, rspec,
        ],
        out_specs=[tok(wide), tok(wide), tok(KV_LORA_RANK), tok(HEAD_PAD)],
        out_shape=[
            jax.ShapeDtypeStruct((t, wide), qk_dtype),
            jax.ShapeDtypeStruct((t, wide), qk_dtype),
            jax.ShapeDtypeStruct((t, KV_LORA_RANK), F32),
            jax.ShapeDtypeStruct((t, HEAD_PAD), F32),
        ],
        compiler_params=_cparams("parallel"),
        name="mla_pre",
    )(y, sh, sc, g1, w["w_in"], w["g_q_a"], w["g_kv_a"], w["w_uq"], w["g_q"], w["w_uk"], w["g_k"],
      cos, sa, sb)


ATT_TQ = 256
ATT_TK = 256


def _attn_prompt_kernel(q_ref, k_ref, ckv_ref, o_ref, m_ref, acc_ref):
    i = pl.program_id(1)
    m_ref[...] = jnp.full_like(m_ref, NEG_BIG)
    acc_ref[...] = jnp.zeros_like(acc_ref)
    ones = jnp.ones((ATT_TK, KV_LORA_RANK), BF16)

    def block(j, diagonal):
        off = pl.multiple_of(j * ATT_TK, ATT_TK)
        cb = jnp.concatenate([ckv_ref[pl.ds(off, ATT_TK), :].astype(BF16), ones], axis=1)
        for hd in range(N_HEADS):
            sl = slice(hd * HEAD_PAD, (hd + 1) * HEAD_PAD)
            s = _dot_nt(q_ref[:, sl], k_ref[pl.ds(off, ATT_TK), sl]) * SM_SCALE
            if diagonal:
                row = lax.broadcasted_iota(jnp.int32, (ATT_TQ, ATT_TK), 0)
                col = lax.broadcasted_iota(jnp.int32, (ATT_TQ, ATT_TK), 1)
                s = jnp.where(col <= row, s, NEG_BIG)
            m_old = m_ref[hd]
            s_max = jnp.max(jnp.maximum(s[:, :128], s[:, 128:]), axis=-1, keepdims=True)
            m_new = jnp.maximum(m_old, jnp.broadcast_to(s_max, m_old.shape))
            p = jnp.concatenate([jnp.exp(s[:, :128] - m_new), jnp.exp(s[:, 128:] - m_new)], axis=1)
            alpha = jnp.exp(m_old - m_new)
            pv = _dot(p.astype(BF16), cb)
            acc_ref[hd] = jnp.concatenate([alpha, alpha], axis=1) * acc_ref[hd] + pv
            m_ref[hd] = m_new

    def body(j, carry):
        block(j, False)
        return carry

    lax.fori_loop(0, i, body, 0)
    block(i, True)
    for hd in range(N_HEADS):
        acc = acc_ref[hd]
        o_ref[:, hd * HEAD_PAD:(hd + 1) * HEAD_PAD] = (
            acc[:, :KV_LORA_RANK] / acc[:, KV_LORA_RANK:]).astype(o_ref.dtype)


def _attn_prompt(q, k, ckv):
    nq = SEQ // ATT_TQ
    wide = N_HEADS * HEAD_PAD
    return pl.pallas_call(
        _attn_prompt_kernel,
        grid=(BATCH, nq),
        in_specs=[
            pl.BlockSpec((ATT_TQ, wide), lambda b, i: (b * nq + i, 0)),
            pl.BlockSpec((SEQ, wide), lambda b, i: (b, 0)),
            pl.BlockSpec((SEQ, KV_LORA_RANK), lambda b, i: (b, 0)),
        ],
        out_specs=pl.BlockSpec((ATT_TQ, wide), lambda b, i: (b * nq + i, 0)),
        out_shape=jax.ShapeDtypeStruct((BATCH * SEQ, wide), BF16),
        scratch_shapes=[pltpu.VMEM((N_HEADS, ATT_TQ, 128), F32),
                        pltpu.VMEM((N_HEADS, ATT_TQ, 2 * KV_LORA_RANK), F32)],
        compiler_params=_cparams("parallel", "parallel"),
        name="attn_prompt",
    )(q, k, ckv)


SA_PAGES = 64
SA_POS = SA_PAGES * PAGE_SIZE
SA_STEPS = N_PAGES // SA_PAGES
SA_PART_PAGES = 32
SA_PART_POS = SA_PART_PAGES * PAGE_SIZE


def _absorb_kernel(q_ref, gk_ref, wukt_ref, qa_ref, qr_ref):
    gk = gk_ref[...]
    for hd in range(N_HEADS):
        qh = q_ref[:, hd * HEAD_PAD:(hd + 1) * HEAD_PAD]
        qa_ref[hd] = _dot((qh * gk).astype(BF16), wukt_ref[hd])
        qr_ref[hd] = pltpu.roll(qh, HEAD_PAD - QK_NOPE_DIM, 1)


def _absorb(q_s, gk_nope, wukt):
    b = q_s.shape[0]
    shp = jax.ShapeDtypeStruct((N_HEADS, b, HEAD_PAD), F32)
    return pl.pallas_call(
        _absorb_kernel,
        in_specs=[_full(q_s.shape), _full(gk_nope.shape), _full(wukt.shape)],
        out_specs=[_full(shp.shape), _full(shp.shape)],
        out_shape=[shp, shp],
        grid=(1,),
        compiler_params=_cparams("arbitrary"),
        name="absorb_q",
    )(q_s, gk_nope, wukt)


def _attn_sample_kernel(pt_ref, kv_hbm, kr_hbm, wukt_ref, qa_ref, qr_ref, qs_ref, knew_ref, cnew_ref,
                        gkr_ref, cos_ref, sin_ref, o_ref, kv_buf, kr_buf, sems, m_ref, l_ref, acc_ref, ssq_ref):
    s, c = pl.program_id(0), pl.program_id(1)
    g = s * SA_STEPS + c
    last = pl.num_programs(0) * SA_STEPS - 1

    def page_copies(step, slot):
        seq, chunk = step // SA_STEPS, step % SA_STEPS
        for p in range(SA_PAGES):
            page = pt_ref[seq, chunk * SA_PAGES + p]
            yield pltpu.make_async_copy(kv_hbm.at[page], kv_buf.at[slot, p], sems.at[slot, 0])
            yield pltpu.make_async_copy(kr_hbm.at[page], kr_buf.at[slot, :, pl.ds(p * PAGE_SIZE, PAGE_SIZE)],
                                        sems.at[slot, 1])

    @pl.when(g == 0)
    def _():
        for cp in page_copies(g, 0):
            cp.start()

    @pl.when(g < last)
    def _():
        for cp in page_copies(g + 1, (g + 1) % 2):
            cp.start()

    slot = g % 2
    for cp in page_copies(g, slot):
        cp.wait()

    @pl.when(c == 0)
    def _():
        m_ref[...] = jnp.full_like(m_ref, NEG_BIG)
        l_ref[...] = jnp.zeros_like(l_ref)
        acc_ref[...] = jnp.zeros_like(acc_ref)

    qa16 = jnp.concatenate([qa_ref[...], jnp.zeros((8, KV_LORA_RANK), F32)], axis=0).astype(BF16)
    lhs = jnp.concatenate([wukt_ref[...], qa16], axis=0)
    for part in range(SA_PAGES // SA_PART_PAGES):
        pages = slice(part * SA_PART_PAGES, (part + 1) * SA_PART_PAGES)
        pos = slice(part * SA_PART_POS, (part + 1) * SA_PART_POS)
        cc = kv_buf[slot, pages].reshape(SA_PART_POS, KV_LORA_RANK).astype(BF16)
        res = _dot_nt(lhs, cc)
        for hd in range(N_HEADS):
            x = res[hd * QK_NOPE_DIM:(hd + 1) * QK_NOPE_DIM]
            ssq_ref[hd:hd + 1, :] = jnp.sum(x * x, axis=0, keepdims=True)
        sn = res[N_HEADS * QK_NOPE_DIM:N_HEADS * QK_NOPE_DIM + N_HEADS]

        krt = kr_buf[slot, :, pos]
        kr2 = jnp.sum(krt * krt, axis=0, keepdims=True)
        krg = krt * gkr_ref[...]
        x1, x2 = krg[:ROPE_HALF], krg[ROPE_HALF:]
        cos, sin = cos_ref[:, pos], sin_ref[:, pos]
        rk = jnp.concatenate([x1 * cos - x2 * sin, x2 * cos + x1 * sin], axis=0).astype(BF16)
        sr = _dot(qr_ref[...][:, :QK_ROPE_DIM].astype(BF16), rk)

        rinv = lax.rsqrt((ssq_ref[...] + kr2) / QK_DIM + RMS_EPS)
        s = (sn + sr) * rinv * SM_SCALE
        m_old = m_ref[...]
        m_new = jnp.maximum(m_old, jnp.max(s, axis=-1, keepdims=True))
        p = jnp.exp(s - m_new)
        alpha = jnp.exp(m_old - m_new)
        l_ref[...] = alpha * l_ref[...] + jnp.sum(p, axis=-1, keepdims=True)
        acc_ref[...] = alpha * acc_ref[...] + _dot(p.astype(BF16), cc)
        m_ref[...] = m_new

    @pl.when(c == SA_STEPS - 1)
    def _():
        s_new = jnp.sum(qs_ref[...] * knew_ref[...], axis=-1, keepdims=True) * SM_SCALE
        m_o = m_ref[...]
        m_n = jnp.maximum(m_o, s_new)
        p_new = jnp.exp(s_new - m_n)
        a = jnp.exp(m_o - m_n)
        l = a * l_ref[...] + p_new
        acc = a * acc_ref[...] + p_new * cnew_ref[...]
        o_ref[...] = acc / l


def _attn_sample(page_table, cache_kv, cache_kr, wukt_flat, qa, qr, qs, knew, cnew, gkr, cos_t, sin_t):
    b = qa.shape[0]

    per_seq = pl.BlockSpec((None, N_HEADS, HEAD_PAD), lambda s, c, pt: (s, 0, 0))
    in_specs = [
        pl.BlockSpec(memory_space=pl.ANY), pl.BlockSpec(memory_space=pl.ANY),
        pl.BlockSpec(wukt_flat.shape, lambda s, c, pt: (0, 0)),
        per_seq, per_seq, per_seq, per_seq,
        pl.BlockSpec((None, 1, KV_LORA_RANK), lambda s, c, pt: (s, 0, 0)),
        pl.BlockSpec(gkr.shape, lambda s, c, pt: (0, 0)),
        pl.BlockSpec((ROPE_HALF, SA_POS), lambda s, c, pt: (0, c)),
        pl.BlockSpec((ROPE_HALF, SA_POS), lambda s, c, pt: (0, c)),
    ]
    grid_spec = pltpu.PrefetchScalarGridSpec(
        num_scalar_prefetch=1,
        grid=(b, SA_STEPS),
        in_specs=in_specs,
        out_specs=pl.BlockSpec((None, N_HEADS, KV_LORA_RANK), lambda s, c, pt: (s, 0, 0)),
        scratch_shapes=[
            pltpu.VMEM((2, SA_PAGES, PAGE_SIZE, KV_LORA_RANK), F32),
            pltpu.VMEM((2, QK_ROPE_DIM, SA_POS), F32),
            pltpu.SemaphoreType.DMA((2, 2)),
            pltpu.VMEM((N_HEADS, 1), F32), pltpu.VMEM((N_HEADS, 1), F32),
            pltpu.VMEM((N_HEADS, KV_LORA_RANK), F32), pltpu.VMEM((N_HEADS, SA_PART_POS), F32),
        ],
    )
    return pl.pallas_call(
        _attn_sample_kernel,
        grid_spec=grid_spec,
        out_shape=jax.ShapeDtypeStruct((b, N_HEADS, KV_LORA_RANK), F32),
        compiler_params=_cparams("arbitrary", "arbitrary"),
        name="attn_sample",
    )(page_table, cache_kv, cache_kr, wukt_flat, qa, qr, qs, knew, cnew, gkr, cos_t, sin_t)


def _mla_post_kernel(ol_ref, wuv_ref, wo_ref, y_ref, gate_ref, o_ref):
    parts = [_dot(ol_ref[:, hd * KV_LORA_RANK:(hd + 1) * KV_LORA_RANK], wuv_ref[hd])
             for hd in range(N_HEADS)]
    o = jnp.concatenate(parts, axis=-1).astype(BF16)
    o_ref[...] = y_ref[...] + gate_ref[...] * _dot(o, wo_ref[...])


def _mla_post(o_lat, wuv, wo, y, gate, tm, tiles_per_seq):
    t = y.shape[0]
    tok = lambda n: pl.BlockSpec((tm, n), lambda i: (i, 0))
    return pl.pallas_call(
        _mla_post_kernel,
        grid=(t // tm,),
        in_specs=[tok(N_HEADS * KV_LORA_RANK), _full(wuv.shape), _full(wo.shape), tok(D_MODEL),
                  _mod_spec(gate, tiles_per_seq)],
        out_specs=tok(D_MODEL),
        out_shape=jax.ShapeDtypeStruct((t, D_MODEL), F32),
        compiler_params=_cparams("parallel"),
        name="mla_post",
    )(o_lat, wuv, wo, y, gate)


def _sgu_kernel(y_ref, sh_ref, sc_ref, gate_ref, g1_ref, win_ref, bin_ref, gv_ref, sp_a_ref, sp_b_ref,
                wout_ref, o_ref, v_ref, *, tm, spatial):
    y = y_ref[...]
    h = _rms(y) * g1_ref[...]
    h = h * (1.0 + sc_ref[...]) + sh_ref[...]
    z = _gelu(_dot(h.astype(BF16), win_ref[...]) + bin_ref[...])
    u = z[:, :SG_HALF]
    v = _rms(z[:, SG_HALF:]) * gv_ref[...]
    if spatial:
        r = lax.broadcasted_iota(jnp.int32, (CHUNK, CHUNK), 0)
        cidx = lax.broadcasted_iota(jnp.int32, (CHUNK, CHUNK), 1)
        vb = v.astype(BF16)
        cols = []
        for g in range(SG_GROUPS):
            wc = jnp.where(cidx <= r, sp_a_ref[g], 0.0).astype(BF16)
            rows = [_dot(wc, vb[c * CHUNK:(c + 1) * CHUNK, g * CHUNK:(g + 1) * CHUNK]) + sp_b_ref[g]
                    for c in range(tm // CHUNK)]
            cols.append(jnp.concatenate(rows, axis=0))
        s = jnp.concatenate(cols, axis=-1)
        v_ref[...] = v[tm - CHUNK:, :]
    else:
        s = v * sp_a_ref[...] + sp_b_ref[...]
        v_ref[...] = v
    o = _dot((u * s).astype(BF16), wout_ref[...])
    o_ref[...] = y + gate_ref[...] * o


def _sgu(y, sh, sc, gate, g1, w, sp_a, sp_b, tm, tiles_per_seq, spatial):
    t = y.shape[0]
    tok = lambda n: pl.BlockSpec((tm, n), lambda i: (i, 0))
    if spatial:
        n_seq = t // (tm * tiles_per_seq)
        v_spec = pl.BlockSpec((None, CHUNK, SG_HALF), lambda i: (i // tiles_per_seq, 0, 0))
        v_shape = jax.ShapeDtypeStruct((n_seq, CHUNK, SG_HALF), F32)
    else:
        v_spec = tok(SG_HALF)
        v_shape = jax.ShapeDtypeStruct((t, SG_HALF), F32)
    return pl.pallas_call(
        functools.partial(_sgu_kernel, tm=tm, spatial=spatial),
        grid=(t // tm,),
        in_specs=[tok(D_MODEL), _mod_spec(sh, tiles_per_seq), _mod_spec(sc, tiles_per_seq),
                  _mod_spec(gate, tiles_per_seq), _full((1, D_MODEL)), _full(w["w_in"].shape),
                  _full((1, 2 * SG_HALF)), _full((1, SG_HALF)), _full(sp_a.shape), _full(sp_b.shape),
                  _full(w["w_out"].shape)],
        out_specs=[tok(D_MODEL), v_spec],
        out_shape=[jax.ShapeDtypeStruct((t, D_MODEL), F32), v_shape],
        compiler_params=_cparams("arbitrary"),
        name="sgu",
    )(y, sh, sc, gate, g1, w["w_in"], w["b_in"], w["g_v"], sp_a, sp_b, w["w_out"])


N_TOP = PEER_TOPK + 1
TOP_ROWS = 24
N_CAND = TOP_ROWS + 7 * 8 + (TOP_ROWS - 8)


def _sorting_network(n):
    def merge(lo, hi, r):
        step = r * 2
        if step < hi - lo:
            yield from merge(lo, hi, step)
            yield from merge(lo + r, hi, step)
            yield from [(i, i + r) for i in range(lo + r, hi - r, step)]
        else:
            yield (lo, lo + r)

    def sort(lo, hi):
        if hi - lo >= 1:
            mid = lo + (hi - lo) // 2
            yield from sort(lo, mid)
            yield from sort(mid + 1, hi)
            yield from merge(lo, hi, 1)

    return list(sort(0, n - 1))


def _top_values(x, n, emit):
    nb = x.shape[0] // 8
    v = [x[8 * k:8 * k + 8] for k in range(nb)]
    for i, j in _sorting_network(16):
        if j < nb:
            v[i], v[j] = jnp.maximum(v[i], v[j]), jnp.minimum(v[i], v[j])
    for r in range(n):
        head = jnp.max(v[0], axis=0, keepdims=True)
        emit(r, head)
        hit = v[0] == head
        for k in range(min(nb, n - 1 - r)):
            below = v[k + 1] if k + 1 < nb else NEG_BIG
            v[k] = jnp.where(hit, below, v[k])


def _peer_pre_kernel(y_ref, sh_ref, sc_ref, g2_ref, wq_ref, sk_ref, ht_ref, rank_ref, eb_ref, cnt_ref,
                     cc_ref, s_scr, t_scr, cand_scr, tv_scr):
    h = _rms(y_ref[...]) * g2_ref[...]
    h = h * (1.0 + sc_ref[...]) + sh_ref[...]
    ht_ref[...] = h.T.astype(BF16)
    q = _dot(h.astype(BF16), wq_ref[...])
    for hd in range(PEER_HEADS):
        for side in range(2):
            o = (hd * 2 + side) * N_KEYS
            s_scr[side, hd] = _dot_nt(sk_ref[side], q[:, o:o + N_KEYS].astype(BF16))
    t_scr[...] = jnp.full_like(t_scr, NEG_BIG)

    def tops(idx, _):
        side, hd = idx // PEER_HEADS, idx % PEER_HEADS

        def emit(r, row):
            t_scr[side, hd, r:r + 1, :] = row

        _top_values(s_scr[side, hd], N_TOP, emit)
        return 0

    lax.fori_loop(0, 2 * PEER_HEADS, tops, 0)

    def finish(hd, _):
        t1, t2 = t_scr[0, hd], t_scr[1, hd]
        cand_scr[0:TOP_ROWS] = t1[0:1] + t2
        for a in range(1, 8):
            cand_scr[TOP_ROWS + (a - 1) * 8:TOP_ROWS + a * 8] = t1[a:a + 1] + t2[0:8]
        cand_scr[TOP_ROWS + 56:] = t1[8:] + t2[0:1]
        cand = cand_scr[...]

        def emit(r, row):
            tv_scr[r:r + 1, :] = row

        _top_values(cand, N_TOP, emit)
        tau = 0.5 * (tv_scr[PEER_TOPK - 1:PEER_TOPK, :] + tv_scr[PEER_TOPK:PEER_TOPK + 1, :])
        top = t1[:1] + t2[:1]
        z = jnp.sum(jnp.where(cand >= tau, jnp.exp(cand - top), 0.0), axis=0, keepdims=True)
        s1, s2 = s_scr[0, hd], s_scr[1, hd]
        eb = jnp.exp(s2 - t2[:1])
        th = tau - s1
        cc = jnp.exp(s1 - t1[:1]) * (0.5 / z)
        rank = jnp.zeros_like(s2)
        cnt = jnp.zeros_like(s1)
        for b in range(PEER_TOPK):
            tb = t2[b:b + 1]
            rank = jnp.where(tb > s2, b + 1.0, rank)
            cnt = jnp.where(tb >= th, b + 1.0, cnt)
        for col in range(s1.shape[1] // 128):
            cs = slice(col * 128, (col + 1) * 128)
            rank_ref[hd, col] = rank[:, cs]
            eb_ref[hd, col] = eb[:, cs]
            cnt_ref[hd, col] = cnt[:, cs]
            cc_ref[hd, col] = cc[:, cs]
        return 0

    lax.fori_loop(0, PEER_HEADS, finish, 0)


def _peer_pre(y, sh, sc, g2, wq, sk, tm, tiles_per_seq):
    t = y.shape[0]
    hk = pl.BlockSpec((PEER_HEADS, tm // 128, N_KEYS, 128), lambda i: (0, i, 0, 0))
    hk_shape = jax.ShapeDtypeStruct((PEER_HEADS, t // 128, N_KEYS, 128), F32)
    return pl.pallas_call(
        _peer_pre_kernel,
        grid=(t // tm,),
        in_specs=[pl.BlockSpec((tm, D_MODEL), lambda i: (i, 0)), _mod_spec(sh, tiles_per_seq),
                  _mod_spec(sc, tiles_per_seq), _full((1, D_MODEL)), _full(wq.shape), _full(sk.shape)],
        out_specs=[pl.BlockSpec((D_MODEL, tm), lambda i: (0, i)), hk, hk, hk, hk],
        out_shape=[jax.ShapeDtypeStruct((D_MODEL, t), BF16), hk_shape, hk_shape, hk_shape, hk_shape],
        scratch_shapes=[pltpu.VMEM((2, PEER_HEADS, N_KEYS, tm), F32),
                        pltpu.VMEM((2, PEER_HEADS, TOP_ROWS, tm), F32),
                        pltpu.VMEM((N_CAND, tm), F32),
                        pltpu.VMEM((TOP_ROWS, tm), F32)],
        compiler_params=_cparams("parallel"),
        name="peer_pre",
    )(y, sh, sc, g2, wq, sk)


PEER_EB = 1024
PEER_ROWS = PEER_EB // N_KEYS
PEER_BLOCKS = N_EXPERTS // PEER_EB


BF16_ROWS = 16


def _selection_weights(rank_ref, eb_ref, cnt_ref, cc_ref, col):
    def row_of(ref, hd, ii):
        tile = jnp.broadcast_to(ref[hd, col, ii:ii + 1, :], (BF16_ROWS, 128)).astype(BF16)
        return jnp.concatenate([tile] * (N_KEYS // BF16_ROWS), axis=0)

    ebs = [eb_ref[hd, col].astype(BF16) for hd in range(PEER_HEADS)]
    ranks = [rank_ref[hd, col].astype(BF16) for hd in range(PEER_HEADS)]
    rows = []
    for ii in range(PEER_ROWS):
        w = None
        for hd in range(PEER_HEADS):
            t = jnp.where(ranks[hd] < row_of(cnt_ref, hd, ii), ebs[hd], jnp.zeros_like(ebs[hd]))
            t = t * row_of(cc_ref, hd, ii)
            w = t if w is None else w + t
        rows.append(w)
    return jnp.concatenate(rows, axis=0)


def _peer_dense_kernel(ht_ref, u_ref, vt_ref, rank_ref, eb_ref, cnt_ref, cc_ref, y_ref, gate_ref, o_ref,
                       acc_ref, *, tm):
    e = pl.program_id(1)

    @pl.when(e == 0)
    def _():
        acc_ref[...] = jnp.zeros_like(acc_ref)

    act = _gelu_twice(_dot(u_ref[...], ht_ref[...]))
    w = jnp.concatenate([_selection_weights(rank_ref, eb_ref, cnt_ref, cc_ref, col)
                         for col in range(tm // 128)], axis=1)
    acc_ref[...] += _dot(vt_ref[...], act.astype(BF16) * w)

    @pl.when(e == PEER_BLOCKS - 1)
    def _():
        o_ref[...] = y_ref[...] + gate_ref[...] * acc_ref[...].T


def _peer_dense(ht, u_b, vt_b, layer, rank, eb, cnt, cc, y, gate, tm, tiles_per_seq):
    t = y.shape[0]
    cols = tm // 128
    hk = pl.BlockSpec((PEER_HEADS, cols, N_KEYS, 128), lambda i, e: (0, i, 0, 0))
    hr = pl.BlockSpec((PEER_HEADS, cols, PEER_ROWS, 128), lambda i, e: (0, i, e, 0))
    return pl.pallas_call(
        functools.partial(_peer_dense_kernel, tm=tm),
        grid=(t // tm, PEER_BLOCKS),
        in_specs=[pl.BlockSpec((D_MODEL, tm), lambda i, e: (0, i)),
                  pl.BlockSpec((None, PEER_EB, D_MODEL), lambda i, e: (layer, e, 0)),
                  pl.BlockSpec((None, D_MODEL, PEER_EB), lambda i, e: (layer, 0, e)),
                  hk, hk, hr, hr,
                  pl.BlockSpec((tm, D_MODEL), lambda i, e: (i, 0)),
                  pl.BlockSpec((None,) + gate.shape[1:], lambda i, e: (i // tiles_per_seq, 0, 0))],
        out_specs=pl.BlockSpec((tm, D_MODEL), lambda i, e: (i, 0)),
        out_shape=jax.ShapeDtypeStruct((t, D_MODEL), F32),
        scratch_shapes=[pltpu.VMEM((D_MODEL, tm), F32)],
        compiler_params=_cparams("parallel", "arbitrary"),
        name="peer_dense",
    )(ht, u_b, vt_b, rank, eb, cnt, cc, y, gate)


def _peer(y, sh, sc, gate, g2, wq, sk, u_b, vt_b, layer, tm, tiles_per_seq, dense_tiles=1):
    ht, rank, eb, cnt, cc = _peer_pre(y, sh, sc, g2, wq, sk, tm, tiles_per_seq)
    return _peer_dense(ht, u_b, vt_b, layer, rank, eb, cnt, cc, y, gate, tm * dense_tiles,
                       tiles_per_seq // dense_tiles)


def _pad_heads(w, used):
    w = jnp.pad(w, [(0, 0)] * (w.ndim - 1) + [(0, HEAD_PAD - used)])
    return w.reshape(w.shape[:-2] + (N_HEADS * HEAD_PAD,))


def _pad_gain(g):
    return jnp.pad(g, (0, HEAD_PAD - QK_DIM)).reshape(1, HEAD_PAD)


def _rope_tables(pos):
    inv_freq = ROPE_THETA ** (-jnp.arange(ROPE_HALF, dtype=F32) * 2.0 / QK_ROPE_DIM)
    ang = pos.astype(F32)[:, None] * inv_freq[None, :]
    cos, sin = jnp.cos(ang), jnp.sin(ang)
    t = pos.shape[0]
    one = jnp.ones((t, QK_NOPE_DIM), F32)
    z64 = jnp.zeros((t, QK_NOPE_DIM), F32)
    z16 = jnp.zeros((t, ROPE_HALF), F32)
    z32 = jnp.zeros((t, HEAD_PAD - QK_DIM), F32)
    cos_t = jnp.concatenate([one, cos, cos, z32], axis=1)
    sa = jnp.concatenate([z64, -sin, z16, z32], axis=1)
    sb = jnp.concatenate([z64, z16, sin, z32], axis=1)
    return cos_t, sa, sb, cos, sin


def _mla_weights(w_in, g_q_a, g_kv_a, w_uq, g_q, w_uk):
    lo = Q_LORA_RANK + KV_LORA_RANK
    w_in_p = jnp.concatenate(
        [w_in[:, :lo], jnp.zeros((D_MODEL, QK_NOPE_DIM), F32), w_in[:, lo:],
         jnp.zeros((D_MODEL, HEAD_PAD - QK_DIM), F32)], axis=1).astype(BF16)
    return {
        "w_in": w_in_p,
        "g_q_a": g_q_a.reshape(1, -1),
        "g_kv_a": g_kv_a.reshape(1, -1),
        "w_uq": _pad_heads(w_uq.reshape(Q_LORA_RANK, N_HEADS, QK_DIM), QK_DIM).astype(BF16),
        "g_q": _pad_gain(g_q),
        "w_uk": _pad_heads(w_uk, QK_NOPE_DIM).astype(BF16),
    }


def _split_mod(m, n_prompt):
    m = m.reshape(m.shape[0], 6, D_MODEL)
    mp = [m[:n_prompt, j].reshape(n_prompt, 1, D_MODEL) for j in range(6)]
    ms = [m[n_prompt:, j].reshape(1, -1, D_MODEL) for j in range(6)]
    return mp, ms


def kernel(x_prompt, x_sample, c_prompt, c_sample, cache_kv_latent, cache_k_rope, page_table, norm1_g, norm2_g, w_mod, b_mod, mla_w_in, mla_g_q_a, mla_g_kv_a, mla_w_uq, mla_g_q, mla_w_uk, mla_w_uv, mla_g_k, mla_w_o, sg_w_in, sg_b_in, sg_g_v, sg_w_s, sg_b_s, sg_w_out, peer_w_q, peer_sub_keys, peer_u, peer_v):
    yp = x_prompt.reshape(BATCH * SEQ, D_MODEL)
    ys = x_sample.reshape(DEC_BATCH, D_MODEL)
    mods = _modulation(jnp.concatenate([c_prompt, c_sample], axis=0), w_mod, b_mod)

    tm_p, tps_p = 512, SEQ // 512
    tm_s, tps_s = DEC_BATCH, 1

    u_b = peer_u.astype(BF16)
    vt_b = jnp.swapaxes(peer_v, 1, 2).astype(BF16)

    outs = {}
    for layer in range(DEPTH):
        mp, ms = _split_mod(mods[layer], BATCH)
        g1 = norm1_g[layer].reshape(1, D_MODEL)
        g2 = norm2_g[layer].reshape(1, D_MODEL)
        if layer % 2 == 0:
            a = layer // 2
            w = _mla_weights(mla_w_in[a], mla_g_q_a[a], mla_g_kv_a[a], mla_w_uq[a], mla_g_q[a], mla_w_uk[a])
            gk = _pad_gain(mla_g_k[a])
            w["g_k"] = gk
            wuv = jnp.transpose(mla_w_uv[a], (1, 0, 2)).astype(BF16)
            wo = mla_w_o[a].astype(BF16)
            cos_p, sa_p, sb_p, _, _ = _rope_tables(jnp.arange(SEQ))
            cos_s, sa_s, sb_s, _, _ = _rope_tables(jnp.full((1,), PAST_LEN))
            _, _, _, cos_c, sin_c = _rope_tables(jnp.arange(PAST_LEN))

            q, k, ckv_p, krp_p = _mla_pre(yp, mp[0], mp[1], g1, w, (cos_p, sa_p, sb_p), tm_p, tps_p, True, BF16)
            o_lat = _attn_prompt(q, k, ckv_p)
            yp = _mla_post(o_lat, wuv, wo, yp, mp[2], tm_p, tps_p)

            q_s, k_s, ckv_s, krp_s = _mla_pre(ys, ms[0], ms[1], g1, w, (cos_s, sa_s, sb_s), tm_s, tps_s, False, F32)
            wukt = jnp.pad(jnp.transpose(mla_w_uk[a], (1, 2, 0)),
                           ((0, 0), (0, HEAD_PAD - QK_NOPE_DIM), (0, 0))).astype(BF16)
            gk_nope = gk * (jnp.arange(HEAD_PAD) < QK_NOPE_DIM)[None, :]
            qa, qr = (jnp.transpose(x, (1, 0, 2)) for x in _absorb(q_s, gk_nope, wukt))
            wukt_flat = jnp.transpose(mla_w_uk[a], (1, 2, 0)).reshape(N_HEADS * QK_NOPE_DIM, KV_LORA_RANK).astype(BF16)
            gkr = mla_g_k[a][QK_NOPE_DIM:].reshape(QK_ROPE_DIM, 1)
            o_lat_s = _attn_sample(
                page_table, cache_kv_latent[a], jnp.swapaxes(cache_k_rope[a], 1, 2), wukt_flat, qa, qr,
                q_s.reshape(DEC_BATCH, N_HEADS, HEAD_PAD), k_s.reshape(DEC_BATCH, N_HEADS, HEAD_PAD),
                ckv_s.reshape(DEC_BATCH, 1, KV_LORA_RANK), gkr, cos_c.T, sin_c.T)
            ys = _mla_post(o_lat_s.reshape(DEC_BATCH, N_HEADS * KV_LORA_RANK).astype(BF16), wuv, wo, ys,
                           ms[2], tm_s, tps_s)

            outs.setdefault("kv_p", []).append(ckv_p.reshape(BATCH, SEQ, KV_LORA_RANK))
            outs.setdefault("kr_p", []).append(krp_p[:, QK_NOPE_DIM:QK_DIM].reshape(BATCH, SEQ, QK_ROPE_DIM))
            outs.setdefault("kv_s", []).append(ckv_s.reshape(DEC_BATCH, 1, KV_LORA_RANK))
            outs.setdefault("kr_s", []).append(krp_s[:, QK_NOPE_DIM:QK_DIM].reshape(DEC_BATCH, 1, QK_ROPE_DIM))
        else:
            bidx = layer // 2
            w = {"w_in": sg_w_in[bidx].astype(BF16), "b_in": sg_b_in[bidx].reshape(1, -1),
                 "g_v": sg_g_v[bidx].reshape(1, -1), "w_out": sg_w_out[bidx].astype(BF16)}
            sp_a = sg_w_s[bidx]
            sp_b = sg_b_s[bidx].reshape(SG_GROUPS, CHUNK, 1)
            yp, vp = _sgu(yp, mp[0], mp[1], mp[2], g1, w, sp_a, sp_b, tm_p, tps_p, True)
            coef = jnp.repeat(sg_w_s[bidx][:, 0, 0], CHUNK).reshape(1, SG_HALF)
            bias = jnp.repeat(sg_b_s[bidx][:, 0], CHUNK).reshape(1, SG_HALF)
            ys, vs = _sgu(ys, ms[0], ms[1], ms[2], g1, w, coef, bias, tm_s, tps_s, False)
            outs.setdefault("v_p", []).append(vp)
            outs.setdefault("v_s", []).append(vs.reshape(DEC_BATCH, 1, SG_HALF))

        wq = peer_w_q[layer].astype(BF16)
        sk = peer_sub_keys[layer].astype(BF16)
        yp = _peer(yp, mp[3], mp[4], mp[5], g2, wq, sk, u_b, vt_b, layer, tm_p, tps_p, dense_tiles=2)
        ys = _peer(ys, ms[3], ms[4], ms[5], g2, wq, sk, u_b, vt_b, layer, tm_s, tps_s)

    return (yp.reshape(BATCH, SEQ, D_MODEL), ys.reshape(DEC_BATCH, 1, D_MODEL),
            jnp.stack(outs["kv_p"]), jnp.stack(outs["kr_p"]), jnp.stack(outs["kv_s"]), jnp.stack(outs["kr_s"]),
            jnp.stack(outs["v_p"]), jnp.stack(outs["v_s"]))
```

```python
import functools

import jax
import jax.numpy as jnp
from jax import lax
from jax.experimental import pallas as pl
from jax.experimental.pallas import tpu as pltpu

F32 = jnp.float32
BF16 = jnp.bfloat16

D_MODEL = 1024
BATCH = 8
SEQ = 2048
DEPTH = 2
DEC_BATCH = 128
PAST_LEN = 16384
PAGE_SIZE = 128
N_PAGES = PAST_LEN // PAGE_SIZE

N_HEADS = 8
Q_LORA_RANK = 256
KV_LORA_RANK = 128
QK_NOPE_DIM = 64
QK_ROPE_DIM = 32
QK_DIM = QK_NOPE_DIM + QK_ROPE_DIM
V_HEAD_DIM = 128
ROPE_THETA = 10000.0
SM_SCALE = QK_DIM ** -0.5
LOG2_E = 1.4426950408889634
HEAD_PAD = 128
ROPE_HALF = QK_ROPE_DIM // 2

CHUNK = 128
SG_HALF = 1536
SG_GROUPS = 12

PEER_HEADS = 8
N_KEYS = 128
N_EXPERTS = N_KEYS * N_KEYS
PEER_TOPK = 16
RMS_EPS = 1e-6

NEG_BIG = -1e30
VMEM_LIMIT = 56 * 1024 * 1024


def _cparams(*sem):
    return pltpu.CompilerParams(dimension_semantics=sem, vmem_limit_bytes=VMEM_LIMIT)


def _rms(x):
    return x * lax.rsqrt(jnp.mean(x * x, axis=-1, keepdims=True) + RMS_EPS)


def _gelu(x):
    cdf = 0.5 * (1.0 + jnp.tanh(0.7978845608028654 * (x + 0.044715 * (x * x * x))))
    return x * cdf


def _gelu_twice(x):
    c = 0.7978845608028654
    return x * (1.0 + jnp.tanh(x * (c + (0.044715 * c) * (x * x))))


def _dot(a, b):
    return jnp.dot(a, b, preferred_element_type=F32)


def _dot_nt(a, b):
    return lax.dot_general(a, b, (((1,), (1,)), ((), ())), preferred_element_type=F32)


def _full(shape):
    n = len(shape)
    return pl.BlockSpec(shape, lambda *_: (0,) * n)


def _mod_spec(arr, tiles_per_seq):
    return pl.BlockSpec((None,) + arr.shape[1:], lambda t, *_: (t // tiles_per_seq, 0, 0))


def _mod_kernel(c_ref, w_ref, b_ref, o_ref):
    c = c_ref[...]
    a = (c * jax.nn.sigmoid(c)).astype(BF16)
    o_ref[...] = _dot(a, w_ref[...].astype(BF16)) + b_ref[...]


def _modulation(c_all, w_mod, b_mod):
    r = c_all.shape[0]
    tn = 1536
    return pl.pallas_call(
        _mod_kernel,
        grid=(DEPTH, 6 * D_MODEL // tn),
        in_specs=[
            pl.BlockSpec((r, D_MODEL), lambda l, n: (0, 0)),
            pl.BlockSpec((None, D_MODEL, tn), lambda l, n: (l, 0, n)),
            pl.BlockSpec((None, 1, tn), lambda l, n: (l, 0, n)),
        ],
        out_specs=pl.BlockSpec((None, r, tn), lambda l, n: (l, 0, n)),
        out_shape=jax.ShapeDtypeStruct((DEPTH, r, 6 * D_MODEL), F32),
        compiler_params=_cparams("parallel", "parallel"),
        name="modulation",
    )(c_all, w_mod, b_mod.reshape(DEPTH, 1, 6 * D_MODEL))


def _rope(x, cos, sa, sb):
    return x * cos + pltpu.roll(x, HEAD_PAD - ROPE_HALF, 1) * sa + pltpu.roll(x, ROPE_HALF, 1) * sb


def _mla_pre_kernel(y_ref, sh_ref, sc_ref, g1_ref, win_ref, gqa_ref, gkva_ref, wuq_ref, gq_ref,
                    wuk_ref, gk_ref, cos_ref, sa_ref, sb_ref, q_ref, k_ref, ckv_ref, kr_ref):
    h = _rms(y_ref[...]) * g1_ref[...]
    h = h * (1.0 + sc_ref[...]) + sh_ref[...]
    z = _dot(h.astype(BF16), win_ref[...])
    cq = _rms(z[:, :Q_LORA_RANK]) * gqa_ref[...]
    ckv = _rms(z[:, Q_LORA_RANK:Q_LORA_RANK + KV_LORA_RANK]) * gkva_ref[...]
    krp = z[:, Q_LORA_RANK + KV_LORA_RANK:]
    ckv_ref[...] = ckv
    kr_ref[...] = krp
    q = _dot(cq.astype(BF16), wuq_ref[...])
    kn = _dot(ckv.astype(BF16), wuk_ref[...])
    cos, sa, sb = cos_ref[...], sa_ref[...], sb_ref[...]
    gq, gk = gq_ref[...], gk_ref[...]
    ones = jnp.ones((HEAD_PAD, HEAD_PAD), BF16)

    def head_norm(x, g):
        ssq = _dot((x * x).astype(BF16), ones)
        return x * lax.rsqrt(ssq / QK_DIM + RMS_EPS) * g

    for hd in range(N_HEADS):
        sl = slice(hd * HEAD_PAD, (hd + 1) * HEAD_PAD)
        q_ref[:, sl] = _rope(head_norm(q[:, sl], gq), cos, sa, sb).astype(q_ref.dtype)
        k_ref[:, sl] = _rope(head_norm(kn[:, sl] + krp, gk), cos, sa, sb).astype(k_ref.dtype)


def _mla_pre(y, sh, sc, g1, w, rope_tabs, tm, tiles_per_seq, rope_tiled, qk_dtype):
    t = y.shape[0]
    cos, sa, sb = rope_tabs
    if rope_tiled:
        rspec = pl.BlockSpec((tm, HEAD_PAD), lambda i: (i % tiles_per_seq, 0))
    else:
        rspec = _full((1, HEAD_PAD))
    wide = N_HEADS * HEAD_PAD
    tok = lambda n: pl.BlockSpec((tm, n), lambda i: (i, 0))
    return pl.pallas_call(
        _mla_pre_kernel,
        grid=(t // tm,),
        in_specs=[
            tok(D_MODEL), _mod_spec(sh, tiles_per_seq), _mod_spec(sc, tiles_per_seq),
            _full((1, D_MODEL)), _full(w["w_in"].shape), _full((1, Q_LORA_RANK)),
            _full((1, KV_LORA_RANK)), _full(w["w_uq"].shape), _full((1, HEAD_PAD)),
            _full(w["w_uk"].shape), _full((1, HEAD_PAD)), rspec, rspec, rspec,
        ],
        out_specs=[tok(wide), tok(wide), tok(KV_LORA_RANK), tok(HEAD_PAD)],
        out_shape=[
            jax.ShapeDtypeStruct((t, wide), qk_dtype),
            jax.ShapeDtypeStruct((t, wide), qk_dtype),
            jax.ShapeDtypeStruct((t, KV_LORA_RANK), F32),
            jax.ShapeDtypeStruct((t, HEAD_PAD), F32),
        ],
        compiler_params=_cparams("parallel"),
        name="mla_pre",
    )(y, sh, sc, g1, w["w_in"], w["g_q_a"], w["g_kv_a"], w["w_uq"], w["g_q"], w["w_uk"], w["g_k"],
      cos, sa, sb)


ATT_TQ = 256
ATT_TK = 256


def _attn_prompt_kernel(q_ref, k_ref, ckv_ref, o_ref, m_ref, acc_ref):
    i = pl.program_id(1)
    m_ref[...] = jnp.full_like(m_ref, NEG_BIG)
    acc_ref[...] = jnp.zeros_like(acc_ref)
    ones = jnp.ones((ATT_TK, KV_LORA_RANK), BF16)

    def block(j, diagonal):
        off = pl.multiple_of(j * ATT_TK, ATT_TK)
        cb = jnp.concatenate([ckv_ref[pl.ds(off, ATT_TK), :].astype(BF16), ones], axis=1)
        for hd in range(N_HEADS):
            sl = slice(hd * HEAD_PAD, (hd + 1) * HEAD_PAD)
            s = _dot_nt(q_ref[:, sl], k_ref[pl.ds(off, ATT_TK), sl])
            if diagonal:
                row = lax.broadcasted_iota(jnp.int32, (ATT_TQ, ATT_TK), 0)
                col = lax.broadcasted_iota(jnp.int32, (ATT_TQ, ATT_TK), 1)
                s = jnp.where(col <= row, s, NEG_BIG)
            m_old = m_ref[hd]
            s_max = jnp.max(jnp.maximum(s[:, :128], s[:, 128:]), axis=-1, keepdims=True)
            m_new = jnp.maximum(m_old, jnp.broadcast_to(s_max, m_old.shape))
            p = jnp.concatenate([jnp.exp2(s[:, :128] - m_new), jnp.exp2(s[:, 128:] - m_new)], axis=1)
            alpha = jnp.exp2(m_old - m_new)
            pv = _dot(p.astype(BF16), cb)
            acc_ref[hd] = jnp.concatenate([alpha, alpha], axis=1) * acc_ref[hd] + pv
            m_ref[hd] = m_new

    def body(j, carry):
        block(j, False)
        return carry

    lax.fori_loop(0, i, body, 0)
    block(i, True)
    for hd in range(N_HEADS):
        acc = acc_ref[hd]
        o_ref[:, hd * HEAD_PAD:(hd + 1) * HEAD_PAD] = (
            acc[:, :KV_LORA_RANK] / acc[:, KV_LORA_RANK:]).astype(o_ref.dtype)


def _attn_prompt(q, k, ckv):
    nq = SEQ // ATT_TQ
    wide = N_HEADS * HEAD_PAD
    return pl.pallas_call(
        _attn_prompt_kernel,
        grid=(BATCH, nq),
        in_specs=[
            pl.BlockSpec((ATT_TQ, wide), lambda b, i: (b * nq + i, 0)),
            pl.BlockSpec((SEQ, wide), lambda b, i: (b, 0)),
            pl.BlockSpec((SEQ, KV_LORA_RANK), lambda b, i: (b, 0)),
        ],
        out_specs=pl.BlockSpec((ATT_TQ, wide), lambda b, i: (b * nq + i, 0)),
        out_shape=jax.ShapeDtypeStruct((BATCH * SEQ, wide), BF16),
        scratch_shapes=[pltpu.VMEM((N_HEADS, ATT_TQ, 128), F32),
                        pltpu.VMEM((N_HEADS, ATT_TQ, 2 * KV_LORA_RANK), F32)],
        compiler_params=_cparams("parallel", "parallel"),
        name="attn_prompt",
    )(q, k, ckv)


SA_PAGES = 64
SA_POS = SA_PAGES * PAGE_SIZE
SA_STEPS = N_PAGES // SA_PAGES
SA_PART_PAGES = 32
SA_PART_POS = SA_PART_PAGES * PAGE_SIZE


def _absorb_kernel(q_ref, gk_ref, wukt_ref, qa_ref, qr_ref):
    gk = gk_ref[...]
    for hd in range(N_HEADS):
        qh = q_ref[:, hd * HEAD_PAD:(hd + 1) * HEAD_PAD]
        qa_ref[hd] = _dot((qh * gk).astype(BF16), wukt_ref[hd])
        qr_ref[hd] = pltpu.roll(qh, HEAD_PAD - QK_NOPE_DIM, 1)


def _absorb(q_s, gk_nope, wukt):
    b = q_s.shape[0]
    shp = jax.ShapeDtypeStruct((N_HEADS, b, HEAD_PAD), F32)
    return pl.pallas_call(
        _absorb_kernel,
        in_specs=[_full(q_s.shape), _full(gk_nope.shape), _full(wukt.shape)],
        out_specs=[_full(shp.shape), _full(shp.shape)],
        out_shape=[shp, shp],
        grid=(1,),
        compiler_params=_cparams("arbitrary"),
        name="absorb_q",
    )(q_s, gk_nope, wukt)


def _attn_sample_kernel(pt_ref, kv_hbm, kr_hbm, wukt_ref, qa_ref, qr_ref, qs_ref, knew_ref, cnew_ref,
                        gkr_ref, cos_ref, sin_ref, o_ref, kv_buf, kr_buf, sems, m_ref, l_ref, acc_ref, ssq_ref):
    s, c = pl.program_id(0), pl.program_id(1)
    g = s * SA_STEPS + c
    last = pl.num_programs(0) * SA_STEPS - 1

    def page_copies(step, slot):
        seq, chunk = step // SA_STEPS, step % SA_STEPS
        for p in range(SA_PAGES):
            page = pt_ref[seq, chunk * SA_PAGES + p]
            yield pltpu.make_async_copy(kv_hbm.at[page], kv_buf.at[slot, p], sems.at[slot, 0])
            yield pltpu.make_async_copy(kr_hbm.at[page], kr_buf.at[slot, :, pl.ds(p * PAGE_SIZE, PAGE_SIZE)],
                                        sems.at[slot, 1])

    @pl.when(g == 0)
    def _():
        for cp in page_copies(g, 0):
            cp.start()

    @pl.when(g < last)
    def _():
        for cp in page_copies(g + 1, (g + 1) % 2):
            cp.start()

    slot = g % 2
    for cp in page_copies(g, slot):
        cp.wait()

    @pl.when(c == 0)
    def _():
        m_ref[...] = jnp.full_like(m_ref, NEG_BIG)
        l_ref[...] = jnp.zeros_like(l_ref)
        acc_ref[...] = jnp.zeros_like(acc_ref)

    qa16 = jnp.concatenate([qa_ref[...], jnp.zeros((8, KV_LORA_RANK), F32)], axis=0).astype(BF16)
    lhs = jnp.concatenate([wukt_ref[...], qa16], axis=0)
    for part in range(SA_PAGES // SA_PART_PAGES):
        pages = slice(part * SA_PART_PAGES, (part + 1) * SA_PART_PAGES)
        pos = slice(part * SA_PART_POS, (part + 1) * SA_PART_POS)
        cc = kv_buf[slot, pages].reshape(SA_PART_POS, KV_LORA_RANK).astype(BF16)
        res = _dot_nt(lhs, cc)
        for hd in range(N_HEADS):
            x = res[hd * QK_NOPE_DIM:(hd + 1) * QK_NOPE_DIM]
            ssq_ref[hd:hd + 1, :] = jnp.sum(x * x, axis=0, keepdims=True)
        sn = res[N_HEADS * QK_NOPE_DIM:N_HEADS * QK_NOPE_DIM + N_HEADS]

        krt = kr_buf[slot, :, pos]
        kr2 = jnp.sum(krt * krt, axis=0, keepdims=True)
        krg = krt * gkr_ref[...]
        x1, x2 = krg[:ROPE_HALF], krg[ROPE_HALF:]
        cos, sin = cos_ref[:, pos], sin_ref[:, pos]
        rk = jnp.concatenate([x1 * cos - x2 * sin, x2 * cos + x1 * sin], axis=0).astype(BF16)
        sr = _dot(qr_ref[...][:, :QK_ROPE_DIM].astype(BF16), rk)

        rinv = lax.rsqrt((ssq_ref[...] + kr2) / QK_DIM + RMS_EPS)
        s = (sn + sr) * rinv * SM_SCALE
        m_old = m_ref[...]
        m_new = jnp.maximum(m_old, jnp.max(s, axis=-1, keepdims=True))
        p = jnp.exp(s - m_new)
        alpha = jnp.exp(m_old - m_new)
        l_ref[...] = alpha * l_ref[...] + jnp.sum(p, axis=-1, keepdims=True)
        acc_ref[...] = alpha * acc_ref[...] + _dot(p.astype(BF16), cc)
        m_ref[...] = m_new

    @pl.when(c == SA_STEPS - 1)
    def _():
        s_new = jnp.sum(qs_ref[...] * knew_ref[...], axis=-1, keepdims=True) * SM_SCALE
        m_o = m_ref[...]
        m_n = jnp.maximum(m_o, s_new)
        p_new = jnp.exp(s_new - m_n)
        a = jnp.exp(m_o - m_n)
        l = a * l_ref[...] + p_new
        acc = a * acc_ref[...] + p_new * cnew_ref[...]
        o_ref[...] = acc / l


def _attn_sample(page_table, cache_kv, cache_kr, wukt_flat, qa, qr, qs, knew, cnew, gkr, cos_t, sin_t):
    b = qa.shape[0]

    per_seq = pl.BlockSpec((None, N_HEADS, HEAD_PAD), lambda s, c, pt: (s, 0, 0))
    in_specs = [
        pl.BlockSpec(memory_space=pl.ANY), pl.BlockSpec(memory_space=pl.ANY),
        pl.BlockSpec(wukt_flat.shape, lambda s, c, pt: (0, 0)),
        per_seq, per_seq, per_seq, per_seq,
        pl.BlockSpec((None, 1, KV_LORA_RANK), lambda s, c, pt: (s, 0, 0)),
        pl.BlockSpec(gkr.shape, lambda s, c, pt: (0, 0)),
        pl.BlockSpec((ROPE_HALF, SA_POS), lambda s, c, pt: (0, c)),
        pl.BlockSpec((ROPE_HALF, SA_POS), lambda s, c, pt: (0, c)),
    ]
    grid_spec = pltpu.PrefetchScalarGridSpec(
        num_scalar_prefetch=1,
        grid=(b, SA_STEPS),
        in_specs=in_specs,
        out_specs=pl.BlockSpec((None, N_HEADS, KV_LORA_RANK), lambda s, c, pt: (s, 0, 0)),
        scratch_shapes=[
            pltpu.VMEM((2, SA_PAGES, PAGE_SIZE, KV_LORA_RANK), F32),
            pltpu.VMEM((2, QK_ROPE_DIM, SA_POS), F32),
            pltpu.SemaphoreType.DMA((2, 2)),
            pltpu.VMEM((N_HEADS, 1), F32), pltpu.VMEM((N_HEADS, 1), F32),
            pltpu.VMEM((N_HEADS, KV_LORA_RANK), F32), pltpu.VMEM((N_HEADS, SA_PART_POS), F32),
        ],
    )
    return pl.pallas_call(
        _attn_sample_kernel,
        grid_spec=grid_spec,
        out_shape=jax.ShapeDtypeStruct((b, N_HEADS, KV_LORA_RANK), F32),
        compiler_params=_cparams("arbitrary", "arbitrary"),
        name="attn_sample",
    )(page_table, cache_kv, cache_kr, wukt_flat, qa, qr, qs, knew, cnew, gkr, cos_t, sin_t)


def _mla_post_kernel(ol_ref, wuv_ref, wo_ref, y_ref, gate_ref, o_ref):
    parts = [_dot(ol_ref[:, hd * KV_LORA_RANK:(hd + 1) * KV_LORA_RANK], wuv_ref[hd])
             for hd in range(N_HEADS)]
    o = jnp.concatenate(parts, axis=-1).astype(BF16)
    o_ref[...] = y_ref[...] + gate_ref[...] * _dot(o, wo_ref[...])


def _mla_post(o_lat, wuv, wo, y, gate, tm, tiles_per_seq):
    t = y.shape[0]
    tok = lambda n: pl.BlockSpec((tm, n), lambda i: (i, 0))
    return pl.pallas_call(
        _mla_post_kernel,
        grid=(t // tm,),
        in_specs=[tok(N_HEADS * KV_LORA_RANK), _full(wuv.shape), _full(wo.shape), tok(D_MODEL),
                  _mod_spec(gate, tiles_per_seq)],
        out_specs=tok(D_MODEL),
        out_shape=jax.ShapeDtypeStruct((t, D_MODEL), F32),
        compiler_params=_cparams("parallel"),
        name="mla_post",
    )(o_lat, wuv, wo, y, gate)


def _sgu_kernel(y_ref, sh_ref, sc_ref, gate_ref, g1_ref, win_ref, bin_ref, gv_ref, sp_a_ref, sp_b_ref,
                wout_ref, o_ref, v_ref, *, tm, spatial):
    y = y_ref[...]
    h = _rms(y) * g1_ref[...]
    h = h * (1.0 + sc_ref[...]) + sh_ref[...]
    z = _gelu(_dot(h.astype(BF16), win_ref[...]) + bin_ref[...])
    u = z[:, :SG_HALF]
    v = _rms(z[:, SG_HALF:]) * gv_ref[...]
    if spatial:
        r = lax.broadcasted_iota(jnp.int32, (CHUNK, CHUNK), 0)
        cidx = lax.broadcasted_iota(jnp.int32, (CHUNK, CHUNK), 1)
        vb = v.astype(BF16)
        cols = []
        for g in range(SG_GROUPS):
            wc = jnp.where(cidx <= r, sp_a_ref[g], 0.0).astype(BF16)
            rows = [_dot(wc, vb[c * CHUNK:(c + 1) * CHUNK, g * CHUNK:(g + 1) * CHUNK]) + sp_b_ref[g]
                    for c in range(tm // CHUNK)]
            cols.append(jnp.concatenate(rows, axis=0))
        s = jnp.concatenate(cols, axis=-1)
        v_ref[...] = v[tm - CHUNK:, :]
    else:
        s = v * sp_a_ref[...] + sp_b_ref[...]
        v_ref[...] = v
    o = _dot((u * s).astype(BF16), wout_ref[...])
    o_ref[...] = y + gate_ref[...] * o


def _sgu(y, sh, sc, gate, g1, w, sp_a, sp_b, tm, tiles_per_seq, spatial):
    t = y.shape[0]
    tok = lambda n: pl.BlockSpec((tm, n), lambda i: (i, 0))
    if spatial:
        n_seq = t // (tm * tiles_per_seq)
        v_spec = pl.BlockSpec((None, CHUNK, SG_HALF), lambda i: (i // tiles_per_seq, 0, 0))
        v_shape = jax.ShapeDtypeStruct((n_seq, CHUNK, SG_HALF), F32)
    else:
        v_spec = tok(SG_HALF)
        v_shape = jax.ShapeDtypeStruct((t, SG_HALF), F32)
    return pl.pallas_call(
        functools.partial(_sgu_kernel, tm=tm, spatial=spatial),
        grid=(t // tm,),
        in_specs=[tok(D_MODEL), _mod_spec(sh, tiles_per_seq), _mod_spec(sc, tiles_per_seq),
                  _mod_spec(gate, tiles_per_seq), _full((1, D_MODEL)), _full(w["w_in"].shape),
                  _full((1, 2 * SG_HALF)), _full((1, SG_HALF)), _full(sp_a.shape), _full(sp_b.shape),
                  _full(w["w_out"].shape)],
        out_specs=[tok(D_MODEL), v_spec],
        out_shape=[jax.ShapeDtypeStruct((t, D_MODEL), F32), v_shape],
        compiler_params=_cparams("arbitrary"),
        name="sgu",
    )(y, sh, sc, gate, g1, w["w_in"], w["b_in"], w["g_v"], sp_a, sp_b, w["w_out"])


N_TOP = PEER_TOPK + 1
TOP_ROWS = 24
N_CAND = TOP_ROWS + 7 * 8 + (TOP_ROWS - 8)


def _sorting_network(n):
    def merge(lo, hi, r):
        step = r * 2
        if step < hi - lo:
            yield from merge(lo, hi, step)
            yield from merge(lo + r, hi, step)
            yield from [(i, i + r) for i in range(lo + r, hi - r, step)]
        else:
            yield (lo, lo + r)

    def sort(lo, hi):
        if hi - lo >= 1:
            mid = lo + (hi - lo) // 2
            yield from sort(lo, mid)
            yield from sort(mid + 1, hi)
            yield from merge(lo, hi, 1)

    return list(sort(0, n - 1))


def _top_values(x, n, emit):
    nb = x.shape[0] // 8
    v = [x[8 * k:8 * k + 8] for k in range(nb)]
    for i, j in _sorting_network(16):
        if j < nb:
            v[i], v[j] = jnp.maximum(v[i], v[j]), jnp.minimum(v[i], v[j])
    for r in range(n):
        head = jnp.max(v[0], axis=0, keepdims=True)
        emit(r, head)
        hit = v[0] == head
        for k in range(min(nb, n - 1 - r)):
            below = v[k + 1] if k + 1 < nb else NEG_BIG
            v[k] = jnp.where(hit, below, v[k])


def _peer_pre_kernel(y_ref, sh_ref, sc_ref, g2_ref, wq_ref, sk_ref, ht_ref, rank_ref, eb_ref, cnt_ref,
                     cc_ref, s_scr, t_scr, cand_scr, tv_scr):
    h = _rms(y_ref[...]) * g2_ref[...]
    h = h * (1.0 + sc_ref[...]) + sh_ref[...]
    ht_ref[...] = h.T.astype(BF16)
    q = _dot(h.astype(BF16), wq_ref[...])
    for hd in range(PEER_HEADS):
        for side in range(2):
            o = (hd * 2 + side) * N_KEYS
            s_scr[side, hd] = _dot_nt(sk_ref[side], q[:, o:o + N_KEYS].astype(BF16))
    t_scr[...] = jnp.full_like(t_scr, NEG_BIG)

    def tops(idx, _):
        side, hd = idx // PEER_HEADS, idx % PEER_HEADS

        def emit(r, row):
            t_scr[side, hd, r:r + 1, :] = row

        _top_values(s_scr[side, hd], N_TOP, emit)
        return 0

    lax.fori_loop(0, 2 * PEER_HEADS, tops, 0)

    def finish(hd, _):
        t1, t2 = t_scr[0, hd], t_scr[1, hd]
        cand_scr[0:TOP_ROWS] = t1[0:1] + t2
        for a in range(1, 8):
            cand_scr[TOP_ROWS + (a - 1) * 8:TOP_ROWS + a * 8] = t1[a:a + 1] + t2[0:8]
        cand_scr[TOP_ROWS + 56:] = t1[8:] + t2[0:1]
        cand = cand_scr[...]

        def emit(r, row):
            tv_scr[r:r + 1, :] = row

        _top_values(cand, N_TOP, emit)
        tau = 0.5 * (tv_scr[PEER_TOPK - 1:PEER_TOPK, :] + tv_scr[PEER_TOPK:PEER_TOPK + 1, :])
        top = t1[:1] + t2[:1]
        z = jnp.sum(jnp.where(cand >= tau, jnp.exp(cand - top), 0.0), axis=0, keepdims=True)
        s1, s2 = s_scr[0, hd], s_scr[1, hd]
        eb = jnp.exp(s2 - t2[:1])
        th = tau - s1
        cc = jnp.exp(s1 - t1[:1]) * (0.5 / z)
        rank = jnp.zeros_like(s2)
        cnt = jnp.zeros_like(s1)
        for b in range(PEER_TOPK):
            tb = t2[b:b + 1]
            rank = jnp.where(tb > s2, b + 1.0, rank)
            cnt = jnp.where(tb >= th, b + 1.0, cnt)
        for col in range(s1.shape[1] // 128):
            cs = slice(col * 128, (col + 1) * 128)
            rank_ref[hd, col] = rank[:, cs]
            eb_ref[hd, col] = eb[:, cs]
            cnt_ref[hd, col] = cnt[:, cs]
            cc_ref[hd, col] = cc[:, cs]
        return 0

    lax.fori_loop(0, PEER_HEADS, finish, 0)


def _peer_pre(y, sh, sc, g2, wq, sk, tm, tiles_per_seq):
    t = y.shape[0]
    hk = pl.BlockSpec((PEER_HEADS, tm // 128, N_KEYS, 128), lambda i: (0, i, 0, 0))
    hk_shape = jax.ShapeDtypeStruct((PEER_HEADS, t // 128, N_KEYS, 128), F32)
    return pl.pallas_call(
        _peer_pre_kernel,
        grid=(t // tm,),
        in_specs=[pl.BlockSpec((tm, D_MODEL), lambda i: (i, 0)), _mod_spec(sh, tiles_per_seq),
                  _mod_spec(sc, tiles_per_seq), _full((1, D_MODEL)), _full(wq.shape), _full(sk.shape)],
        out_specs=[pl.BlockSpec((D_MODEL, tm), lambda i: (0, i)), hk, hk, hk, hk],
        out_shape=[jax.ShapeDtypeStruct((D_MODEL, t), BF16), hk_shape, hk_shape, hk_shape, hk_shape],
        scratch_shapes=[pltpu.VMEM((2, PEER_HEADS, N_KEYS, tm), F32),
                        pltpu.VMEM((2, PEER_HEADS, TOP_ROWS, tm), F32),
                        pltpu.VMEM((N_CAND, tm), F32),
                        pltpu.VMEM((TOP_ROWS, tm), F32)],
        compiler_params=_cparams("parallel"),
        name="peer_pre",
    )(y, sh, sc, g2, wq, sk)


PEER_EB = 1024
PEER_ROWS = PEER_EB // N_KEYS
PEER_BLOCKS = N_EXPERTS // PEER_EB


BF16_ROWS = 16


def _selection_weights(rank_ref, eb_ref, cnt_ref, cc_ref, col):
    def row_of(ref, hd, ii):
        tile = jnp.broadcast_to(ref[hd, col, ii:ii + 1, :], (BF16_ROWS, 128)).astype(BF16)
        return jnp.concatenate([tile] * (N_KEYS // BF16_ROWS), axis=0)

    ebs = [eb_ref[hd, col].astype(BF16) for hd in range(PEER_HEADS)]
    ranks = [rank_ref[hd, col].astype(BF16) for hd in range(PEER_HEADS)]
    rows = []
    for ii in range(PEER_ROWS):
        w = None
        for hd in range(PEER_HEADS):
            t = jnp.where(ranks[hd] < row_of(cnt_ref, hd, ii), ebs[hd], jnp.zeros_like(ebs[hd]))
            t = t * row_of(cc_ref, hd, ii)
            w = t if w is None else w + t
        rows.append(w)
    return jnp.concatenate(rows, axis=0)


def _peer_dense_kernel(ht_ref, u_ref, vt_ref, rank_ref, eb_ref, cnt_ref, cc_ref, y_ref, gate_ref, o_ref,
                       acc_ref, *, tm):
    e = pl.program_id(1)

    @pl.when(e == 0)
    def _():
        acc_ref[...] = jnp.zeros_like(acc_ref)

    act = _gelu_twice(_dot(u_ref[...], ht_ref[...]))
    w = jnp.concatenate([_selection_weights(rank_ref, eb_ref, cnt_ref, cc_ref, col)
                         for col in range(tm // 128)], axis=1)
    acc_ref[...] += _dot(vt_ref[...], act.astype(BF16) * w)

    @pl.when(e == PEER_BLOCKS - 1)
    def _():
        o_ref[...] = y_ref[...] + gate_ref[...] * acc_ref[...].T


def _peer_dense(ht, u_b, vt_b, layer, rank, eb, cnt, cc, y, gate, tm, tiles_per_seq):
    t = y.shape[0]
    cols = tm // 128
    hk = pl.BlockSpec((PEER_HEADS, cols, N_KEYS, 128), lambda i, e: (0, i, 0, 0))
    hr = pl.BlockSpec((PEER_HEADS, cols, PEER_ROWS, 128), lambda i, e: (0, i, e, 0))
    return pl.pallas_call(
        functools.partial(_peer_dense_kernel, tm=tm),
        grid=(t // tm, PEER_BLOCKS),
        in_specs=[pl.BlockSpec((D_MODEL, tm), lambda i, e: (0, i)),
                  pl.BlockSpec((None, PEER_EB, D_MODEL), lambda i, e: (layer, e, 0)),
                  pl.BlockSpec((None, D_MODEL, PEER_EB), lambda i, e: (layer, 0, e)),
                  hk, hk, hr, hr,
                  pl.BlockSpec((tm, D_MODEL), lambda i, e: (i, 0)),
                  pl.BlockSpec((None,) + gate.shape[1:], lambda i, e: (i // tiles_per_seq, 0, 0))],
        out_specs=pl.BlockSpec((tm, D_MODEL), lambda i, e: (i, 0)),
        out_shape=jax.ShapeDtypeStruct((t, D_MODEL), F32),
        scratch_shapes=[pltpu.VMEM((D_MODEL, tm), F32)],
        compiler_params=_cparams("parallel", "arbitrary"),
        name="peer_dense",
    )(ht, u_b, vt_b, rank, eb, cnt, cc, y, gate)


def _peer(y, sh, sc, gate, g2, wq, sk, u_b, vt_b, layer, tm, tiles_per_seq, dense_tiles=1):
    ht, rank, eb, cnt, cc = _peer_pre(y, sh, sc, g2, wq, sk, tm, tiles_per_seq)
    return _peer_dense(ht, u_b, vt_b, layer, rank, eb, cnt, cc, y, gate, tm * dense_tiles,
                       tiles_per_seq // dense_tiles)


def _pad_heads(w, used):
    w = jnp.pad(w, [(0, 0)] * (w.ndim - 1) + [(0, HEAD_PAD - used)])
    return w.reshape(w.shape[:-2] + (N_HEADS * HEAD_PAD,))


def _pad_gain(g):
    return jnp.pad(g, (0, HEAD_PAD - QK_DIM)).reshape(1, HEAD_PAD)


def _rope_tables(pos):
    inv_freq = ROPE_THETA ** (-jnp.arange(ROPE_HALF, dtype=F32) * 2.0 / QK_ROPE_DIM)
    ang = pos.astype(F32)[:, None] * inv_freq[None, :]
    cos, sin = jnp.cos(ang), jnp.sin(ang)
    t = pos.shape[0]
    one = jnp.ones((t, QK_NOPE_DIM), F32)
    z64 = jnp.zeros((t, QK_NOPE_DIM), F32)
    z16 = jnp.zeros((t, ROPE_HALF), F32)
    z32 = jnp.zeros((t, HEAD_PAD - QK_DIM), F32)
    cos_t = jnp.concatenate([one, cos, cos, z32], axis=1)
    sa = jnp.concatenate([z64, -sin, z16, z32], axis=1)
    sb = jnp.concatenate([z64, z16, sin, z32], axis=1)
    return cos_t, sa, sb, cos, sin


def _mla_weights(w_in, g_q_a, g_kv_a, w_uq, g_q, w_uk):
    lo = Q_LORA_RANK + KV_LORA_RANK
    w_in_p = jnp.concatenate(
        [w_in[:, :lo], jnp.zeros((D_MODEL, QK_NOPE_DIM), F32), w_in[:, lo:],
         jnp.zeros((D_MODEL, HEAD_PAD - QK_DIM), F32)], axis=1).astype(BF16)
    return {
        "w_in": w_in_p,
        "g_q_a": g_q_a.reshape(1, -1),
        "g_kv_a": g_kv_a.reshape(1, -1),
        "w_uq": _pad_heads(w_uq.reshape(Q_LORA_RANK, N_HEADS, QK_DIM), QK_DIM).astype(BF16),
        "g_q": _pad_gain(g_q),
        "w_uk": _pad_heads(w_uk, QK_NOPE_DIM).astype(BF16),
    }


def _split_mod(m, n_prompt):
    m = m.reshape(m.shape[0], 6, D_MODEL)
    mp = [m[:n_prompt, j].reshape(n_prompt, 1, D_MODEL) for j in range(6)]
    ms = [m[n_prompt:, j].reshape(1, -1, D_MODEL) for j in range(6)]
    return mp, ms


def kernel(x_prompt, x_sample, c_prompt, c_sample, cache_kv_latent, cache_k_rope, page_table, norm1_g, norm2_g, w_mod, b_mod, mla_w_in, mla_g_q_a, mla_g_kv_a, mla_w_uq, mla_g_q, mla_w_uk, mla_w_uv, mla_g_k, mla_w_o, sg_w_in, sg_b_in, sg_g_v, sg_w_s, sg_b_s, sg_w_out, peer_w_q, peer_sub_keys, peer_u, peer_v):
    yp = x_prompt.reshape(BATCH * SEQ, D_MODEL)
    ys = x_sample.reshape(DEC_BATCH, D_MODEL)
    mods = _modulation(jnp.concatenate([c_prompt, c_sample], axis=0), w_mod, b_mod)

    tm_p, tps_p = 512, SEQ // 512
    tm_s, tps_s = DEC_BATCH, 1

    u_b = peer_u.astype(BF16)
    vt_b = jnp.swapaxes(peer_v, 1, 2).astype(BF16)

    outs = {}
    for layer in range(DEPTH):
        mp, ms = _split_mod(mods[layer], BATCH)
        g1 = norm1_g[layer].reshape(1, D_MODEL)
        g2 = norm2_g[layer].reshape(1, D_MODEL)
        if layer % 2 == 0:
            a = layer // 2
            w = _mla_weights(mla_w_in[a], mla_g_q_a[a], mla_g_kv_a[a], mla_w_uq[a], mla_g_q[a], mla_w_uk[a])
            gk = _pad_gain(mla_g_k[a])
            w["g_k"] = gk
            wuv = jnp.transpose(mla_w_uv[a], (1, 0, 2)).astype(BF16)
            wo = mla_w_o[a].astype(BF16)
            cos_p, sa_p, sb_p, _, _ = _rope_tables(jnp.arange(SEQ))
            cos_s, sa_s, sb_s, _, _ = _rope_tables(jnp.full((1,), PAST_LEN))
            _, _, _, cos_c, sin_c = _rope_tables(jnp.arange(PAST_LEN))

            w_scaled = dict(w, g_q=w["g_q"] * (SM_SCALE * LOG2_E))
            q, k, ckv_p, krp_p = _mla_pre(yp, mp[0], mp[1], g1, w_scaled, (cos_p, sa_p, sb_p), tm_p, tps_p, True,
                                          BF16)
            o_lat = _attn_prompt(q, k, ckv_p)
            yp = _mla_post(o_lat, wuv, wo, yp, mp[2], tm_p, tps_p)

            q_s, k_s, ckv_s, krp_s = _mla_pre(ys, ms[0], ms[1], g1, w, (cos_s, sa_s, sb_s), tm_s, tps_s, False, F32)
            wukt = jnp.pad(jnp.transpose(mla_w_uk[a], (1, 2, 0)),
                           ((0, 0), (0, HEAD_PAD - QK_NOPE_DIM), (0, 0))).astype(BF16)
            gk_nope = gk * (jnp.arange(HEAD_PAD) < QK_NOPE_DIM)[None, :]
            qa, qr = (jnp.transpose(x, (1, 0, 2)) for x in _absorb(q_s, gk_nope, wukt))
            wukt_flat = jnp.transpose(mla_w_uk[a], (1, 2, 0)).reshape(N_HEADS * QK_NOPE_DIM, KV_LORA_RANK).astype(BF16)
            gkr = mla_g_k[a][QK_NOPE_DIM:].reshape(QK_ROPE_DIM, 1)
            o_lat_s = _attn_sample(
                page_table, cache_kv_latent[a], jnp.swapaxes(cache_k_rope[a], 1, 2), wukt_flat, qa, qr,
                q_s.reshape(DEC_BATCH, N_HEADS, HEAD_PAD), k_s.reshape(DEC_BATCH, N_HEADS, HEAD_PAD),
                ckv_s.reshape(DEC_BATCH, 1, KV_LORA_RANK), gkr, cos_c.T, sin_c.T)
            ys = _mla_post(o_lat_s.reshape(DEC_BATCH, N_HEADS * KV_LORA_RANK).astype(BF16), wuv, wo, ys,
                           ms[2], tm_s, tps_s)

            outs.setdefault("kv_p", []).append(ckv_p.reshape(BATCH, SEQ, KV_LORA_RANK))
            outs.setdefault("kr_p", []).append(krp_p[:, QK_NOPE_DIM:QK_DIM].reshape(BATCH, SEQ, QK_ROPE_DIM))
            outs.setdefault("kv_s", []).append(ckv_s.reshape(DEC_BATCH, 1, KV_LORA_RANK))
            outs.setdefault("kr_s", []).append(krp_s[:, QK_NOPE_DIM:QK_DIM].reshape(DEC_BATCH, 1, QK_ROPE_DIM))
        else:
            bidx = layer // 2
            w = {"w_in": sg_w_in[bidx].astype(BF16), "b_in": sg_b_in[bidx].reshape(1, -1),
                 "g_v": sg_g_v[bidx].reshape(1, -1), "w_out": sg_w_out[bidx].astype(BF16)}
            sp_a = sg_w_s[bidx]
            sp_b = sg_b_s[bidx].reshape(SG_GROUPS, CHUNK, 1)
            yp, vp = _sgu(yp, mp[0], mp[1], mp[2], g1, w, sp_a, sp_b, tm_p, tps_p, True)
            coef = jnp.repeat(sg_w_s[bidx][:, 0, 0], CHUNK).reshape(1, SG_HALF)
            bias = jnp.repeat(sg_b_s[bidx][:, 0], CHUNK).reshape(1, SG_HALF)
            ys, vs = _sgu(ys, ms[0], ms[1], ms[2], g1, w, coef, bias, tm_s, tps_s, False)
            outs.setdefault("v_p", []).append(vp)
            outs.setdefault("v_s", []).append(vs.reshape(DEC_BATCH, 1, SG_HALF))

        wq = peer_w_q[layer].astype(BF16)
        sk = peer_sub_keys[layer].astype(BF16)
        yp = _peer(yp, mp[3], mp[4], mp[5], g2, wq, sk, u_b, vt_b, layer, tm_p, tps_p, dense_tiles=2)
        ys = _peer(ys, ms[3], ms[4], ms[5], g2, wq, sk, u_b, vt_b, layer, tm_s, tps_s)

    return (yp.reshape(BATCH, SEQ, D_MODEL), ys.reshape(DEC_BATCH, 1, D_MODEL),
            jnp.stack(outs["kv_p"]), jnp.stack(outs["kr_p"]), jnp.stack(outs["kv_s"]), jnp.stack(outs["kr_s"]),
            jnp.stack(outs["v_p"]), jnp.stack(outs["v_s"]))
```

```python
import functools

import jax
import jax.numpy as jnp
from jax import lax
from jax.experimental import pallas as pl
from jax.experimental.pallas import tpu as pltpu

F32 = jnp.float32
BF16 = jnp.bfloat16

D_MODEL = 1024
BATCH = 8
SEQ = 2048
DEPTH = 2
DEC_BATCH = 128
PAST_LEN = 16384
PAGE_SIZE = 128
N_PAGES = PAST_LEN // PAGE_SIZE

N_HEADS = 8
Q_LORA_RANK = 256
KV_LORA_RANK = 128
QK_NOPE_DIM = 64
QK_ROPE_DIM = 32
QK_DIM = QK_NOPE_DIM + QK_ROPE_DIM
V_HEAD_DIM = 128
ROPE_THETA = 10000.0
SM_SCALE = QK_DIM ** -0.5
LOG2_E = 1.4426950408889634
HEAD_PAD = 128
ROPE_HALF = QK_ROPE_DIM // 2

CHUNK = 128
SG_HALF = 1536
SG_GROUPS = 12

PEER_HEADS = 8
N_KEYS = 128
N_EXPERTS = N_KEYS * N_KEYS
PEER_TOPK = 16
RMS_EPS = 1e-6

NEG_BIG = -1e30
VMEM_LIMIT = 56 * 1024 * 1024


def _cparams(*sem):
    return pltpu.CompilerParams(dimension_semantics=sem, vmem_limit_bytes=VMEM_LIMIT)


def _rms(x):
    return x * lax.rsqrt(jnp.mean(x * x, axis=-1, keepdims=True) + RMS_EPS)


def _gelu(x):
    cdf = 0.5 * (1.0 + jnp.tanh(0.7978845608028654 * (x + 0.044715 * (x * x * x))))
    return x * cdf


def _gelu_twice(x):
    c = 0.7978845608028654
    return x * (1.0 + jnp.tanh(x * (c + (0.044715 * c) * (x * x))))


def _dot(a, b):
    return jnp.dot(a, b, preferred_element_type=F32)


def _dot_nt(a, b):
    return lax.dot_general(a, b, (((1,), (1,)), ((), ())), preferred_element_type=F32)


def _full(shape):
    n = len(shape)
    return pl.BlockSpec(shape, lambda *_: (0,) * n)


def _mod_spec(arr, tiles_per_seq):
    return pl.BlockSpec((None,) + arr.shape[1:], lambda t, *_: (t // tiles_per_seq, 0, 0))


def _mod_kernel(c_ref, w_ref, b_ref, o_ref):
    c = c_ref[...]
    a = (c * jax.nn.sigmoid(c)).astype(BF16)
    o_ref[...] = _dot(a, w_ref[...].astype(BF16)) + b_ref[...]


def _modulation(c_all, w_mod, b_mod):
    r = c_all.shape[0]
    tn = 1536
    return pl.pallas_call(
        _mod_kernel,
        grid=(DEPTH, 6 * D_MODEL // tn),
        in_specs=[
            pl.BlockSpec((r, D_MODEL), lambda l, n: (0, 0)),
            pl.BlockSpec((None, D_MODEL, tn), lambda l, n: (l, 0, n)),
            pl.BlockSpec((None, 1, tn), lambda l, n: (l, 0, n)),
        ],
        out_specs=pl.BlockSpec((None, r, tn), lambda l, n: (l, 0, n)),
        out_shape=jax.ShapeDtypeStruct((DEPTH, r, 6 * D_MODEL), F32),
        compiler_params=_cparams("parallel", "parallel"),
        name="modulation",
    )(c_all, w_mod, b_mod.reshape(DEPTH, 1, 6 * D_MODEL))


def _rope(x, cos, sa, sb):
    return x * cos + pltpu.roll(x, HEAD_PAD - ROPE_HALF, 1) * sa + pltpu.roll(x, ROPE_HALF, 1) * sb


def _mla_pre_kernel(y_ref, sh_ref, sc_ref, g1_ref, win_ref, gqa_ref, gkva_ref, wuq_ref, gq_ref,
                    wuk_ref, gk_ref, cos_ref, sa_ref, sb_ref, q_ref, k_ref, ckv_ref, kr_ref):
    h = _rms(y_ref[...]) * g1_ref[...]
    h = h * (1.0 + sc_ref[...]) + sh_ref[...]
    z = _dot(h.astype(BF16), win_ref[...])
    cq = _rms(z[:, :Q_LORA_RANK]) * gqa_ref[...]
    ckv = _rms(z[:, Q_LORA_RANK:Q_LORA_RANK + KV_LORA_RANK]) * gkva_ref[...]
    krp = z[:, Q_LORA_RANK + KV_LORA_RANK:]
    ckv_ref[...] = ckv
    kr_ref[...] = krp
    q = _dot(cq.astype(BF16), wuq_ref[...])
    kn = _dot(ckv.astype(BF16), wuk_ref[...])
    cos, sa, sb = cos_ref[...], sa_ref[...], sb_ref[...]
    gq, gk = gq_ref[...], gk_ref[...]
    ones = jnp.ones((HEAD_PAD, HEAD_PAD), BF16)

    def head_norm(x, g):
        ssq = _dot((x * x).astype(BF16), ones)
        return x * lax.rsqrt(ssq / QK_DIM + RMS_EPS) * g

    for hd in range(N_HEADS):
        sl = slice(hd * HEAD_PAD, (hd + 1) * HEAD_PAD)
        q_ref[:, sl] = _rope(head_norm(q[:, sl], gq), cos, sa, sb).astype(q_ref.dtype)
        k_ref[:, sl] = _rope(head_norm(kn[:, sl] + krp, gk), cos, sa, sb).astype(k_ref.dtype)


def _mla_pre(y, sh, sc, g1, w, rope_tabs, tm, tiles_per_seq, rope_tiled, qk_dtype):
    t = y.shape[0]
    cos, sa, sb = rope_tabs
    if rope_tiled:
        rspec = pl.BlockSpec((tm, HEAD_PAD), lambda i: (i % tiles_per_seq, 0))
    else:
        rspec = _full((1, HEAD_PAD))
    wide = N_HEADS * HEAD_PAD
    tok = lambda n: pl.BlockSpec((tm, n), lambda i: (i, 0))
    return pl.pallas_call(
        _mla_pre_kernel,
        grid=(t // tm,),
        in_specs=[
            tok(D_MODEL), _mod_spec(sh, tiles_per_seq), _mod_spec(sc, tiles_per_seq),
            _full((1, D_MODEL)), _full(w["w_in"].shape), _full((1, Q_LORA_RANK)),
            _full((1, KV_LORA_RANK)), _full(w["w_uq"].shape), _full((1, HEAD_PAD)),
            _full(w["w_uk"].shape), _full((1, HEAD_PAD)), rspec, rspec, rspec,
        ],
        out_specs=[tok(wide), tok(wide), tok(KV_LORA_RANK), tok(HEAD_PAD)],
        out_shape=[
            jax.ShapeDtypeStruct((t, wide), qk_dtype),
            jax.ShapeDtypeStruct((t, wide), qk_dtype),
            jax.ShapeDtypeStruct((t, KV_LORA_RANK), F32),
            jax.ShapeDtypeStruct((t, HEAD_PAD), F32),
        ],
        compiler_params=_cparams("parallel"),
        name="mla_pre",
    )(y, sh, sc, g1, w["w_in"], w["g_q_a"], w["g_kv_a"], w["w_uq"], w["g_q"], w["w_uk"], w["g_k"],
      cos, sa, sb)


ATT_TQ = 256
ATT_TK = 256


def _attn_prompt_kernel(q_ref, k_ref, ckv_ref, o_ref, m_ref, acc_ref):
    i = pl.program_id(1)
    m_ref[...] = jnp.full_like(m_ref, NEG_BIG)
    acc_ref[...] = jnp.zeros_like(acc_ref)
    ones = jnp.ones((ATT_TK, KV_LORA_RANK), BF16)

    def block(j, diagonal):
        off = pl.multiple_of(j * ATT_TK, ATT_TK)
        cb = jnp.concatenate([ckv_ref[pl.ds(off, ATT_TK), :].astype(BF16), ones], axis=1)
        for hd in range(N_HEADS):
            sl = slice(hd * HEAD_PAD, (hd + 1) * HEAD_PAD)
            s = _dot_nt(q_ref[:, sl], k_ref[pl.ds(off, ATT_TK), sl])
            if diagonal:
                row = lax.broadcasted_iota(jnp.int32, (ATT_TQ, ATT_TK), 0)
                col = lax.broadcasted_iota(jnp.int32, (ATT_TQ, ATT_TK), 1)
                s = jnp.where(col <= row, s, NEG_BIG)
            m_old = m_ref[hd]
            s_max = jnp.max(jnp.maximum(s[:, :128], s[:, 128:]), axis=-1, keepdims=True)
            m_new = jnp.maximum(m_old, jnp.broadcast_to(s_max, m_old.shape))
            p = jnp.concatenate([jnp.exp2(s[:, :128] - m_new), jnp.exp2(s[:, 128:] - m_new)], axis=1)
            alpha = jnp.exp2(m_old - m_new)
            pv = _dot(p.astype(BF16), cb)
            acc_ref[hd] = jnp.concatenate([alpha, alpha], axis=1) * acc_ref[hd] + pv
            m_ref[hd] = m_new

    def body(j, carry):
        block(j, False)
        return carry

    lax.fori_loop(0, i, body, 0)
    block(i, True)
    for hd in range(N_HEADS):
        acc = acc_ref[hd]
        o_ref[:, hd * HEAD_PAD:(hd + 1) * HEAD_PAD] = (
            acc[:, :KV_LORA_RANK] / acc[:, KV_LORA_RANK:]).astype(o_ref.dtype)


def _attn_prompt(q, k, ckv):
    nq = SEQ // ATT_TQ
    wide = N_HEADS * HEAD_PAD
    return pl.pallas_call(
        _attn_prompt_kernel,
        grid=(BATCH, nq),
        in_specs=[
            pl.BlockSpec((ATT_TQ, wide), lambda b, i: (b * nq + i, 0)),
            pl.BlockSpec((SEQ, wide), lambda b, i: (b, 0)),
            pl.BlockSpec((SEQ, KV_LORA_RANK), lambda b, i: (b, 0)),
        ],
        out_specs=pl.BlockSpec((ATT_TQ, wide), lambda b, i: (b * nq + i, 0)),
        out_shape=jax.ShapeDtypeStruct((BATCH * SEQ, wide), BF16),
        scratch_shapes=[pltpu.VMEM((N_HEADS, ATT_TQ, 128), F32),
                        pltpu.VMEM((N_HEADS, ATT_TQ, 2 * KV_LORA_RANK), F32)],
        compiler_params=_cparams("parallel", "parallel"),
        name="attn_prompt",
    )(q, k, ckv)


SA_PAGES = 64
SA_POS = SA_PAGES * PAGE_SIZE
SA_STEPS = N_PAGES // SA_PAGES
SA_PART_PAGES = 64
SA_PART_POS = SA_PART_PAGES * PAGE_SIZE


def _absorb_kernel(q_ref, gk_ref, wukt_ref, qa_ref, qr_ref):
    gk = gk_ref[...]
    for hd in range(N_HEADS):
        qh = q_ref[:, hd * HEAD_PAD:(hd + 1) * HEAD_PAD]
        qa_ref[hd] = _dot((qh * gk).astype(BF16), wukt_ref[hd])
        qr_ref[hd] = pltpu.roll(qh, HEAD_PAD - QK_NOPE_DIM, 1)


def _absorb(q_s, gk_nope, wukt):
    b = q_s.shape[0]
    shp = jax.ShapeDtypeStruct((N_HEADS, b, HEAD_PAD), F32)
    return pl.pallas_call(
        _absorb_kernel,
        in_specs=[_full(q_s.shape), _full(gk_nope.shape), _full(wukt.shape)],
        out_specs=[_full(shp.shape), _full(shp.shape)],
        out_shape=[shp, shp],
        grid=(1,),
        compiler_params=_cparams("arbitrary"),
        name="absorb_q",
    )(q_s, gk_nope, wukt)


def _attn_sample_kernel(pt_ref, kv_hbm, kr_hbm, wukt_ref, qa_ref, qr_ref, qs_ref, knew_ref, cnew_ref,
                        gkr_ref, cos_ref, sin_ref, o_ref, kv_buf, kr_buf, sems, m_ref, l_ref, acc_ref, ssq_ref):
    s, c = pl.program_id(0), pl.program_id(1)
    g = s * SA_STEPS + c
    last = pl.num_programs(0) * SA_STEPS - 1

    def page_copies(step, slot):
        seq, chunk = step // SA_STEPS, step % SA_STEPS
        for p in range(SA_PAGES):
            page = pt_ref[seq, chunk * SA_PAGES + p]
            yield pltpu.make_async_copy(kv_hbm.at[page], kv_buf.at[slot, p], sems.at[slot, 0])
            yield pltpu.make_async_copy(kr_hbm.at[page], kr_buf.at[slot, :, pl.ds(p * PAGE_SIZE, PAGE_SIZE)],
                                        sems.at[slot, 1])

    @pl.when(g == 0)
    def _():
        for cp in page_copies(g, 0):
            cp.start()

    @pl.when(g < last)
    def _():
        for cp in page_copies(g + 1, (g + 1) % 2):
            cp.start()

    slot = g % 2
    for cp in page_copies(g, slot):
        cp.wait()

    @pl.when(c == 0)
    def _():
        m_ref[...] = jnp.full_like(m_ref, NEG_BIG)
        l_ref[...] = jnp.zeros_like(l_ref)
        acc_ref[...] = jnp.zeros_like(acc_ref)

    qa16 = jnp.concatenate([qa_ref[...], jnp.zeros((8, KV_LORA_RANK), F32)], axis=0).astype(BF16)
    lhs = jnp.concatenate([wukt_ref[...], qa16], axis=0)
    for part in range(SA_PAGES // SA_PART_PAGES):
        pages = slice(part * SA_PART_PAGES, (part + 1) * SA_PART_PAGES)
        pos = slice(part * SA_PART_POS, (part + 1) * SA_PART_POS)
        cc = kv_buf[slot, pages].reshape(SA_PART_POS, KV_LORA_RANK).astype(BF16)
        res = _dot_nt(lhs, cc)
        for hd in range(N_HEADS):
            x = res[hd * QK_NOPE_DIM:(hd + 1) * QK_NOPE_DIM]
            ssq_ref[hd:hd + 1, :] = jnp.sum(x * x, axis=0, keepdims=True)
        sn = res[N_HEADS * QK_NOPE_DIM:N_HEADS * QK_NOPE_DIM + N_HEADS]

        krt = kr_buf[slot, :, pos]
        kr2 = jnp.sum(krt * krt, axis=0, keepdims=True)
        krg = krt * gkr_ref[...]
        x1, x2 = krg[:ROPE_HALF], krg[ROPE_HALF:]
        cos, sin = cos_ref[:, pos], sin_ref[:, pos]
        rk = jnp.concatenate([x1 * cos - x2 * sin, x2 * cos + x1 * sin], axis=0).astype(BF16)
        sr = _dot(qr_ref[...][:, :QK_ROPE_DIM].astype(BF16), rk)

        rinv = lax.rsqrt((ssq_ref[...] + kr2) / QK_DIM + RMS_EPS)
        s = (sn + sr) * rinv * SM_SCALE
        m_old = m_ref[...]
        m_new = jnp.maximum(m_old, jnp.max(s, axis=-1, keepdims=True))
        p = jnp.exp(s - m_new)
        alpha = jnp.exp(m_old - m_new)
        l_ref[...] = alpha * l_ref[...] + jnp.sum(p, axis=-1, keepdims=True)
        acc_ref[...] = alpha * acc_ref[...] + _dot(p.astype(BF16), cc)
        m_ref[...] = m_new

    @pl.when(c == SA_STEPS - 1)
    def _():
        s_new = jnp.sum(qs_ref[...] * knew_ref[...], axis=-1, keepdims=True) * SM_SCALE
        m_o = m_ref[...]
        m_n = jnp.maximum(m_o, s_new)
        p_new = jnp.exp(s_new - m_n)
        a = jnp.exp(m_o - m_n)
        l = a * l_ref[...] + p_new
        acc = a * acc_ref[...] + p_new * cnew_ref[...]
        o_ref[...] = acc / l


def _attn_sample(page_table, cache_kv, cache_kr, wukt_flat, qa, qr, qs, knew, cnew, gkr, cos_t, sin_t):
    b = qa.shape[0]

    per_seq = pl.BlockSpec((None, N_HEADS, HEAD_PAD), lambda s, c, pt: (s, 0, 0))
    in_specs = [
        pl.BlockSpec(memory_space=pl.ANY), pl.BlockSpec(memory_space=pl.ANY),
        pl.BlockSpec(wukt_flat.shape, lambda s, c, pt: (0, 0)),
        per_seq, per_seq, per_seq, per_seq,
        pl.BlockSpec((None, 1, KV_LORA_RANK), lambda s, c, pt: (s, 0, 0)),
        pl.BlockSpec(gkr.shape, lambda s, c, pt: (0, 0)),
        pl.BlockSpec((ROPE_HALF, SA_POS), lambda s, c, pt: (0, c)),
        pl.BlockSpec((ROPE_HALF, SA_POS), lambda s, c, pt: (0, c)),
    ]
    grid_spec = pltpu.PrefetchScalarGridSpec(
        num_scalar_prefetch=1,
        grid=(b, SA_STEPS),
        in_specs=in_specs,
        out_specs=pl.BlockSpec((None, N_HEADS, KV_LORA_RANK), lambda s, c, pt: (s, 0, 0)),
        scratch_shapes=[
            pltpu.VMEM((2, SA_PAGES, PAGE_SIZE, KV_LORA_RANK), F32),
            pltpu.VMEM((2, QK_ROPE_DIM, SA_POS), F32),
            pltpu.SemaphoreType.DMA((2, 2)),
            pltpu.VMEM((N_HEADS, 1), F32), pltpu.VMEM((N_HEADS, 1), F32),
            pltpu.VMEM((N_HEADS, KV_LORA_RANK), F32), pltpu.VMEM((N_HEADS, SA_PART_POS), F32),
        ],
    )
    return pl.pallas_call(
        _attn_sample_kernel,
        grid_spec=grid_spec,
        out_shape=jax.ShapeDtypeStruct((b, N_HEADS, KV_LORA_RANK), F32),
        compiler_params=_cparams("arbitrary", "arbitrary"),
        name="attn_sample",
    )(page_table, cache_kv, cache_kr, wukt_flat, qa, qr, qs, knew, cnew, gkr, cos_t, sin_t)


def _mla_post_kernel(ol_ref, wuv_ref, wo_ref, y_ref, gate_ref, o_ref):
    parts = [_dot(ol_ref[:, hd * KV_LORA_RANK:(hd + 1) * KV_LORA_RANK], wuv_ref[hd])
             for hd in range(N_HEADS)]
    o = jnp.concatenate(parts, axis=-1).astype(BF16)
    o_ref[...] = y_ref[...] + gate_ref[...] * _dot(o, wo_ref[...])


def _mla_post(o_lat, wuv, wo, y, gate, tm, tiles_per_seq):
    t = y.shape[0]
    tok = lambda n: pl.BlockSpec((tm, n), lambda i: (i, 0))
    return pl.pallas_call(
        _mla_post_kernel,
        grid=(t // tm,),
        in_specs=[tok(N_HEADS * KV_LORA_RANK), _full(wuv.shape), _full(wo.shape), tok(D_MODEL),
                  _mod_spec(gate, tiles_per_seq)],
        out_specs=tok(D_MODEL),
        out_shape=jax.ShapeDtypeStruct((t, D_MODEL), F32),
        compiler_params=_cparams("parallel"),
        name="mla_post",
    )(o_lat, wuv, wo, y, gate)


def _sgu_kernel(y_ref, sh_ref, sc_ref, gate_ref, g1_ref, win_ref, bin_ref, gv_ref, sp_a_ref, sp_b_ref,
                wout_ref, o_ref, v_ref, *, tm, spatial):
    y = y_ref[...]
    h = _rms(y) * g1_ref[...]
    h = h * (1.0 + sc_ref[...]) + sh_ref[...]
    z = _gelu(_dot(h.astype(BF16), win_ref[...]) + bin_ref[...])
    u = z[:, :SG_HALF]
    v = _rms(z[:, SG_HALF:]) * gv_ref[...]
    if spatial:
        r = lax.broadcasted_iota(jnp.int32, (CHUNK, CHUNK), 0)
        cidx = lax.broadcasted_iota(jnp.int32, (CHUNK, CHUNK), 1)
        vb = v.astype(BF16)
        cols = []
        for g in range(SG_GROUPS):
            wc = jnp.where(cidx <= r, sp_a_ref[g], 0.0).astype(BF16)
            rows = [_dot(wc, vb[c * CHUNK:(c + 1) * CHUNK, g * CHUNK:(g + 1) * CHUNK]) + sp_b_ref[g]
                    for c in range(tm // CHUNK)]
            cols.append(jnp.concatenate(rows, axis=0))
        s = jnp.concatenate(cols, axis=-1)
        v_ref[...] = v[tm - CHUNK:, :]
    else:
        s = v * sp_a_ref[...] + sp_b_ref[...]
        v_ref[...] = v
    o = _dot((u * s).astype(BF16), wout_ref[...])
    o_ref[...] = y + gate_ref[...] * o


def _sgu(y, sh, sc, gate, g1, w, sp_a, sp_b, tm, tiles_per_seq, spatial):
    t = y.shape[0]
    tok = lambda n: pl.BlockSpec((tm, n), lambda i: (i, 0))
    if spatial:
        n_seq = t // (tm * tiles_per_seq)
        v_spec = pl.BlockSpec((None, CHUNK, SG_HALF), lambda i: (i // tiles_per_seq, 0, 0))
        v_shape = jax.ShapeDtypeStruct((n_seq, CHUNK, SG_HALF), F32)
    else:
        v_spec = tok(SG_HALF)
        v_shape = jax.ShapeDtypeStruct((t, SG_HALF), F32)
    return pl.pallas_call(
        functools.partial(_sgu_kernel, tm=tm, spatial=spatial),
        grid=(t // tm,),
        in_specs=[tok(D_MODEL), _mod_spec(sh, tiles_per_seq), _mod_spec(sc, tiles_per_seq),
                  _mod_spec(gate, tiles_per_seq), _full((1, D_MODEL)), _full(w["w_in"].shape),
                  _full((1, 2 * SG_HALF)), _full((1, SG_HALF)), _full(sp_a.shape), _full(sp_b.shape),
                  _full(w["w_out"].shape)],
        out_specs=[tok(D_MODEL), v_spec],
        out_shape=[jax.ShapeDtypeStruct((t, D_MODEL), F32), v_shape],
        compiler_params=_cparams("arbitrary"),
        name="sgu",
    )(y, sh, sc, gate, g1, w["w_in"], w["b_in"], w["g_v"], sp_a, sp_b, w["w_out"])


N_TOP = PEER_TOPK + 1
TOP_ROWS = 24
N_CAND = TOP_ROWS + 7 * 8 + (TOP_ROWS - 8)


def _sorting_network(n):
    def merge(lo, hi, r):
        step = r * 2
        if step < hi - lo:
            yield from merge(lo, hi, step)
            yield from merge(lo + r, hi, step)
            yield from [(i, i + r) for i in range(lo + r, hi - r, step)]
        else:
            yield (lo, lo + r)

    def sort(lo, hi):
        if hi - lo >= 1:
            mid = lo + (hi - lo) // 2
            yield from sort(lo, mid)
            yield from sort(mid + 1, hi)
            yield from merge(lo, hi, 1)

    return list(sort(0, n - 1))


def _top_values(x, n, emit):
    nb = x.shape[0] // 8
    v = [x[8 * k:8 * k + 8] for k in range(nb)]
    for i, j in _sorting_network(16):
        if j < nb:
            v[i], v[j] = jnp.maximum(v[i], v[j]), jnp.minimum(v[i], v[j])
    for r in range(n):
        head = jnp.max(v[0], axis=0, keepdims=True)
        emit(r, head)
        hit = v[0] == head
        for k in range(min(nb, n - 1 - r)):
            below = v[k + 1] if k + 1 < nb else NEG_BIG
            v[k] = jnp.where(hit, below, v[k])


def _peer_pre_kernel(y_ref, sh_ref, sc_ref, g2_ref, wq_ref, sk_ref, ht_ref, rank_ref, eb_ref, cnt_ref,
                     cc_ref, s_scr, t_scr, cand_scr, tv_scr):
    h = _rms(y_ref[...]) * g2_ref[...]
    h = h * (1.0 + sc_ref[...]) + sh_ref[...]
    ht_ref[...] = h.T.astype(BF16)
    q = _dot(h.astype(BF16), wq_ref[...])
    for hd in range(PEER_HEADS):
        for side in range(2):
            o = (hd * 2 + side) * N_KEYS
            s_scr[side, hd] = _dot_nt(sk_ref[side], q[:, o:o + N_KEYS].astype(BF16))
    t_scr[...] = jnp.full_like(t_scr, NEG_BIG)

    def tops(idx, _):
        side, hd = idx // PEER_HEADS, idx % PEER_HEADS

        def emit(r, row):
            t_scr[side, hd, r:r + 1, :] = row

        _top_values(s_scr[side, hd], N_TOP, emit)
        return 0

    lax.fori_loop(0, 2 * PEER_HEADS, tops, 0)

    def finish(hd, _):
        t1, t2 = t_scr[0, hd], t_scr[1, hd]
        cand_scr[0:TOP_ROWS] = t1[0:1] + t2
        for a in range(1, 8):
            cand_scr[TOP_ROWS + (a - 1) * 8:TOP_ROWS + a * 8] = t1[a:a + 1] + t2[0:8]
        cand_scr[TOP_ROWS + 56:] = t1[8:] + t2[0:1]
        cand = cand_scr[...]

        def emit(r, row):
            tv_scr[r:r + 1, :] = row

        _top_values(cand, N_TOP, emit)
        tau = 0.5 * (tv_scr[PEER_TOPK - 1:PEER_TOPK, :] + tv_scr[PEER_TOPK:PEER_TOPK + 1, :])
        top = t1[:1] + t2[:1]
        z = jnp.sum(jnp.where(cand >= tau, jnp.exp(cand - top), 0.0), axis=0, keepdims=True)
        s1, s2 = s_scr[0, hd], s_scr[1, hd]
        eb = jnp.exp(s2 - t2[:1])
        th = tau - s1
        cc = jnp.exp(s1 - t1[:1]) * (0.5 / z)
        rank = jnp.zeros_like(s2)
        cnt = jnp.zeros_like(s1)
        for b in range(PEER_TOPK):
            tb = t2[b:b + 1]
            rank = jnp.where(tb > s2, b + 1.0, rank)
            cnt = jnp.where(tb >= th, b + 1.0, cnt)
        for col in range(s1.shape[1] // 128):
            cs = slice(col * 128, (col + 1) * 128)
            rank_ref[hd, col] = rank[:, cs]
            eb_ref[hd, col] = eb[:, cs]
            cnt_ref[hd, col] = cnt[:, cs]
            cc_ref[hd, col] = cc[:, cs]
        return 0

    lax.fori_loop(0, PEER_HEADS, finish, 0)


def _peer_pre(y, sh, sc, g2, wq, sk, tm, tiles_per_seq):
    t = y.shape[0]
    hk = pl.BlockSpec((PEER_HEADS, tm // 128, N_KEYS, 128), lambda i: (0, i, 0, 0))
    hk_shape = jax.ShapeDtypeStruct((PEER_HEADS, t // 128, N_KEYS, 128), F32)
    return pl.pallas_call(
        _peer_pre_kernel,
        grid=(t // tm,),
        in_specs=[pl.BlockSpec((tm, D_MODEL), lambda i: (i, 0)), _mod_spec(sh, tiles_per_seq),
                  _mod_spec(sc, tiles_per_seq), _full((1, D_MODEL)), _full(wq.shape), _full(sk.shape)],
        out_specs=[pl.BlockSpec((D_MODEL, tm), lambda i: (0, i)), hk, hk, hk, hk],
        out_shape=[jax.ShapeDtypeStruct((D_MODEL, t), BF16), hk_shape, hk_shape, hk_shape, hk_shape],
        scratch_shapes=[pltpu.VMEM((2, PEER_HEADS, N_KEYS, tm), F32),
                        pltpu.VMEM((2, PEER_HEADS, TOP_ROWS, tm), F32),
                        pltpu.VMEM((N_CAND, tm), F32),
                        pltpu.VMEM((TOP_ROWS, tm), F32)],
        compiler_params=_cparams("parallel"),
        name="peer_pre",
    )(y, sh, sc, g2, wq, sk)


PEER_EB = 1024
PEER_ROWS = PEER_EB // N_KEYS
PEER_BLOCKS = N_EXPERTS // PEER_EB


BF16_ROWS = 16


def _selection_weights(rank_ref, eb_ref, cnt_ref, cc_ref, col):
    def row_of(ref, hd, ii):
        tile = jnp.broadcast_to(ref[hd, col, ii:ii + 1, :], (BF16_ROWS, 128)).astype(BF16)
        return jnp.concatenate([tile] * (N_KEYS // BF16_ROWS), axis=0)

    ebs = [eb_ref[hd, col].astype(BF16) for hd in range(PEER_HEADS)]
    ranks = [rank_ref[hd, col].astype(BF16) for hd in range(PEER_HEADS)]
    rows = []
    for ii in range(PEER_ROWS):
        w = None
        for hd in range(PEER_HEADS):
            t = jnp.where(ranks[hd] < row_of(cnt_ref, hd, ii), ebs[hd], jnp.zeros_like(ebs[hd]))
            t = t * row_of(cc_ref, hd, ii)
            w = t if w is None else w + t
        rows.append(w)
    return jnp.concatenate(rows, axis=0)


def _peer_dense_kernel(ht_ref, u_ref, vt_ref, rank_ref, eb_ref, cnt_ref, cc_ref, y_ref, gate_ref, o_ref,
                       acc_ref, *, tm):
    e = pl.program_id(1)

    @pl.when(e == 0)
    def _():
        acc_ref[...] = jnp.zeros_like(acc_ref)

    act = _gelu_twice(_dot(u_ref[...], ht_ref[...]))
    w = jnp.concatenate([_selection_weights(rank_ref, eb_ref, cnt_ref, cc_ref, col)
                         for col in range(tm // 128)], axis=1)
    acc_ref[...] += _dot(vt_ref[...], act.astype(BF16) * w)

    @pl.when(e == PEER_BLOCKS - 1)
    def _():
        o_ref[...] = y_ref[...] + gate_ref[...] * acc_ref[...].T


def _peer_dense(ht, u_b, vt_b, layer, rank, eb, cnt, cc, y, gate, tm, tiles_per_seq):
    t = y.shape[0]
    cols = tm // 128
    hk = pl.BlockSpec((PEER_HEADS, cols, N_KEYS, 128), lambda i, e: (0, i, 0, 0))
    hr = pl.BlockSpec((PEER_HEADS, cols, PEER_ROWS, 128), lambda i, e: (0, i, e, 0))
    return pl.pallas_call(
        functools.partial(_peer_dense_kernel, tm=tm),
        grid=(t // tm, PEER_BLOCKS),
        in_specs=[pl.BlockSpec((D_MODEL, tm), lambda i, e: (0, i)),
                  pl.BlockSpec((None, PEER_EB, D_MODEL), lambda i, e: (layer, e, 0)),
                  pl.BlockSpec((None, D_MODEL, PEER_EB), lambda i, e: (layer, 0, e)),
                  hk, hk, hr, hr,
                  pl.BlockSpec((tm, D_MODEL), lambda i, e: (i, 0)),
                  pl.BlockSpec((None,) + gate.shape[1:], lambda i, e: (i // tiles_per_seq, 0, 0))],
        out_specs=pl.BlockSpec((tm, D_MODEL), lambda i, e: (i, 0)),
        out_shape=jax.ShapeDtypeStruct((t, D_MODEL), F32),
        scratch_shapes=[pltpu.VMEM((D_MODEL, tm), F32)],
        compiler_params=_cparams("parallel", "arbitrary"),
        name="peer_dense",
    )(ht, u_b, vt_b, rank, eb, cnt, cc, y, gate)


def _peer(y, sh, sc, gate, g2, wq, sk, u_b, vt_b, layer, tm, tiles_per_seq, dense_tiles=1):
    ht, rank, eb, cnt, cc = _peer_pre(y, sh, sc, g2, wq, sk, tm, tiles_per_seq)
    return _peer_dense(ht, u_b, vt_b, layer, rank, eb, cnt, cc, y, gate, tm * dense_tiles,
                       tiles_per_seq // dense_tiles)


def _pad_heads(w, used):
    w = jnp.pad(w, [(0, 0)] * (w.ndim - 1) + [(0, HEAD_PAD - used)])
    return w.reshape(w.shape[:-2] + (N_HEADS * HEAD_PAD,))


def _pad_gain(g):
    return jnp.pad(g, (0, HEAD_PAD - QK_DIM)).reshape(1, HEAD_PAD)


def _rope_tables(pos):
    inv_freq = ROPE_THETA ** (-jnp.arange(ROPE_HALF, dtype=F32) * 2.0 / QK_ROPE_DIM)
    ang = pos.astype(F32)[:, None] * inv_freq[None, :]
    cos, sin = jnp.cos(ang), jnp.sin(ang)
    t = pos.shape[0]
    one = jnp.ones((t, QK_NOPE_DIM), F32)
    z64 = jnp.zeros((t, QK_NOPE_DIM), F32)
    z16 = jnp.zeros((t, ROPE_HALF), F32)
    z32 = jnp.zeros((t, HEAD_PAD - QK_DIM), F32)
    cos_t = jnp.concatenate([one, cos, cos, z32], axis=1)
    sa = jnp.concatenate([z64, -sin, z16, z32], axis=1)
    sb = jnp.concatenate([z64, z16, sin, z32], axis=1)
    return cos_t, sa, sb, cos, sin


def _mla_weights(w_in, g_q_a, g_kv_a, w_uq, g_q, w_uk):
    lo = Q_LORA_RANK + KV_LORA_RANK
    w_in_p = jnp.concatenate(
        [w_in[:, :lo], jnp.zeros((D_MODEL, QK_NOPE_DIM), F32), w_in[:, lo:],
         jnp.zeros((D_MODEL, HEAD_PAD - QK_DIM), F32)], axis=1).astype(BF16)
    return {
        "w_in": w_in_p,
        "g_q_a": g_q_a.reshape(1, -1),
        "g_kv_a": g_kv_a.reshape(1, -1),
        "w_uq": _pad_heads(w_uq.reshape(Q_LORA_RANK, N_HEADS, QK_DIM), QK_DIM).astype(BF16),
        "g_q": _pad_gain(g_q),
        "w_uk": _pad_heads(w_uk, QK_NOPE_DIM).astype(BF16),
    }


def _split_mod(m, n_prompt):
    m = m.reshape(m.shape[0], 6, D_MODEL)
    mp = [m[:n_prompt, j].reshape(n_prompt, 1, D_MODEL) for j in range(6)]
    ms = [m[n_prompt:, j].reshape(1, -1, D_MODEL) for j in range(6)]
    return mp, ms


def kernel(x_prompt, x_sample, c_prompt, c_sample, cache_kv_latent, cache_k_rope, page_table, norm1_g, norm2_g, w_mod, b_mod, mla_w_in, mla_g_q_a, mla_g_kv_a, mla_w_uq, mla_g_q, mla_w_uk, mla_w_uv, mla_g_k, mla_w_o, sg_w_in, sg_b_in, sg_g_v, sg_w_s, sg_b_s, sg_w_out, peer_w_q, peer_sub_keys, peer_u, peer_v):
    yp = x_prompt.reshape(BATCH * SEQ, D_MODEL)
    ys = x_sample.reshape(DEC_BATCH, D_MODEL)
    mods = _modulation(jnp.concatenate([c_prompt, c_sample], axis=0), w_mod, b_mod)

    tm_p, tps_p = 512, SEQ // 512
    tm_s, tps_s = DEC_BATCH, 1

    u_b = peer_u.astype(BF16)
    vt_b = jnp.swapaxes(peer_v, 1, 2).astype(BF16)

    outs = {}
    for layer in range(DEPTH):
        mp, ms = _split_mod(mods[layer], BATCH)
        g1 = norm1_g[layer].reshape(1, D_MODEL)
        g2 = norm2_g[layer].reshape(1, D_MODEL)
        if layer % 2 == 0:
            a = layer // 2
            w = _mla_weights(mla_w_in[a], mla_g_q_a[a], mla_g_kv_a[a], mla_w_uq[a], mla_g_q[a], mla_w_uk[a])
            gk = _pad_gain(mla_g_k[a])
            w["g_k"] = gk
            wuv = jnp.transpose(mla_w_uv[a], (1, 0, 2)).astype(BF16)
            wo = mla_w_o[a].astype(BF16)
            cos_p, sa_p, sb_p, _, _ = _rope_tables(jnp.arange(SEQ))
            cos_s, sa_s, sb_s, _, _ = _rope_tables(jnp.full((1,), PAST_LEN))
            _, _, _, cos_c, sin_c = _rope_tables(jnp.arange(PAST_LEN))

            w_scaled = dict(w, g_q=w["g_q"] * (SM_SCALE * LOG2_E))
            q, k, ckv_p, krp_p = _mla_pre(yp, mp[0], mp[1], g1, w_scaled, (cos_p, sa_p, sb_p), tm_p, tps_p, True,
                                          BF16)
            o_lat = _attn_prompt(q, k, ckv_p)
            yp = _mla_post(o_lat, wuv, wo, yp, mp[2], tm_p, tps_p)

            q_s, k_s, ckv_s, krp_s = _mla_pre(ys, ms[0], ms[1], g1, w, (cos_s, sa_s, sb_s), tm_s, tps_s, False, F32)
            wukt = jnp.pad(jnp.transpose(mla_w_uk[a], (1, 2, 0)),
                           ((0, 0), (0, HEAD_PAD - QK_NOPE_DIM), (0, 0))).astype(BF16)
            gk_nope = gk * (jnp.arange(HEAD_PAD) < QK_NOPE_DIM)[None, :]
            qa, qr = (jnp.transpose(x, (1, 0, 2)) for x in _absorb(q_s, gk_nope, wukt))
            wukt_flat = jnp.transpose(mla_w_uk[a], (1, 2, 0)).reshape(N_HEADS * QK_NOPE_DIM, KV_LORA_RANK).astype(BF16)
            gkr = mla_g_k[a][QK_NOPE_DIM:].reshape(QK_ROPE_DIM, 1)
            o_lat_s = _attn_sample(
                page_table, cache_kv_latent[a], jnp.swapaxes(cache_k_rope[a], 1, 2), wukt_flat, qa, qr,
                q_s.reshape(DEC_BATCH, N_HEADS, HEAD_PAD), k_s.reshape(DEC_BATCH, N_HEADS, HEAD_PAD),
                ckv_s.reshape(DEC_BATCH, 1, KV_LORA_RANK), gkr, cos_c.T, sin_c.T)
            ys = _mla_post(o_lat_s.reshape(DEC_BATCH, N_HEADS * KV_LORA_RANK).astype(BF16), wuv, wo, ys,
                           ms[2], tm_s, tps_s)

            outs.setdefault("kv_p", []).append(ckv_p.reshape(BATCH, SEQ, KV_LORA_RANK))
            outs.setdefault("kr_p", []).append(krp_p[:, QK_NOPE_DIM:QK_DIM].reshape(BATCH, SEQ, QK_ROPE_DIM))
            outs.setdefault("kv_s", []).append(ckv_s.reshape(DEC_BATCH, 1, KV_LORA_RANK))
            outs.setdefault("kr_s", []).append(krp_s[:, QK_NOPE_DIM:QK_DIM].reshape(DEC_BATCH, 1, QK_ROPE_DIM))
        else:
            bidx = layer // 2
            w = {"w_in": sg_w_in[bidx].astype(BF16), "b_in": sg_b_in[bidx].reshape(1, -1),
                 "g_v": sg_g_v[bidx].reshape(1, -1), "w_out": sg_w_out[bidx].astype(BF16)}
            sp_a = sg_w_s[bidx]
            sp_b = sg_b_s[bidx].reshape(SG_GROUPS, CHUNK, 1)
            yp, vp = _sgu(yp, mp[0], mp[1], mp[2], g1, w, sp_a, sp_b, tm_p, tps_p, True)
            coef = jnp.repeat(sg_w_s[bidx][:, 0, 0], CHUNK).reshape(1, SG_HALF)
            bias = jnp.repeat(sg_b_s[bidx][:, 0], CHUNK).reshape(1, SG_HALF)
            ys, vs = _sgu(ys, ms[0], ms[1], ms[2], g1, w, coef, bias, tm_s, tps_s, False)
            outs.setdefault("v_p", []).append(vp)
            outs.setdefault("v_s", []).append(vs.reshape(DEC_BATCH, 1, SG_HALF))

        wq = peer_w_q[layer].astype(BF16)
        sk = peer_sub_keys[layer].astype(BF16)
        yp = _peer(yp, mp[3], mp[4], mp[5], g2, wq, sk, u_b, vt_b, layer, tm_p, tps_p, dense_tiles=2)
        ys = _peer(ys, ms[3], ms[4], ms[5], g2, wq, sk, u_b, vt_b, layer, tm_s, tps_s)

    return (yp.reshape(BATCH, SEQ, D_MODEL), ys.reshape(DEC_BATCH, 1, D_MODEL),
            jnp.stack(outs["kv_p"]), jnp.stack(outs["kr_p"]), jnp.stack(outs["kv_s"]), jnp.stack(outs["kr_s"]),
            jnp.stack(outs["v_p"]), jnp.stack(outs["v_s"]))
```

```python
import functools

import jax
import jax.numpy as jnp
from jax import lax
from jax.experimental import pallas as pl
from jax.experimental.pallas import tpu as pltpu

F32 = jnp.float32
BF16 = jnp.bfloat16

D_MODEL = 1024
BATCH = 8
SEQ = 2048
DEPTH = 2
DEC_BATCH = 128
PAST_LEN = 16384
PAGE_SIZE = 128
N_PAGES = PAST_LEN // PAGE_SIZE

N_HEADS = 8
Q_LORA_RANK = 256
KV_LORA_RANK = 128
QK_NOPE_DIM = 64
QK_ROPE_DIM = 32
QK_DIM = QK_NOPE_DIM + QK_ROPE_DIM
V_HEAD_DIM = 128
ROPE_THETA = 10000.0
SM_SCALE = QK_DIM ** -0.5
LOG2_E = 1.4426950408889634
HEAD_PAD = 128
ROPE_HALF = QK_ROPE_DIM // 2

CHUNK = 128
SG_HALF = 1536
SG_GROUPS = 12

PEER_HEADS = 8
N_KEYS = 128
N_EXPERTS = N_KEYS * N_KEYS
PEER_TOPK = 16
RMS_EPS = 1e-6

NEG_BIG = -1e30
VMEM_LIMIT = 56 * 1024 * 1024


def _cparams(*sem):
    return pltpu.CompilerParams(dimension_semantics=sem, vmem_limit_bytes=VMEM_LIMIT)


def _rms(x):
    return x * lax.rsqrt(jnp.mean(x * x, axis=-1, keepdims=True) + RMS_EPS)


def _gelu(x):
    cdf = 0.5 * (1.0 + jnp.tanh(0.7978845608028654 * (x + 0.044715 * (x * x * x))))
    return x * cdf


def _gelu_twice(x):
    c = 0.7978845608028654
    return x * (1.0 + jnp.tanh(x * (c + (0.044715 * c) * (x * x))))


def _dot(a, b):
    return jnp.dot(a, b, preferred_element_type=F32)


def _dot_nt(a, b):
    return lax.dot_general(a, b, (((1,), (1,)), ((), ())), preferred_element_type=F32)


def _full(shape):
    n = len(shape)
    return pl.BlockSpec(shape, lambda *_: (0,) * n)


def _mod_spec(arr, tiles_per_seq):
    return pl.BlockSpec((None,) + arr.shape[1:], lambda t, *_: (t // tiles_per_seq, 0, 0))


def _mod_kernel(c_ref, w_ref, b_ref, o_ref):
    c = c_ref[...]
    a = (c * jax.nn.sigmoid(c)).astype(BF16)
    o_ref[...] = _dot(a, w_ref[...].astype(BF16)) + b_ref[...]


def _modulation(c_all, w_mod, b_mod):
    r = c_all.shape[0]
    tn = 1536
    return pl.pallas_call(
        _mod_kernel,
        grid=(DEPTH, 6 * D_MODEL // tn),
        in_specs=[
            pl.BlockSpec((r, D_MODEL), lambda l, n: (0, 0)),
            pl.BlockSpec((None, D_MODEL, tn), lambda l, n: (l, 0, n)),
            pl.BlockSpec((None, 1, tn), lambda l, n: (l, 0, n)),
        ],
        out_specs=pl.BlockSpec((None, r, tn), lambda l, n: (l, 0, n)),
        out_shape=jax.ShapeDtypeStruct((DEPTH, r, 6 * D_MODEL), F32),
        compiler_params=_cparams("parallel", "parallel"),
        name="modulation",
    )(c_all, w_mod, b_mod.reshape(DEPTH, 1, 6 * D_MODEL))


def _rope(x, cos, sa, sb):
    return x * cos + pltpu.roll(x, HEAD_PAD - ROPE_HALF, 1) * sa + pltpu.roll(x, ROPE_HALF, 1) * sb


def _mla_pre_kernel(y_ref, sh_ref, sc_ref, g1_ref, win_ref, gqa_ref, gkva_ref, wuq_ref, gq_ref,
                    wuk_ref, gk_ref, cos_ref, sa_ref, sb_ref, q_ref, k_ref, ckv_ref, kr_ref):
    h = _rms(y_ref[...]) * g1_ref[...]
    h = h * (1.0 + sc_ref[...]) + sh_ref[...]
    z = _dot(h.astype(BF16), win_ref[...])
    cq = _rms(z[:, :Q_LORA_RANK]) * gqa_ref[...]
    ckv = _rms(z[:, Q_LORA_RANK:Q_LORA_RANK + KV_LORA_RANK]) * gkva_ref[...]
    krp = z[:, Q_LORA_RANK + KV_LORA_RANK:]
    ckv_ref[...] = ckv
    kr_ref[...] = krp
    q = _dot(cq.astype(BF16), wuq_ref[...])
    kn = _dot(ckv.astype(BF16), wuk_ref[...])
    cos, sa, sb = cos_ref[...], sa_ref[...], sb_ref[...]
    gq, gk = gq_ref[...], gk_ref[...]
    ones = jnp.ones((HEAD_PAD, HEAD_PAD), BF16)

    def head_norm(x, g):
        ssq = _dot((x * x).astype(BF16), ones)
        return x * lax.rsqrt(ssq / QK_DIM + RMS_EPS) * g

    for hd in range(N_HEADS):
        sl = slice(hd * HEAD_PAD, (hd + 1) * HEAD_PAD)
        q_ref[:, sl] = _rope(head_norm(q[:, sl], gq), cos, sa, sb).astype(q_ref.dtype)
        k_ref[:, sl] = _rope(head_norm(kn[:, sl] + krp, gk), cos, sa, sb).astype(k_ref.dtype)


def _mla_pre(y, sh, sc, g1, w, rope_tabs, tm, tiles_per_seq, rope_tiled, qk_dtype):
    t = y.shape[0]
    cos, sa, sb = rope_tabs
    if rope_tiled:
        rspec = pl.BlockSpec((tm, HEAD_PAD), lambda i: (i % tiles_per_seq, 0))
    else:
        rspec = _full((1, HEAD_PAD))
    wide = N_HEADS * HEAD_PAD
    tok = lambda n: pl.BlockSpec((tm, n), lambda i: (i, 0))
    return pl.pallas_call(
        _mla_pre_kernel,
        grid=(t // tm,),
        in_specs=[
            tok(D_MODEL), _mod_spec(sh, tiles_per_seq), _mod_spec(sc, tiles_per_seq),
            _full((1, D_MODEL)), _full(w["w_in"].shape), _full((1, Q_LORA_RANK)),
            _full((1, KV_LORA_RANK)), _full(w["w_uq"].shape), _full((1, HEAD_PAD)),
            _full(w["w_uk"].shape), _full((1, HEAD_PAD)), rspec, rspec, rspec,
        ],
        out_specs=[tok(wide), tok(wide), tok(KV_LORA_RANK), tok(HEAD_PAD)],
        out_shape=[
            jax.ShapeDtypeStruct((t, wide), qk_dtype),
            jax.ShapeDtypeStruct((t, wide), qk_dtype),
            jax.ShapeDtypeStruct((t, KV_LORA_RANK), F32),
            jax.ShapeDtypeStruct((t, HEAD_PAD), F32),
        ],
        compiler_params=_cparams("parallel"),
        name="mla_pre",
    )(y, sh, sc, g1, w["w_in"], w["g_q_a"], w["g_kv_a"], w["w_uq"], w["g_q"], w["w_uk"], w["g_k"],
      cos, sa, sb)


ATT_TQ = 256
ATT_TK = 256


def _attn_prompt_kernel(q_ref, k_ref, ckv_ref, o_ref, m_ref, acc_ref):
    i = pl.program_id(1)
    m_ref[...] = jnp.full_like(m_ref, NEG_BIG)
    acc_ref[...] = jnp.zeros_like(acc_ref)
    ones = jnp.ones((ATT_TK, KV_LORA_RANK), BF16)

    def block(j, diagonal):
        off = pl.multiple_of(j * ATT_TK, ATT_TK)
        cb = jnp.concatenate([ckv_ref[pl.ds(off, ATT_TK), :].astype(BF16), ones], axis=1)
        for hd in range(N_HEADS):
            sl = slice(hd * HEAD_PAD, (hd + 1) * HEAD_PAD)
            s = _dot_nt(q_ref[:, sl], k_ref[pl.ds(off, ATT_TK), sl])
            if diagonal:
                row = lax.broadcasted_iota(jnp.int32, (ATT_TQ, ATT_TK), 0)
                col = lax.broadcasted_iota(jnp.int32, (ATT_TQ, ATT_TK), 1)
                s = jnp.where(col <= row, s, NEG_BIG)
            m_old = m_ref[hd]
            s_max = jnp.max(jnp.maximum(s[:, :128], s[:, 128:]), axis=-1, keepdims=True)
            m_new = jnp.maximum(m_old, jnp.broadcast_to(s_max, m_old.shape))
            p = jnp.concatenate([jnp.exp2(s[:, :128] - m_new), jnp.exp2(s[:, 128:] - m_new)], axis=1)
            alpha = jnp.exp2(m_old - m_new)
            pv = _dot(p.astype(BF16), cb)
            acc_ref[hd] = jnp.concatenate([alpha, alpha], axis=1) * acc_ref[hd] + pv
            m_ref[hd] = m_new

    def body(j, carry):
        block(j, False)
        return carry

    lax.fori_loop(0, i, body, 0)
    block(i, True)
    for hd in range(N_HEADS):
        acc = acc_ref[hd]
        o_ref[:, hd * HEAD_PAD:(hd + 1) * HEAD_PAD] = (
            acc[:, :KV_LORA_RANK] / acc[:, KV_LORA_RANK:]).astype(o_ref.dtype)


def _attn_prompt(q, k, ckv):
    nq = SEQ // ATT_TQ
    wide = N_HEADS * HEAD_PAD
    return pl.pallas_call(
        _attn_prompt_kernel,
        grid=(BATCH, nq),
        in_specs=[
            pl.BlockSpec((ATT_TQ, wide), lambda b, i: (b * nq + i, 0)),
            pl.BlockSpec((SEQ, wide), lambda b, i: (b, 0)),
            pl.BlockSpec((SEQ, KV_LORA_RANK), lambda b, i: (b, 0)),
        ],
        out_specs=pl.BlockSpec((ATT_TQ, wide), lambda b, i: (b * nq + i, 0)),
        out_shape=jax.ShapeDtypeStruct((BATCH * SEQ, wide), BF16),
        scratch_shapes=[pltpu.VMEM((N_HEADS, ATT_TQ, 128), F32),
                        pltpu.VMEM((N_HEADS, ATT_TQ, 2 * KV_LORA_RANK), F32)],
        compiler_params=_cparams("parallel", "parallel"),
        name="attn_prompt",
    )(q, k, ckv)


SA_PAGES = 64
SA_POS = SA_PAGES * PAGE_SIZE
SA_STEPS = N_PAGES // SA_PAGES
SA_PART_PAGES = 64
SA_PART_POS = SA_PART_PAGES * PAGE_SIZE


def _absorb_kernel(q_ref, gk_ref, wukt_ref, qa_ref, qr_ref):
    gk = gk_ref[...]
    for hd in range(N_HEADS):
        qh = q_ref[:, hd * HEAD_PAD:(hd + 1) * HEAD_PAD]
        qa_ref[hd] = _dot((qh * gk).astype(BF16), wukt_ref[hd])
        qr_ref[hd] = pltpu.roll(qh, HEAD_PAD - QK_NOPE_DIM, 1)


def _absorb(q_s, gk_nope, wukt):
    b = q_s.shape[0]
    shp = jax.ShapeDtypeStruct((N_HEADS, b, HEAD_PAD), F32)
    return pl.pallas_call(
        _absorb_kernel,
        in_specs=[_full(q_s.shape), _full(gk_nope.shape), _full(wukt.shape)],
        out_specs=[_full(shp.shape), _full(shp.shape)],
        out_shape=[shp, shp],
        grid=(1,),
        compiler_params=_cparams("arbitrary"),
        name="absorb_q",
    )(q_s, gk_nope, wukt)


def _attn_sample_kernel(pt_ref, kv_hbm, kr_hbm, wukt_ref, qa_ref, qr_ref, qs_ref, knew_ref, cnew_ref,
                        gkr_ref, cos_ref, sin_ref, o_ref, kv_buf, kr_buf, sems, m_ref, l_ref, acc_ref, ssq_ref):
    s, c = pl.program_id(0), pl.program_id(1)
    g = s * SA_STEPS + c
    last = pl.num_programs(0) * SA_STEPS - 1

    def page_copies(step, slot):
        seq, chunk = step // SA_STEPS, step % SA_STEPS
        for p in range(SA_PAGES):
            page = pt_ref[seq, chunk * SA_PAGES + p]
            yield pltpu.make_async_copy(kv_hbm.at[page], kv_buf.at[slot, p], sems.at[slot, 0])
            yield pltpu.make_async_copy(kr_hbm.at[page], kr_buf.at[slot, :, pl.ds(p * PAGE_SIZE, PAGE_SIZE)],
                                        sems.at[slot, 1])

    @pl.when(g == 0)
    def _():
        for cp in page_copies(g, 0):
            cp.start()

    @pl.when(g < last)
    def _():
        for cp in page_copies(g + 1, (g + 1) % 2):
            cp.start()

    slot = g % 2
    for cp in page_copies(g, slot):
        cp.wait()

    @pl.when(c == 0)
    def _():
        m_ref[...] = jnp.full_like(m_ref, NEG_BIG)
        l_ref[...] = jnp.zeros_like(l_ref)
        acc_ref[...] = jnp.zeros_like(acc_ref)

    qa16 = jnp.concatenate([qa_ref[...], jnp.zeros((8, KV_LORA_RANK), F32)], axis=0).astype(BF16)
    lhs = jnp.concatenate([wukt_ref[...], qa16], axis=0)
    for part in range(SA_PAGES // SA_PART_PAGES):
        pages = slice(part * SA_PART_PAGES, (part + 1) * SA_PART_PAGES)
        pos = slice(part * SA_PART_POS, (part + 1) * SA_PART_POS)
        cc = kv_buf[slot, pages].reshape(SA_PART_POS, KV_LORA_RANK).astype(BF16)
        res = _dot_nt(lhs, cc)
        for hd in range(N_HEADS):
            x = res[hd * QK_NOPE_DIM:(hd + 1) * QK_NOPE_DIM]
            ssq_ref[hd:hd + 1, :] = jnp.sum(x * x, axis=0, keepdims=True)
        sn = res[N_HEADS * QK_NOPE_DIM:N_HEADS * QK_NOPE_DIM + N_HEADS]

        krt = kr_buf[slot, :, pos]
        kr2 = jnp.sum(krt * krt, axis=0, keepdims=True)
        krg = krt * gkr_ref[...]
        x1, x2 = krg[:ROPE_HALF], krg[ROPE_HALF:]
        cos, sin = cos_ref[:, pos], sin_ref[:, pos]
        rk = jnp.concatenate([x1 * cos - x2 * sin, x2 * cos + x1 * sin], axis=0).astype(BF16)
        sr = _dot(qr_ref[...][:, :QK_ROPE_DIM].astype(BF16), rk)

        rinv = lax.rsqrt((ssq_ref[...] + kr2) / QK_DIM + RMS_EPS)
        s = (sn + sr) * rinv * SM_SCALE
        m_old = m_ref[...]
        m_new = jnp.maximum(m_old, jnp.max(s, axis=-1, keepdims=True))
        p = jnp.exp(s - m_new)
        alpha = jnp.exp(m_old - m_new)
        l_ref[...] = alpha * l_ref[...] + jnp.sum(p, axis=-1, keepdims=True)
        acc_ref[...] = alpha * acc_ref[...] + _dot(p.astype(BF16), cc)
        m_ref[...] = m_new

    @pl.when(c == SA_STEPS - 1)
    def _():
        s_new = jnp.sum(qs_ref[...] * knew_ref[...], axis=-1, keepdims=True) * SM_SCALE
        m_o = m_ref[...]
        m_n = jnp.maximum(m_o, s_new)
        p_new = jnp.exp(s_new - m_n)
        a = jnp.exp(m_o - m_n)
        l = a * l_ref[...] + p_new
        acc = a * acc_ref[...] + p_new * cnew_ref[...]
        o_ref[...] = acc / l


def _attn_sample(page_table, cache_kv, cache_kr, wukt_flat, qa, qr, qs, knew, cnew, gkr, cos_t, sin_t):
    b = qa.shape[0]

    per_seq = pl.BlockSpec((None, N_HEADS, HEAD_PAD), lambda s, c, pt: (s, 0, 0))
    in_specs = [
        pl.BlockSpec(memory_space=pl.ANY), pl.BlockSpec(memory_space=pl.ANY),
        pl.BlockSpec(wukt_flat.shape, lambda s, c, pt: (0, 0)),
        per_seq, per_seq, per_seq, per_seq,
        pl.BlockSpec((None, 1, KV_LORA_RANK), lambda s, c, pt: (s, 0, 0)),
        pl.BlockSpec(gkr.shape, lambda s, c, pt: (0, 0)),
        pl.BlockSpec((ROPE_HALF, SA_POS), lambda s, c, pt: (0, c)),
        pl.BlockSpec((ROPE_HALF, SA_POS), lambda s, c, pt: (0, c)),
    ]
    grid_spec = pltpu.PrefetchScalarGridSpec(
        num_scalar_prefetch=1,
        grid=(b, SA_STEPS),
        in_specs=in_specs,
        out_specs=pl.BlockSpec((None, N_HEADS, KV_LORA_RANK), lambda s, c, pt: (s, 0, 0)),
        scratch_shapes=[
            pltpu.VMEM((2, SA_PAGES, PAGE_SIZE, KV_LORA_RANK), F32),
            pltpu.VMEM((2, QK_ROPE_DIM, SA_POS), F32),
            pltpu.SemaphoreType.DMA((2, 2)),
            pltpu.VMEM((N_HEADS, 1), F32), pltpu.VMEM((N_HEADS, 1), F32),
            pltpu.VMEM((N_HEADS, KV_LORA_RANK), F32), pltpu.VMEM((N_HEADS, SA_PART_POS), F32),
        ],
    )
    return pl.pallas_call(
        _attn_sample_kernel,
        grid_spec=grid_spec,
        out_shape=jax.ShapeDtypeStruct((b, N_HEADS, KV_LORA_RANK), F32),
        compiler_params=_cparams("arbitrary", "arbitrary"),
        name="attn_sample",
    )(page_table, cache_kv, cache_kr, wukt_flat, qa, qr, qs, knew, cnew, gkr, cos_t, sin_t)


def _fold_vo_kernel(wuv_ref, wo_ref, o_ref):
    for hd in range(N_HEADS):
        rows = slice(hd * V_HEAD_DIM, (hd + 1) * V_HEAD_DIM)
        o_ref[hd * KV_LORA_RANK:(hd + 1) * KV_LORA_RANK, :] = _dot(wuv_ref[hd], wo_ref[rows, :]).astype(o_ref.dtype)


def _fold_vo(wuv, wo):
    shape = (N_HEADS * KV_LORA_RANK, D_MODEL)
    return pl.pallas_call(
        _fold_vo_kernel,
        grid=(1,),
        in_specs=[_full(wuv.shape), _full(wo.shape)],
        out_specs=_full(shape),
        out_shape=jax.ShapeDtypeStruct(shape, BF16),
        compiler_params=_cparams("arbitrary"),
        name="mla_fold_vo",
    )(wuv, wo)


def _mla_post_kernel(ol_ref, wvo_ref, y_ref, gate_ref, o_ref):
    o_ref[...] = y_ref[...] + gate_ref[...] * _dot(ol_ref[...], wvo_ref[...])


def _mla_post(o_lat, wvo, y, gate, tm, tiles_per_seq):
    t = y.shape[0]
    tok = lambda n: pl.BlockSpec((tm, n), lambda i: (i, 0))
    return pl.pallas_call(
        _mla_post_kernel,
        grid=(t // tm,),
        in_specs=[tok(N_HEADS * KV_LORA_RANK), _full(wvo.shape), tok(D_MODEL), _mod_spec(gate, tiles_per_seq)],
        out_specs=tok(D_MODEL),
        out_shape=jax.ShapeDtypeStruct((t, D_MODEL), F32),
        compiler_params=_cparams("parallel"),
        name="mla_post",
    )(o_lat, wvo, y, gate)


def _sgu_kernel(y_ref, sh_ref, sc_ref, gate_ref, g1_ref, win_ref, bin_ref, gv_ref, sp_a_ref, sp_b_ref,
                wout_ref, o_ref, v_ref, *, tm, spatial):
    y = y_ref[...]
    h = _rms(y) * g1_ref[...]
    h = h * (1.0 + sc_ref[...]) + sh_ref[...]
    z = _gelu(_dot(h.astype(BF16), win_ref[...]) + bin_ref[...])
    u = z[:, :SG_HALF]
    v = _rms(z[:, SG_HALF:]) * gv_ref[...]
    if spatial:
        r = lax.broadcasted_iota(jnp.int32, (CHUNK, CHUNK), 0)
        cidx = lax.broadcasted_iota(jnp.int32, (CHUNK, CHUNK), 1)
        vb = v.astype(BF16)
        cols = []
        for g in range(SG_GROUPS):
            wc = jnp.where(cidx <= r, sp_a_ref[g], 0.0).astype(BF16)
            rows = [_dot(wc, vb[c * CHUNK:(c + 1) * CHUNK, g * CHUNK:(g + 1) * CHUNK]) + sp_b_ref[g]
                    for c in range(tm // CHUNK)]
            cols.append(jnp.concatenate(rows, axis=0))
        s = jnp.concatenate(cols, axis=-1)
        v_ref[...] = v[tm - CHUNK:, :]
    else:
        s = v * sp_a_ref[...] + sp_b_ref[...]
        v_ref[...] = v
    o = _dot((u * s).astype(BF16), wout_ref[...])
    o_ref[...] = y + gate_ref[...] * o


def _sgu(y, sh, sc, gate, g1, w, sp_a, sp_b, tm, tiles_per_seq, spatial):
    t = y.shape[0]
    tok = lambda n: pl.BlockSpec((tm, n), lambda i: (i, 0))
    if spatial:
        n_seq = t // (tm * tiles_per_seq)
        v_spec = pl.BlockSpec((None, CHUNK, SG_HALF), lambda i: (i // tiles_per_seq, 0, 0))
        v_shape = jax.ShapeDtypeStruct((n_seq, CHUNK, SG_HALF), F32)
    else:
        v_spec = tok(SG_HALF)
        v_shape = jax.ShapeDtypeStruct((t, SG_HALF), F32)
    return pl.pallas_call(
        functools.partial(_sgu_kernel, tm=tm, spatial=spatial),
        grid=(t // tm,),
        in_specs=[tok(D_MODEL), _mod_spec(sh, tiles_per_seq), _mod_spec(sc, tiles_per_seq),
                  _mod_spec(gate, tiles_per_seq), _full((1, D_MODEL)), _full(w["w_in"].shape),
                  _full((1, 2 * SG_HALF)), _full((1, SG_HALF)), _full(sp_a.shape), _full(sp_b.shape),
                  _full(w["w_out"].shape)],
        out_specs=[tok(D_MODEL), v_spec],
        out_shape=[jax.ShapeDtypeStruct((t, D_MODEL), F32), v_shape],
        compiler_params=_cparams("arbitrary"),
        name="sgu",
    )(y, sh, sc, gate, g1, w["w_in"], w["b_in"], w["g_v"], sp_a, sp_b, w["w_out"])


N_TOP = PEER_TOPK + 1
TOP_ROWS = 24
N_CAND = TOP_ROWS + 7 * 8 + (TOP_ROWS - 8)


def _sorting_network(n):
    def merge(lo, hi, r):
        step = r * 2
        if step < hi - lo:
            yield from merge(lo, hi, step)
            yield from merge(lo + r, hi, step)
            yield from [(i, i + r) for i in range(lo + r, hi - r, step)]
        else:
            yield (lo, lo + r)

    def sort(lo, hi):
        if hi - lo >= 1:
            mid = lo + (hi - lo) // 2
            yield from sort(lo, mid)
            yield from sort(mid + 1, hi)
            yield from merge(lo, hi, 1)

    return list(sort(0, n - 1))


def _top_values(x, n, emit):
    nb = x.shape[0] // 8
    v = [x[8 * k:8 * k + 8] for k in range(nb)]
    for i, j in _sorting_network(16):
        if j < nb:
            v[i], v[j] = jnp.maximum(v[i], v[j]), jnp.minimum(v[i], v[j])
    for r in range(n):
        head = jnp.max(v[0], axis=0, keepdims=True)
        emit(r, head)
        hit = v[0] == head
        for k in range(min(nb, n - 1 - r)):
            below = v[k + 1] if k + 1 < nb else NEG_BIG
            v[k] = jnp.where(hit, below, v[k])


def _peer_pre_kernel(y_ref, sh_ref, sc_ref, g2_ref, wq_ref, sk_ref, ht_ref, rank_ref, eb_ref, cnt_ref,
                     cc_ref, s_scr, t_scr, cand_scr, tv_scr):
    h = _rms(y_ref[...]) * g2_ref[...]
    h = h * (1.0 + sc_ref[...]) + sh_ref[...]
    ht_ref[...] = h.T.astype(BF16)
    q = _dot(h.astype(BF16), wq_ref[...])
    for hd in range(PEER_HEADS):
        for side in range(2):
            o = (hd * 2 + side) * N_KEYS
            s_scr[side, hd] = _dot_nt(sk_ref[side], q[:, o:o + N_KEYS].astype(BF16))
    t_scr[...] = jnp.full_like(t_scr, NEG_BIG)

    def tops(idx, _):
        side, hd = idx // PEER_HEADS, idx % PEER_HEADS

        def emit(r, row):
            t_scr[side, hd, r:r + 1, :] = row

        _top_values(s_scr[side, hd], N_TOP, emit)
        return 0

    lax.fori_loop(0, 2 * PEER_HEADS, tops, 0)

    def finish(hd, _):
        t1, t2 = t_scr[0, hd], t_scr[1, hd]
        cand_scr[0:TOP_ROWS] = t1[0:1] + t2
        for a in range(1, 8):
            cand_scr[TOP_ROWS + (a - 1) * 8:TOP_ROWS + a * 8] = t1[a:a + 1] + t2[0:8]
        cand_scr[TOP_ROWS + 56:] = t1[8:] + t2[0:1]
        cand = cand_scr[...]

        def emit(r, row):
            tv_scr[r:r + 1, :] = row

        _top_values(cand, N_TOP, emit)
        tau = 0.5 * (tv_scr[PEER_TOPK - 1:PEER_TOPK, :] + tv_scr[PEER_TOPK:PEER_TOPK + 1, :])
        top = t1[:1] + t2[:1]
        z = jnp.sum(jnp.where(cand >= tau, jnp.exp(cand - top), 0.0), axis=0, keepdims=True)
        s1, s2 = s_scr[0, hd], s_scr[1, hd]
        eb = jnp.exp(s2 - t2[:1])
        th = tau - s1
        cc = jnp.exp(s1 - t1[:1]) * (0.5 / z)
        rank = jnp.zeros_like(s2)
        cnt = jnp.zeros_like(s1)
        for b in range(PEER_TOPK):
            tb = t2[b:b + 1]
            rank = jnp.where(tb > s2, b + 1.0, rank)
            cnt = jnp.where(tb >= th, b + 1.0, cnt)
        for col in range(s1.shape[1] // 128):
            cs = slice(col * 128, (col + 1) * 128)
            rank_ref[hd, col] = rank[:, cs]
            eb_ref[hd, col] = eb[:, cs]
            cnt_ref[hd, col] = cnt[:, cs]
            cc_ref[hd, col] = cc[:, cs]
        return 0

    lax.fori_loop(0, PEER_HEADS, finish, 0)


def _peer_pre(y, sh, sc, g2, wq, sk, tm, tiles_per_seq):
    t = y.shape[0]
    hk = pl.BlockSpec((PEER_HEADS, tm // 128, N_KEYS, 128), lambda i: (0, i, 0, 0))
    hk_shape = jax.ShapeDtypeStruct((PEER_HEADS, t // 128, N_KEYS, 128), F32)
    return pl.pallas_call(
        _peer_pre_kernel,
        grid=(t // tm,),
        in_specs=[pl.BlockSpec((tm, D_MODEL), lambda i: (i, 0)), _mod_spec(sh, tiles_per_seq),
                  _mod_spec(sc, tiles_per_seq), _full((1, D_MODEL)), _full(wq.shape), _full(sk.shape)],
        out_specs=[pl.BlockSpec((D_MODEL, tm), lambda i: (0, i)), hk, hk, hk, hk],
        out_shape=[jax.ShapeDtypeStruct((D_MODEL, t), BF16), hk_shape, hk_shape, hk_shape, hk_shape],
        scratch_shapes=[pltpu.VMEM((2, PEER_HEADS, N_KEYS, tm), F32),
                        pltpu.VMEM((2, PEER_HEADS, TOP_ROWS, tm), F32),
                        pltpu.VMEM((N_CAND, tm), F32),
                        pltpu.VMEM((TOP_ROWS, tm), F32)],
        compiler_params=_cparams("parallel"),
        name="peer_pre",
    )(y, sh, sc, g2, wq, sk)


PEER_EB = 1024
PEER_ROWS = PEER_EB // N_KEYS
PEER_BLOCKS = N_EXPERTS // PEER_EB


BF16_ROWS = 16


def _selection_weights(rank_ref, eb_ref, cnt_ref, cc_ref, col):
    def row_of(ref, hd, ii):
        tile = jnp.broadcast_to(ref[hd, col, ii:ii + 1, :], (BF16_ROWS, 128)).astype(BF16)
        return jnp.concatenate([tile] * (N_KEYS // BF16_ROWS), axis=0)

    ebs = [eb_ref[hd, col].astype(BF16) for hd in range(PEER_HEADS)]
    ranks = [rank_ref[hd, col].astype(BF16) for hd in range(PEER_HEADS)]
    rows = []
    for ii in range(PEER_ROWS):
        w = None
        for hd in range(PEER_HEADS):
            t = jnp.where(ranks[hd] < row_of(cnt_ref, hd, ii), ebs[hd], jnp.zeros_like(ebs[hd]))
            t = t * row_of(cc_ref, hd, ii)
            w = t if w is None else w + t
        rows.append(w)
    return jnp.concatenate(rows, axis=0)


def _peer_dense_kernel(ht_ref, u_ref, vt_ref, rank_ref, eb_ref, cnt_ref, cc_ref, y_ref, gate_ref, o_ref,
                       acc_ref, *, tm):
    e = pl.program_id(1)

    @pl.when(e == 0)
    def _():
        acc_ref[...] = jnp.zeros_like(acc_ref)

    act = _gelu_twice(_dot(u_ref[...], ht_ref[...]))
    w = jnp.concatenate([_selection_weights(rank_ref, eb_ref, cnt_ref, cc_ref, col)
                         for col in range(tm // 128)], axis=1)
    acc_ref[...] += _dot(vt_ref[...], act.astype(BF16) * w)

    @pl.when(e == PEER_BLOCKS - 1)
    def _():
        o_ref[...] = y_ref[...] + gate_ref[...] * acc_ref[...].T


def _peer_dense(ht, u_b, vt_b, layer, rank, eb, cnt, cc, y, gate, tm, tiles_per_seq):
    t = y.shape[0]
    cols = tm // 128
    hk = pl.BlockSpec((PEER_HEADS, cols, N_KEYS, 128), lambda i, e: (0, i, 0, 0))
    hr = pl.BlockSpec((PEER_HEADS, cols, PEER_ROWS, 128), lambda i, e: (0, i, e, 0))
    return pl.pallas_call(
        functools.partial(_peer_dense_kernel, tm=tm),
        grid=(t // tm, PEER_BLOCKS),
        in_specs=[pl.BlockSpec((D_MODEL, tm), lambda i, e: (0, i)),
                  pl.BlockSpec((None, PEER_EB, D_MODEL), lambda i, e: (layer, e, 0)),
                  pl.BlockSpec((None, D_MODEL, PEER_EB), lambda i, e: (layer, 0, e)),
                  hk, hk, hr, hr,
                  pl.BlockSpec((tm, D_MODEL), lambda i, e: (i, 0)),
                  pl.BlockSpec((None,) + gate.shape[1:], lambda i, e: (i // tiles_per_seq, 0, 0))],
        out_specs=pl.BlockSpec((tm, D_MODEL), lambda i, e: (i, 0)),
        out_shape=jax.ShapeDtypeStruct((t, D_MODEL), F32),
        scratch_shapes=[pltpu.VMEM((D_MODEL, tm), F32)],
        compiler_params=_cparams("parallel", "arbitrary"),
        name="peer_dense",
    )(ht, u_b, vt_b, rank, eb, cnt, cc, y, gate)


def _peer(y, sh, sc, gate, g2, wq, sk, u_b, vt_b, layer, tm, tiles_per_seq, dense_tiles=1):
    ht, rank, eb, cnt, cc = _peer_pre(y, sh, sc, g2, wq, sk, tm, tiles_per_seq)
    return _peer_dense(ht, u_b, vt_b, layer, rank, eb, cnt, cc, y, gate, tm * dense_tiles,
                       tiles_per_seq // dense_tiles)


def _pad_heads(w, used):
    w = jnp.pad(w, [(0, 0)] * (w.ndim - 1) + [(0, HEAD_PAD - used)])
    return w.reshape(w.shape[:-2] + (N_HEADS * HEAD_PAD,))


def _pad_gain(g):
    return jnp.pad(g, (0, HEAD_PAD - QK_DIM)).reshape(1, HEAD_PAD)


def _rope_tables(pos):
    inv_freq = ROPE_THETA ** (-jnp.arange(ROPE_HALF, dtype=F32) * 2.0 / QK_ROPE_DIM)
    ang = pos.astype(F32)[:, None] * inv_freq[None, :]
    cos, sin = jnp.cos(ang), jnp.sin(ang)
    t = pos.shape[0]
    one = jnp.ones((t, QK_NOPE_DIM), F32)
    z64 = jnp.zeros((t, QK_NOPE_DIM), F32)
    z16 = jnp.zeros((t, ROPE_HALF), F32)
    z32 = jnp.zeros((t, HEAD_PAD - QK_DIM), F32)
    cos_t = jnp.concatenate([one, cos, cos, z32], axis=1)
    sa = jnp.concatenate([z64, -sin, z16, z32], axis=1)
    sb = jnp.concatenate([z64, z16, sin, z32], axis=1)
    return cos_t, sa, sb, cos, sin


def _mla_weights(w_in, g_q_a, g_kv_a, w_uq, g_q, w_uk):
    lo = Q_LORA_RANK + KV_LORA_RANK
    w_in_p = jnp.concatenate(
        [w_in[:, :lo], jnp.zeros((D_MODEL, QK_NOPE_DIM), F32), w_in[:, lo:],
         jnp.zeros((D_MODEL, HEAD_PAD - QK_DIM), F32)], axis=1).astype(BF16)
    return {
        "w_in": w_in_p,
        "g_q_a": g_q_a.reshape(1, -1),
        "g_kv_a": g_kv_a.reshape(1, -1),
        "w_uq": _pad_heads(w_uq.reshape(Q_LORA_RANK, N_HEADS, QK_DIM), QK_DIM).astype(BF16),
        "g_q": _pad_gain(g_q),
        "w_uk": _pad_heads(w_uk, QK_NOPE_DIM).astype(BF16),
    }


def _split_mod(m, n_prompt):
    m = m.reshape(m.shape[0], 6, D_MODEL)
    mp = [m[:n_prompt, j].reshape(n_prompt, 1, D_MODEL) for j in range(6)]
    ms = [m[n_prompt:, j].reshape(1, -1, D_MODEL) for j in range(6)]
    return mp, ms


def kernel(x_prompt, x_sample, c_prompt, c_sample, cache_kv_latent, cache_k_rope, page_table, norm1_g, norm2_g, w_mod, b_mod, mla_w_in, mla_g_q_a, mla_g_kv_a, mla_w_uq, mla_g_q, mla_w_uk, mla_w_uv, mla_g_k, mla_w_o, sg_w_in, sg_b_in, sg_g_v, sg_w_s, sg_b_s, sg_w_out, peer_w_q, peer_sub_keys, peer_u, peer_v):
    yp = x_prompt.reshape(BATCH * SEQ, D_MODEL)
    ys = x_sample.reshape(DEC_BATCH, D_MODEL)
    mods = _modulation(jnp.concatenate([c_prompt, c_sample], axis=0), w_mod, b_mod)

    tm_p, tps_p = 512, SEQ // 512
    tm_s, tps_s = DEC_BATCH, 1

    u_b = peer_u.astype(BF16)
    vt_b = jnp.swapaxes(peer_v, 1, 2).astype(BF16)

    outs = {}
    for layer in range(DEPTH):
        mp, ms = _split_mod(mods[layer], BATCH)
        g1 = norm1_g[layer].reshape(1, D_MODEL)
        g2 = norm2_g[layer].reshape(1, D_MODEL)
        if layer % 2 == 0:
            a = layer // 2
            w = _mla_weights(mla_w_in[a], mla_g_q_a[a], mla_g_kv_a[a], mla_w_uq[a], mla_g_q[a], mla_w_uk[a])
            gk = _pad_gain(mla_g_k[a])
            w["g_k"] = gk
            wuv = jnp.transpose(mla_w_uv[a], (1, 0, 2)).astype(BF16)
            wvo = _fold_vo(wuv, mla_w_o[a].astype(BF16))
            cos_p, sa_p, sb_p, _, _ = _rope_tables(jnp.arange(SEQ))
            cos_s, sa_s, sb_s, _, _ = _rope_tables(jnp.full((1,), PAST_LEN))
            _, _, _, cos_c, sin_c = _rope_tables(jnp.arange(PAST_LEN))

            w_scaled = dict(w, g_q=w["g_q"] * (SM_SCALE * LOG2_E))
            q, k, ckv_p, krp_p = _mla_pre(yp, mp[0], mp[1], g1, w_scaled, (cos_p, sa_p, sb_p), tm_p, tps_p, True,
                                          BF16)
            o_lat = _attn_prompt(q, k, ckv_p)
            yp = _mla_post(o_lat, wvo, yp, mp[2], tm_p, tps_p)

            q_s, k_s, ckv_s, krp_s = _mla_pre(ys, ms[0], ms[1], g1, w, (cos_s, sa_s, sb_s), tm_s, tps_s, False, F32)
            wukt = jnp.pad(jnp.transpose(mla_w_uk[a], (1, 2, 0)),
                           ((0, 0), (0, HEAD_PAD - QK_NOPE_DIM), (0, 0))).astype(BF16)
            gk_nope = gk * (jnp.arange(HEAD_PAD) < QK_NOPE_DIM)[None, :]
            qa, qr = (jnp.transpose(x, (1, 0, 2)) for x in _absorb(q_s, gk_nope, wukt))
            wukt_flat = jnp.transpose(mla_w_uk[a], (1, 2, 0)).reshape(N_HEADS * QK_NOPE_DIM, KV_LORA_RANK).astype(BF16)
            gkr = mla_g_k[a][QK_NOPE_DIM:].reshape(QK_ROPE_DIM, 1)
            o_lat_s = _attn_sample(
                page_table, cache_kv_latent[a], jnp.swapaxes(cache_k_rope[a], 1, 2), wukt_flat, qa, qr,
                q_s.reshape(DEC_BATCH, N_HEADS, HEAD_PAD), k_s.reshape(DEC_BATCH, N_HEADS, HEAD_PAD),
                ckv_s.reshape(DEC_BATCH, 1, KV_LORA_RANK), gkr, cos_c.T, sin_c.T)
            ys = _mla_post(o_lat_s.reshape(DEC_BATCH, N_HEADS * KV_LORA_RANK).astype(BF16), wvo, ys,
                           ms[2], tm_s, tps_s)

            outs.setdefault("kv_p", []).append(ckv_p.reshape(BATCH, SEQ, KV_LORA_RANK))
            outs.setdefault("kr_p", []).append(krp_p[:, QK_NOPE_DIM:QK_DIM].reshape(BATCH, SEQ, QK_ROPE_DIM))
            outs.setdefault("kv_s", []).append(ckv_s.reshape(DEC_BATCH, 1, KV_LORA_RANK))
            outs.setdefault("kr_s", []).append(krp_s[:, QK_NOPE_DIM:QK_DIM].reshape(DEC_BATCH, 1, QK_ROPE_DIM))
        else:
            bidx = layer // 2
            w = {"w_in": sg_w_in[bidx].astype(BF16), "b_in": sg_b_in[bidx].reshape(1, -1),
                 "g_v": sg_g_v[bidx].reshape(1, -1), "w_out": sg_w_out[bidx].astype(BF16)}
            sp_a = sg_w_s[bidx]
            sp_b = sg_b_s[bidx].reshape(SG_GROUPS, CHUNK, 1)
            yp, vp = _sgu(yp, mp[0], mp[1], mp[2], g1, w, sp_a, sp_b, tm_p, tps_p, True)
            coef = jnp.repeat(sg_w_s[bidx][:, 0, 0], CHUNK).reshape(1, SG_HALF)
            bias = jnp.repeat(sg_b_s[bidx][:, 0], CHUNK).reshape(1, SG_HALF)
            ys, vs = _sgu(ys, ms[0], ms[1], ms[2], g1, w, coef, bias, tm_s, tps_s, False)
            outs.setdefault("v_p", []).append(vp)
            outs.setdefault("v_s", []).append(vs.reshape(DEC_BATCH, 1, SG_HALF))

        wq = peer_w_q[layer].astype(BF16)
        sk = peer_sub_keys[layer].astype(BF16)
        yp = _peer(yp, mp[3], mp[4], mp[5], g2, wq, sk, u_b, vt_b, layer, tm_p, tps_p, dense_tiles=2)
        ys = _peer(ys, ms[3], ms[4], ms[5], g2, wq, sk, u_b, vt_b, layer, tm_s, tps_s)

    return (yp.reshape(BATCH, SEQ, D_MODEL), ys.reshape(DEC_BATCH, 1, D_MODEL),
            jnp.stack(outs["kv_p"]), jnp.stack(outs["kr_p"]), jnp.stack(outs["kv_s"]), jnp.stack(outs["kr_s"]),
            jnp.stack(outs["v_p"]), jnp.stack(outs["v_s"]))
```
